```python
import jax, jax.numpy as jnp
from jax import lax
import numpy as np

D_MODEL = 2048
BATCH = 16
SEQ = 256
DEPTH = 4
DEC_BATCH = 8
DEC_SEQ = 4096
PAST_LEN = 256

GRID_W = 64
N_HEADS = 8
QK_NOPE = 128
QK_ROPE = 64
V_DIM = 128
QK_DIM = QK_NOPE + QK_ROPE
Q_LORA = 512
KV_LORA = 256
ROPE_BASE = 10000.0
Q_BLOCK = 128
LRU_WIDTH = 1024
LRU_BLOCKS = 16
LRU_BLOCK = LRU_WIDTH // LRU_BLOCKS
CONV_W = 4
CONV_LEFT = 2
LRU_C = 8.0
POOL_WIDTH = 1024
POOL_GROUPS = 4
POOL_GROUP = POOL_WIDTH // POOL_GROUPS
POOL_WINDOWS = (2, 4, 8, 16)
N_BRANCH = 3
D_FF = 4 * D_MODEL
N_MOD = 6
EPS = 1e-6
IN_SPLITS = (Q_LORA, KV_LORA, QK_ROPE, LRU_WIDTH, LRU_WIDTH, POOL_WIDTH, N_BRANCH * D_MODEL)
D_IN = Q_LORA + KV_LORA + QK_ROPE + 2 * LRU_WIDTH + POOL_WIDTH + N_BRANCH * D_MODEL

kernel_name = "hybrid_mla_rglru_pool_diffusion_step"


def rms_norm(x, w):
    xf = x.astype(jnp.float32)
    y = xf * lax.rsqrt(jnp.mean(xf * xf, axis=-1, keepdims=True) + EPS)
    return (y * w.astype(jnp.float32)).astype(x.dtype)


def modulation(cond, w_mod, b_mod):
    m = jax.nn.silu(cond) @ w_mod + b_mod
    return jnp.split(m[:, None, :], N_MOD, axis=-1)


def split_cols(z):
    idx = tuple(np.cumsum(IN_SPLITS)[:-1].tolist())
    return jnp.split(z, idx, axis=-1)


def axial_rope(rows):
    row = jnp.repeat(jnp.arange(rows), GRID_W).astype(jnp.float32)
    col = jnp.tile(jnp.arange(GRID_W), rows).astype(jnp.float32)
    n_freq = QK_ROPE // 4
    inv = ROPE_BASE ** (-(jnp.arange(n_freq, dtype=jnp.float32) / n_freq))
    ang = jnp.concatenate([row[:, None] * inv, col[:, None] * inv], axis=-1)
    return jnp.cos(ang), jnp.sin(ang)


def apply_rope(x, rope):
    cos, sin = rope
    cos = cos[None, :, None, :].astype(x.dtype)
    sin = sin[None, :, None, :].astype(x.dtype)
    nope, pe = x[..., :QK_NOPE], x[..., QK_NOPE:]
    p1, p2 = pe[..., :QK_ROPE // 2], pe[..., QK_ROPE // 2:]
    return jnp.concatenate([nope, p1 * cos - p2 * sin, p1 * sin + p2 * cos], axis=-1)


def mla_queries(q_lat, lp, rope):
    B, T, _ = q_lat.shape
    q = (rms_norm(q_lat, lp["q_a_norm"]) @ lp["w_qb"]).reshape(B, T, N_HEADS, QK_DIM)
    q = rms_norm(q, lp["q_norm"])
    if rope is not None:
        q = apply_rope(q, rope)
    return q


def mla_keys(ckv, k_rope, lp, rope):
    B, T, _ = ckv.shape
    kv = (ckv @ lp["w_kvb"]).reshape(B, T, N_HEADS, QK_NOPE + V_DIM)
    k_pe = jnp.broadcast_to(k_rope[:, :, None, :], (B, T, N_HEADS, QK_ROPE))
    k = rms_norm(jnp.concatenate([kv[..., :QK_NOPE], k_pe], axis=-1), lp["k_norm"])
    if rope is not None:
        k = apply_rope(k, rope)
    return k, kv[..., QK_NOPE:]


def block_attention(q, k, v):
    B, T = q.shape[:2]
    nb = T // Q_BLOCK
    qb = q.reshape(B, nb, Q_BLOCK, N_HEADS, QK_DIM).transpose(1, 0, 2, 3, 4)
    scale = QK_DIM ** -0.5

    def one_block(qblk):
        s = jnp.einsum("bqhd,bkhd->bhqk", qblk, k).astype(jnp.float32) * scale
        p = jax.nn.softmax(s, axis=-1).astype(v.dtype)
        return jnp.einsum("bhqk,bkhd->bqhd", p, v)

    o = lax.map(one_block, qb)
    return o.transpose(1, 0, 2, 3, 4).reshape(B, T, N_HEADS * V_DIM)


def centred_conv(u, w, b):
    T = u.shape[1]
    up = jnp.pad(u, ((0, 0), (CONV_LEFT, CONV_W - 1 - CONV_LEFT), (0, 0)))
    y = b
    for tap in range(CONV_W):
        y = y + w[tap] * up[:, tap:tap + T]
    return y


def block_diag_linear(x, w, b=None):
    nb, bs_in, bs_out = w.shape
    xs = x.reshape(x.shape[:-1] + (nb, bs_in))
    y = jnp.einsum("btnd,nde->btne", xs, w).reshape(x.shape[:-1] + (nb * bs_out,))
    return y if b is None else y + b


def _lin_combine(e1, e2):
    a1, b1 = e1
    a2, b2 = e2
    return a1 * a2, a2 * b1 + b2


def rglru(xc, wa, ba, wx, bx, lam, h0, reverse):
    r = jax.nn.sigmoid(block_diag_linear(xc, wa, ba).astype(jnp.float32))
    i = jax.nn.sigmoid(block_diag_linear(xc, wx, bx).astype(jnp.float32))
    log_a = -LRU_C * r * jax.nn.softplus(-lam.astype(jnp.float32))
    a = jnp.exp(log_a)
    b = jnp.sqrt(-jnp.expm1(2.0 * log_a)) * (i * xc.astype(jnp.float32))
    if reverse:
        a, b = a[:, ::-1], b[:, ::-1]
    b = b.at[:, 0].add(a[:, 0] * h0)
    _, h = lax.associative_scan(_lin_combine, (a, b), axis=1)
    if reverse:
        h = h[:, ::-1]
    return h.astype(xc.dtype)


def pool_mixer(u, w_pool, pool_scale):
    B, T, _ = u.shape
    uf = u.astype(jnp.float32)
    csum = jnp.pad(jnp.cumsum(uf, axis=1), ((0, 0), (1, 0), (0, 0)))
    t = np.arange(T)
    outs = []
    for g, win in enumerate(POOL_WINDOWS):
        lo = np.clip(t - win // 2, 0, T)
        hi = np.clip(t + win // 2, 0, T)
        sl = slice(g * POOL_GROUP, (g + 1) * POOL_GROUP)
        cg = csum[..., sl]
        mean = (cg[:, hi] - cg[:, lo]) / (hi - lo).astype(np.float32)[None, :, None]
        outs.append(mean - uf[..., sl])
    d = jnp.concatenate(outs, axis=-1).astype(u.dtype)
    return block_diag_linear(d, w_pool) * pool_scale


def trunk_layer(x, cond, lp, rope, ctx):
    B, T, _ = x.shape
    sh1, sc1, g1, sh2, sc2, g2 = modulation(cond, lp["w_mod"], lp["b_mod"])
    h = rms_norm(x, lp["norm1"]) * (1 + sc1) + sh1
    q_lat, kv_lat, k_rope, u_lru, u_gate, u_pool, gate_logits = split_cols(h @ lp["w_in"])

    ckv = rms_norm(kv_lat, lp["kv_a_norm"])
    q = mla_queries(q_lat, lp, rope)
    k, v = mla_keys(ckv, k_rope, lp, rope)
    if ctx is None:
        h0 = jnp.zeros((B, 2, LRU_WIDTH), jnp.float32)
    else:
        ckv_c, kr_c, st_c = ctx
        kc, vc = mla_keys(ckv_c, kr_c, lp, None)
        k = jnp.concatenate([k, kc], axis=1)
        v = jnp.concatenate([v, vc], axis=1)
        h0 = st_c.astype(jnp.float32)
    y_mla = block_attention(q, k, v) @ lp["w_mla_o"]

    xc = centred_conv(u_lru, lp["conv_w"], lp["conv_b"])
    h_f = rglru(xc, lp["lru_wa"][0], lp["lru_ba"][0], lp["lru_wx"][0], lp["lru_bx"][0],
                lp["lru_lambda"][0], h0[:, 0], False)
    h_b = rglru(xc, lp["lru_wa"][1], lp["lru_ba"][1], lp["lru_wx"][1], lp["lru_bx"][1],
                lp["lru_lambda"][1], h0[:, 1], True)
    y_lru = ((h_f + h_b) * jax.nn.gelu(u_gate)) @ lp["w_lru_o"]

    y_pool = pool_mixer(u_pool, lp["pool_w"], lp["pool_scale"]) @ lp["w_pool_o"]

    gates = jax.nn.sigmoid(gate_logits.astype(jnp.float32)).astype(x.dtype)
    gates = gates.reshape(B, T, N_BRANCH, D_MODEL)
    merged = gates[:, :, 0] * y_mla + gates[:, :, 1] * y_lru + gates[:, :, 2] * y_pool
    x = x + g1 * (merged @ lp["w_out"])

    h2 = rms_norm(x, lp["norm2"]) * (1 + sc2) + sh2
    x = x + g2 * (jnp.square(jax.nn.relu(h2 @ lp["w_ff1"])) @ lp["w_ff2"])
    return x, ckv, k_rope, h_f, h_b


def setup_inputs(seed: int = 0) -> dict:
    key = jax.random.key(seed)
    ks = iter(jax.random.split(key, 48))

    def nrm(shape, fan_in, gain=1.0):
        return (gain * fan_in ** -0.5) * jax.random.normal(next(ks), shape, jnp.float32)

    def gain_vec(shape):
        return 1.0 + 0.01 * jax.random.normal(next(ks), shape, jnp.float32)

    def small(shape):
        return 0.01 * jax.random.normal(next(ks), shape, jnp.float32)

    u = jax.random.uniform(next(ks), (DEPTH, 2, LRU_WIDTH), jnp.float32, minval=0.9, maxval=0.999)
    a_base = u ** (1.0 / LRU_C)
    lru_lambda = jnp.log(a_base) - jnp.log1p(-a_base)

    return {
        "x_prompt": jax.random.normal(next(ks), (BATCH, SEQ, D_MODEL), jnp.float32),
        "x_sample": jax.random.normal(next(ks), (DEC_BATCH, DEC_SEQ, D_MODEL), jnp.float32),
        "c": jax.random.normal(next(ks), (DEC_BATCH, D_MODEL), jnp.float32),
        "cache_ckv": jax.random.normal(next(ks), (DEC_BATCH, DEPTH, PAST_LEN, KV_LORA), jnp.float32),
        "cache_krope": jax.random.normal(next(ks), (DEC_BATCH, DEPTH, PAST_LEN, QK_ROPE), jnp.float32),
        "state_lru": 0.5 * jax.random.normal(next(ks), (DEC_BATCH, DEPTH, 2, LRU_WIDTH), jnp.float32),
        "c_ctx": jax.random.normal(next(ks), (D_MODEL,), jnp.float32),
        "w_mod": nrm((DEPTH, D_MODEL, N_MOD * D_MODEL), D_MODEL, 0.3),
        "b_mod": small((DEPTH, N_MOD * D_MODEL)),
        "norm1_w": gain_vec((DEPTH, D_MODEL)),
        "norm2_w": gain_vec((DEPTH, D_MODEL)),
        "w_in": nrm((DEPTH, D_MODEL, D_IN), D_MODEL),
        "q_a_norm": gain_vec((DEPTH, Q_LORA)),
        "w_qb": nrm((DEPTH, Q_LORA, N_HEADS * QK_DIM), Q_LORA),
        "kv_a_norm": gain_vec((DEPTH, KV_LORA)),
        "w_kvb": nrm((DEPTH, KV_LORA, N_HEADS * (QK_NOPE + V_DIM)), KV_LORA),
        "q_norm": gain_vec((DEPTH, QK_DIM)),
        "k_norm": gain_vec((DEPTH, QK_DIM)),
        "w_mla_o": nrm((DEPTH, N_HEADS * V_DIM, D_MODEL), N_HEADS * V_DIM),
        "conv_w": nrm((DEPTH, CONV_W, LRU_WIDTH), CONV_W),
        "conv_b": small((DEPTH, LRU_WIDTH)),
        "lru_wa": nrm((DEPTH, 2, LRU_BLOCKS, LRU_BLOCK, LRU_BLOCK), LRU_BLOCK),
        "lru_ba": small((DEPTH, 2, LRU_WIDTH)),
        "lru_wx": nrm((DEPTH, 2, LRU_BLOCKS, LRU_BLOCK, LRU_BLOCK), LRU_BLOCK),
        "lru_bx": small((DEPTH, 2, LRU_WIDTH)),
        "lru_lambda": lru_lambda,
        "w_lru_o": nrm((DEPTH, LRU_WIDTH, D_MODEL), LRU_WIDTH),
        "pool_w": nrm((DEPTH, POOL_GROUPS, POOL_GROUP, POOL_GROUP), POOL_GROUP),
        "pool_scale": gain_vec((DEPTH, POOL_WIDTH)),
        "w_pool_o": nrm((DEPTH, POOL_WIDTH, D_MODEL), POOL_WIDTH),
        "w_out": nrm((DEPTH, D_MODEL, D_MODEL), D_MODEL),
        "w_ff1": nrm((DEPTH, D_MODEL, D_FF), D_MODEL),
        "w_ff2": nrm((DEPTH, D_FF, D_MODEL), D_FF),
    }


def reference(x_prompt, x_sample, c, cache_ckv, cache_krope, state_lru, c_ctx,
              w_mod, b_mod, norm1_w, norm2_w, w_in, q_a_norm, w_qb, kv_a_norm, w_kvb,
              q_norm, k_norm, w_mla_o, conv_w, conv_b, lru_wa, lru_ba, lru_wx, lru_bx,
              lru_lambda, w_lru_o, pool_w, pool_scale, w_pool_o, w_out, w_ff1, w_ff2):
    rows = x_sample.shape[1] // GRID_W
    rope = axial_rope(rows)
    cond_ctx = c_ctx[None, :]
    y_prompt = x_prompt
    y_sample = x_sample
    ckv_list, krope_list, state_list = [], [], []
    for l in range(DEPTH):
        lp = {
            "w_mod": w_mod[l], "b_mod": b_mod[l], "norm1": norm1_w[l], "norm2": norm2_w[l],
            "w_in": w_in[l], "q_a_norm": q_a_norm[l], "w_qb": w_qb[l],
            "kv_a_norm": kv_a_norm[l], "w_kvb": w_kvb[l], "q_norm": q_norm[l],
            "k_norm": k_norm[l], "w_mla_o": w_mla_o[l], "conv_w": conv_w[l],
            "conv_b": conv_b[l], "lru_wa": lru_wa[l], "lru_ba": lru_ba[l],
            "lru_wx": lru_wx[l], "lru_bx": lru_bx[l], "lru_lambda": lru_lambda[l],
            "w_lru_o": w_lru_o[l], "pool_w": pool_w[l], "pool_scale": pool_scale[l],
            "w_pool_o": w_pool_o[l], "w_out": w_out[l], "w_ff1": w_ff1[l], "w_ff2": w_ff2[l],
        }
        y_prompt, ckv, k_rope, h_f, h_b = trunk_layer(y_prompt, cond_ctx, lp, None, None)
        ckv_list.append(ckv)
        krope_list.append(k_rope)
        state_list.append(jnp.stack([h_f[:, -1], h_b[:, 0]], axis=1))
        y_sample, _, _, _, _ = trunk_layer(
            y_sample, c, lp, rope, (cache_ckv[:, l], cache_krope[:, l], state_lru[:, l]))
    new_ckv = jnp.stack(ckv_list, axis=1)
    new_krope = jnp.stack(krope_list, axis=1)
    new_lru_state = jnp.stack(state_list, axis=1)
    return (y_prompt, y_sample, new_ckv, new_krope, new_lru_state)
```

```python
import functools

import jax
import jax.numpy as jnp
from jax import lax
from jax.experimental import pallas as pl
from jax.experimental.pallas import tpu as pltpu

F32 = jnp.float32
BF16 = jnp.bfloat16

EPS = 1e-6
GRID_W = 64
N_HEADS = 8
QK_NOPE = 128
QK_ROPE = 64
V_DIM = 128
QK_DIM = QK_NOPE + QK_ROPE
Q_LORA = 512
KV_LORA = 256
ROPE_BASE = 10000.0
LRU_WIDTH = 1024
LRU_C = 8.0
POOL_WIDTH = 1024
POOL_WINDOWS = (2, 4, 8, 16)
N_MOD = 6

LANE = 128
MXU_TILE = 256
VMEM_LIMIT = 56 * 2**20

OFF_Q = 0
OFF_KV = Q_LORA
OFF_KR = OFF_KV + KV_LORA
OFF_KROT = OFF_KR + LANE
OFF_LRU = OFF_KROT + LANE
OFF_GATE = OFF_LRU + LRU_WIDTH
OFF_POOL = OFF_GATE + LRU_WIDTH
OFF_BR = OFF_POOL + POOL_WIDTH


def _params(sem):
    return pltpu.CompilerParams(dimension_semantics=sem, vmem_limit_bytes=VMEM_LIMIT)


def _pick(n, prefs):
    for p in prefs:
        if n % p == 0:
            return p
    return n


def _mod_kernel(c_ref, w_ref, b_ref, o_ref):
    c = c_ref[...]
    s = c * jax.nn.sigmoid(c)
    o_ref[...] = jnp.dot(s.astype(BF16), w_ref[...].astype(BF16), preferred_element_type=F32) + b_ref[...]


def _modulation(cond, w_mod, b_mod):
    L, D, N = w_mod.shape
    R = cond.shape[0]
    tn = _pick(N, (1024, 512, 256, 128))
    return pl.pallas_call(
        _mod_kernel,
        grid=(L, N // tn),
        in_specs=[
            pl.BlockSpec((R, D), lambda l, j: (0, 0)),
            pl.BlockSpec((None, D, tn), lambda l, j: (l, 0, j)),
            pl.BlockSpec((None, 1, tn), lambda l, j: (l, 0, j)),
        ],
        out_specs=pl.BlockSpec((None, R, tn), lambda l, j: (l, 0, j)),
        out_shape=jax.ShapeDtypeStruct((L, R, N), F32),
        compiler_params=_params(("arbitrary", "arbitrary")),
        name="modulation",
    )(cond, w_mod, b_mod.reshape(L, 1, N))


def _inproj_kernel(x_ref, nw_ref, sh_ref, sc_ref, w_ref, o_ref, h_scr):
    @pl.when(pl.program_id(1) == 0)
    def _():
        x = x_ref[...]
        ms = jnp.mean(x * x, axis=-1, keepdims=True)
        y = x * lax.rsqrt(ms + EPS) * nw_ref[...]
        h_scr[...] = (y * (1.0 + sc_ref[...]) + sh_ref[...]).astype(BF16)

    o_ref[...] = jnp.dot(h_scr[...], w_ref[...], preferred_element_type=F32)


def _in_proj(x2d, mod4, row_of, norm_w, w_in_p, l, tm):
    M, D = x2d.shape
    N = w_in_p.shape[-1]
    tn = _pick(N, (1024, 512, 256))
    return pl.pallas_call(
        _inproj_kernel,
        grid=(M // tm, N // tn),
        in_specs=[
            pl.BlockSpec((tm, D), lambda i, j: (i, 0)),
            pl.BlockSpec((None, 1, D), lambda i, j: (l, 0, 0)),
            pl.BlockSpec((None, None, 1, D), lambda i, j: (l, row_of(i * tm), 0, 0)),
            pl.BlockSpec((None, None, 1, D), lambda i, j: (l, row_of(i * tm), 0, 1)),
            pl.BlockSpec((None, D, tn), lambda i, j: (l, 0, j)),
        ],
        out_specs=pl.BlockSpec((tm, tn), lambda i, j: (i, j)),
        out_shape=jax.ShapeDtypeStruct((M, N), F32),
        scratch_shapes=[pltpu.VMEM((tm, D), BF16)],
        compiler_params=_params(("arbitrary", "arbitrary")),
        name="in_proj",
    )(x2d, norm_w, mod4, mod4, w_in_p)


def _head_scale(nope, rope_sq_sum):
    ss = jnp.sum(nope * nope, axis=-1, keepdims=True) + rope_sq_sum
    return lax.rsqrt(ss * (1.0 / QK_DIM) + EPS)


def _qprep_kernel(*refs, use_rope):
    if use_rope:
        ql_ref, an_ref, w_ref, wrot_ref, gn_ref, gr_ref, gs_ref, cos_ref, sin_ref, q_ref = refs
    else:
        ql_ref, an_ref, w_ref, gn_ref, gr_ref, q_ref = refs
    ql = ql_ref[...]
    ms = jnp.mean(ql * ql, axis=-1, keepdims=True)
    qn = (ql * lax.rsqrt(ms + EPS) * an_ref[...]).astype(BF16)
    qq = jnp.dot(qn, w_ref[...], preferred_element_type=F32)
    if use_rope:
        qrot = jnp.dot(qn, wrot_ref[...], preferred_element_type=F32)
        cos = cos_ref[...]
        sin = sin_ref[...]
        gs = gs_ref[...]
    gn = gn_ref[...]
    gr = gr_ref[...]
    hw = N_HEADS * LANE
    for h in range(N_HEADS):
        nope = qq[:, h * LANE:(h + 1) * LANE]
        rope = qq[:, hw + h * LANE:hw + (h + 1) * LANE]
        s = _head_scale(nope, jnp.sum(rope * rope, axis=-1, keepdims=True))
        if use_rope:
            r = gr * rope * cos + gs * qrot[:, h * LANE:(h + 1) * LANE] * sin
        else:
            r = gr * rope
        q_ref[h, :, 0:LANE] = (nope * s * gn).astype(BF16)
        q_ref[h, :, LANE:2 * LANE] = (r * s).astype(BF16)


def _q_prep(z, B, T, l, W, rope, tm):
    M = z.shape[0]
    nb = T // tm
    use_rope = rope is not None
    wspec = lambda n: pl.BlockSpec((None, Q_LORA, n), lambda i: (l, 0, 0))
    vec = pl.BlockSpec((None, 1, LANE), lambda i: (l, 0, 0))
    in_specs = [
        pl.BlockSpec((tm, Q_LORA), lambda i: (i, OFF_Q // Q_LORA)),
        pl.BlockSpec((None, 1, Q_LORA), lambda i: (l, 0, 0)),
        wspec(2 * N_HEADS * LANE),
    ]
    args = [z, W["q_a_norm"], W["w_q"]]
    if use_rope:
        tab = pl.BlockSpec((tm, LANE), lambda i: (i % nb, 0))
        in_specs += [wspec(N_HEADS * LANE), vec, vec, vec, tab, tab]
        args += [W["w_qrot"], W["qg_n"], W["qg_r"], W["qg_s"], rope[0], rope[1]]
    else:
        in_specs += [vec, vec]
        args += [W["qg_n"], W["qg_r"]]
    return pl.pallas_call(
        functools.partial(_qprep_kernel, use_rope=use_rope),
        grid=(M // tm,),
        in_specs=in_specs,
        out_specs=pl.BlockSpec((None, N_HEADS, tm, 2 * LANE), lambda i: (i // nb, 0, i % nb, 0)),
        out_shape=jax.ShapeDtypeStruct((B, N_HEADS, T, 2 * LANE), BF16),
        compiler_params=_params(("arbitrary",)),
        name="q_prep",
    )(*args)


def _kprep_kernel(*refs, use_rope, normalize):
    refs = list(refs)
    kv_ref = refs.pop(0)
    kr_ref = refs.pop(0)
    krot_ref = refs.pop(0) if use_rope else None
    an_ref = refs.pop(0) if normalize else None
    w_ref = refs.pop(0)
    gn_ref = refs.pop(0)
    gr_ref = refs.pop(0)
    if use_rope:
        gs_ref, cos_ref, sin_ref = refs.pop(0), refs.pop(0), refs.pop(0)
    k_ref = refs.pop(0)
    v_ref = refs.pop(0)
    ckv_ref = refs.pop(0) if normalize else None

    kv = kv_ref[...]
    if normalize:
        ms = jnp.mean(kv * kv, axis=-1, keepdims=True)
        kv = kv * lax.rsqrt(ms + EPS) * an_ref[...]
        ckv_ref[...] = kv
    kk = jnp.dot(kv.astype(BF16), w_ref[...], preferred_element_type=F32)
    kr = kr_ref[...]
    ssr = jnp.sum(kr * kr, axis=-1, keepdims=True)
    if use_rope:
        base = gr_ref[...] * kr * cos_ref[...] + gs_ref[...] * krot_ref[...] * sin_ref[...]
    else:
        base = gr_ref[...] * kr
    gn = gn_ref[...]
    hw = N_HEADS * LANE
    for h in range(N_HEADS):
        nope = kk[:, h * LANE:(h + 1) * LANE]
        s = _head_scale(nope, ssr)
        k_ref[h, :, 0:LANE] = (nope * s * gn).astype(BF16)
        k_ref[h, :, LANE:2 * LANE] = (base * s).astype(BF16)
        v_ref[h] = kk[:, hw + h * LANE:hw + (h + 1) * LANE].astype(BF16)


def _k_prep(kv_src, kr_src, B, T, l, W, rope, tm, *, kv_spec, kr_spec, krot_spec, normalize):
    M = B * T
    nb = T // tm
    use_rope = rope is not None
    vec = pl.BlockSpec((None, 1, LANE), lambda i: (l, 0, 0))
    in_specs = [kv_spec, kr_spec]
    args = [kv_src, kr_src]
    if use_rope:
        in_specs.append(krot_spec)
        args.append(kr_src)
    if normalize:
        in_specs.append(pl.BlockSpec((None, 1, KV_LORA), lambda i: (l, 0, 0)))
        args.append(W["kv_a_norm"])
    in_specs += [pl.BlockSpec((None, KV_LORA, 2 * N_HEADS * LANE), lambda i: (l, 0, 0)), vec, vec]
    args += [W["w_kv"], W["kg_n"], W["kg_r"]]
    if use_rope:
        tab = pl.BlockSpec((tm, LANE), lambda i: (i % nb, 0))
        in_specs += [vec, tab, tab]
        args += [W["kg_s"], rope[0], rope[1]]
    out_specs = [
        pl.BlockSpec((None, N_HEADS, tm, 2 * LANE), lambda i: (i // nb, 0, i % nb, 0)),
        pl.BlockSpec((None, N_HEADS, tm, LANE), lambda i: (i // nb, 0, i % nb, 0)),
    ]
    out_shape = [
        jax.ShapeDtypeStruct((B, N_HEADS, T, 2 * LANE), BF16),
        jax.ShapeDtypeStruct((B, N_HEADS, T, LANE), BF16),
    ]
    if normalize:
        out_specs.append(pl.BlockSpec((tm, KV_LORA), lambda i: (i, 0)))
        out_shape.append(jax.ShapeDtypeStruct((M, KV_LORA), F32))
    return pl.pallas_call(
        functools.partial(_kprep_kernel, use_rope=use_rope, normalize=normalize),
        grid=(M // tm,),
        in_specs=in_specs,
        out_specs=out_specs,
        out_shape=out_shape,
        compiler_params=_params(("arbitrary",)),
        name="k_prep",
    )(*args)


def _attn_kernel(*refs, has_ctx, scale):
    if has_ctx:
        q_ref, k_ref, v_ref, kc_ref, vc_ref, o_ref = refs
    else:
        q_ref, k_ref, v_ref, o_ref = refs
    q = q_ref[...]
    nt = (((1,), (1,)), ((), ()))
    s = lax.dot_general(q, k_ref[...], nt, preferred_element_type=F32) * scale
    m = jnp.max(s, axis=-1, keepdims=True)
    if has_ctx:
        sc = lax.dot_general(q, kc_ref[...], nt, preferred_element_type=F32) * scale
        m = jnp.maximum(m, jnp.max(sc, axis=-1, keepdims=True))
    p = jnp.exp(s - m)
    den = jnp.sum(p, axis=-1, keepdims=True)
    o = jnp.dot(p.astype(BF16), v_ref[...], preferred_element_type=F32)
    if has_ctx:
        pc = jnp.exp(sc - m)
        den = den + jnp.sum(pc, axis=-1, keepdims=True)
        o = o + jnp.dot(pc.astype(BF16), vc_ref[...], preferred_element_type=F32)
    o_ref[...] = (o / den).astype(BF16)


def _attention(q, k, v, kc, vc, tq):
    B, H, T, _ = q.shape
    S = k.shape[2]
    has_ctx = kc is not None
    in_specs = [
        pl.BlockSpec((None, None, tq, 2 * LANE), lambda b, h, i: (b, h, i, 0)),
        pl.BlockSpec((None, None, S, 2 * LANE), lambda b, h, i: (b, h, 0, 0)),
        pl.BlockSpec((None, None, S, LANE), lambda b, h, i: (b, h, 0, 0)),
    ]
    args = [q, k, v]
    if has_ctx:
        P = kc.shape[2]
        in_specs += [
            pl.BlockSpec((None, None, P, 2 * LANE), lambda b, h, i: (b, h, 0, 0)),
            pl.BlockSpec((None, None, P, LANE), lambda b, h, i: (b, h, 0, 0)),
        ]
        args += [kc, vc]
    return pl.pallas_call(
        functools.partial(_attn_kernel, has_ctx=has_ctx, scale=QK_DIM ** -0.5),
        grid=(B, H, T // tq),
        in_specs=in_specs,
        out_specs=pl.BlockSpec((None, tq, LANE), lambda b, h, i: (b, i, h)),
        out_shape=jax.ShapeDtypeStruct((B, T, H * LANE), BF16),
        compiler_params=_params(("arbitrary", "arbitrary", "arbitrary")),
        name="attention",
    )(*args)


def _shift_rows(x, k, row):
    n = x.shape[0]
    rolled = pltpu.roll(x, k % n, 0)
    valid = (row >= k) if k > 0 else (row < n + k)
    return jnp.where(valid, rolled, 0.0)


def _scan_chunk(a, b, row, reverse):
    n = a.shape[0]
    d = 1
    while d < n:
        if reverse:
            a_s = pltpu.roll(a, n - d, 0)
            b_s = pltpu.roll(b, n - d, 0)
            valid = row < n - d
        else:
            a_s = pltpu.roll(a, d, 0)
            b_s = pltpu.roll(b, d, 0)
            valid = row >= d
        b = jnp.where(valid, a * b_s, 0.0) + b
        a = jnp.where(valid, a * a_s, a)
        d *= 2
    return a, b


def _softplus(x):
    return jnp.maximum(x, 0.0) + jnp.log(1.0 + jnp.exp(-jnp.abs(x)))


def _gelu_tanh(x):
    return 0.5 * x * (1.0 + jnp.tanh(0.7978845608028654 * (x + 0.044715 * x * x * x)))


def _lru_kernel(u_ref, ug_ref, cw_ref, cb_ref, w_ref, b_ref, lam_ref, h0_ref, y_ref, st_ref, xc_scr, hf_scr, *, tc):
    T, C = xc_scr.shape
    nch = T // tc
    row_t = lax.broadcasted_iota(jnp.int32, (T, C), 0)
    u = u_ref[...]
    cw = cw_ref[...]
    xc = (cb_ref[...] + cw[0:1] * _shift_rows(u, 2, row_t) + cw[1:2] * _shift_rows(u, 1, row_t)
          + cw[2:3] * u + cw[3:4] * _shift_rows(u, -1, row_t))
    xc_scr[...] = xc

    row_c = lax.broadcasted_iota(jnp.int32, (tc, C), 0)
    sp = _softplus(-lam_ref[...])
    h0 = h0_ref[...]

    def gates(xcj, d):
        g = jnp.dot(xcj.astype(BF16), w_ref[:, 2 * d * C:2 * (d + 1) * C], preferred_element_type=F32)
        g = g + b_ref[:, 2 * d * C:2 * (d + 1) * C]
        r = jax.nn.sigmoid(g[:, :C])
        i = jax.nn.sigmoid(g[:, C:])
        log_a = -LRU_C * r * sp[d:d + 1]
        a = jnp.exp(log_a)
        b = jnp.sqrt(jnp.tanh(-log_a) * (a * a + 1.0)) * (i * xcj)
        return a, b

    def fwd(j, carry):
        t0 = pl.multiple_of(j * tc, tc)
        xcj = xc_scr[pl.ds(t0, tc), :]
        a, b = gates(xcj, 0)
        A, Bc = _scan_chunk(a, b, row_c, False)
        h = A * carry + Bc
        hf_scr[pl.ds(t0, tc), :] = h
        return h[tc - 1:tc, :]

    hf_last = lax.fori_loop(0, nch, fwd, h0[0:1])
    st_ref[0:1, :] = hf_last

    def bwd(jj, carry):
        j = nch - 1 - jj
        t0 = pl.multiple_of(j * tc, tc)
        xcj = xc_scr[pl.ds(t0, tc), :]
        a, b = gates(xcj, 1)
        A, Bc = _scan_chunk(a, b, row_c, True)
        h = A * carry + Bc
        y = (hf_scr[pl.ds(t0, tc), :] + h) * _gelu_tanh(ug_ref[pl.ds(t0, tc), :])
        y_ref[pl.ds(t0, tc), :] = y.astype(BF16)
        return h[0:1, :]

    hb_first = lax.fori_loop(0, nch, bwd, h0[1:2])
    st_ref[1:2, :] = hb_first


def _lru_mixer(z3, h0, l, W, tc):
    B, T, _ = z3.shape
    C = MXU_TILE
    nct = LRU_WIDTH // C
    return pl.pallas_call(
        functools.partial(_lru_kernel, tc=tc),
        grid=(B, nct),
        in_specs=[
            pl.BlockSpec((None, T, C), lambda b, c: (b, 0, OFF_LRU // C + c)),
            pl.BlockSpec((None, T, C), lambda b, c: (b, 0, OFF_GATE // C + c)),
            pl.BlockSpec((None, 4, C), lambda b, c: (l, 0, c)),
            pl.BlockSpec((None, 1, C), lambda b, c: (l, 0, c)),
            pl.BlockSpec((None, None, C, 4 * C), lambda b, c: (l, c, 0, 0)),
            pl.BlockSpec((None, None, 1, 4 * C), lambda b, c: (l, c, 0, 0)),
            pl.BlockSpec((None, 2, C), lambda b, c: (l, 0, c)),
            pl.BlockSpec((None, 2, C), lambda b, c: (b, 0, c)),
        ],
        out_specs=[
            pl.BlockSpec((None, T, C), lambda b, c: (b, 0, c)),
            pl.BlockSpec((None, 2, C), lambda b, c: (b, 0, c)),
        ],
        out_shape=[
            jax.ShapeDtypeStruct((B, T, LRU_WIDTH), BF16),
            jax.ShapeDtypeStruct((B, 2, LRU_WIDTH), F32),
        ],
        scratch_shapes=[pltpu.VMEM((T, C), F32), pltpu.VMEM((T, C), F32)],
        compiler_params=_params(("arbitrary", "arbitrary")),
        name="lru_mixer",
    )(z3, z3, W["conv_w"], W["conv_b"], W["lru_w"], W["lru_b"], W["lru_lambda"], h0)


def _pool_kernel(u_ref, w_ref, sc_ref, o_ref):
    T, C = u_ref.shape
    g = pl.program_id(1)
    for gi, win in enumerate(POOL_WINDOWS):
        @pl.when(g == gi)
        def _(win=win):
            row = lax.broadcasted_iota(jnp.int32, (T, C), 0)
            u = u_ref[...]
            s = u + _shift_rows(u, 1, row)
            w = 4
            while w <= win:
                q = w // 4
                s = _shift_rows(s, q, row) + _shift_rows(s, -q, row)
                w *= 2
            half = win // 2
            cnt = (jnp.minimum(row + half, T) - jnp.maximum(row - half, 0)).astype(F32)
            d = s / cnt - u
            y = jnp.dot(d.astype(BF16), w_ref[...], preferred_element_type=F32) * sc_ref[...]
            o_ref[...] = y.astype(BF16)


def _pool_mixer(z3, l, W):
    B, T, _ = z3.shape
    C = MXU_TILE
    ng = POOL_WIDTH // C
    return pl.pallas_call(
        _pool_kernel,
        grid=(B, ng),
        in_specs=[
            pl.BlockSpec((None, T, C), lambda b, g: (b, 0, OFF_POOL // C + g)),
            pl.BlockSpec((None, None, C, C), lambda b, g: (l, g, 0, 0)),
            pl.BlockSpec((None, 1, C), lambda b, g: (l, 0, g)),
        ],
        out_specs=pl.BlockSpec((None, T, C), lambda b, g: (b, 0, g)),
        out_shape=jax.ShapeDtypeStruct((B, T, POOL_WIDTH), BF16),
        compiler_params=_params(("arbitrary", "arbitrary")),
        name="pool_mixer",
    )(z3, W["pool_w"], W["pool_scale"])


def _merge_kernel(a_ref, r_ref, p_ref, g0_ref, g1_ref, g2_ref, wa_ref, wr_ref, wp_ref, o_ref):
    ya = jnp.dot(a_ref[...], wa_ref[...], preferred_element_type=F32)
    yr = jnp.dot(r_ref[...], wr_ref[...], preferred_element_type=F32)
    yp = jnp.dot(p_ref[...], wp_ref[...], preferred_element_type=F32)
    m = (jax.nn.sigmoid(g0_ref[...]) * ya + jax.nn.sigmoid(g1_ref[...]) * yr
         + jax.nn.sigmoid(g2_ref[...]) * yp)
    o_ref[...] = m.astype(BF16)


def _merge(attn, lru, pool, z, l, W, tm):
    M, K = attn.shape
    D = W["w_mla_o"].shape[-1]
    tn = _pick(D, (512, 256))
    nbr = D // tn
    act = pl.BlockSpec((tm, K), lambda i, j: (i, 0))
    wsp = pl.BlockSpec((None, K, tn), lambda i, j: (l, 0, j))
    gsp = lambda k: pl.BlockSpec((tm, tn), lambda i, j: (i, OFF_BR // tn + k * nbr + j))
    return pl.pallas_call(
        _merge_kernel,
        grid=(M // tm, D // tn),
        in_specs=[act, act, act, gsp(0), gsp(1), gsp(2), wsp, wsp, wsp],
        out_specs=pl.BlockSpec((tm, tn), lambda i, j: (i, j)),
        out_shape=jax.ShapeDtypeStruct((M, D), BF16),
        compiler_params=_params(("arbitrary", "arbitrary")),
        name="merge",
    )(attn, lru, pool, z, z, z, W["w_mla_o"], W["w_lru_o"], W["w_pool_o"])


def _outproj_kernel(m_ref, w_ref, x_ref, g_ref, o_ref):
    y = jnp.dot(m_ref[...], w_ref[...], preferred_element_type=F32)
    o_ref[...] = x_ref[...] + g_ref[...] * y


def _out_proj(merged, x2d, mod4, row_of, l, W, tm):
    M, D = x2d.shape
    tn = _pick(D, (512, 256))
    nbr = D // tn
    return pl.pallas_call(
        _outproj_kernel,
        grid=(M // tm, D // tn),
        in_specs=[
            pl.BlockSpec((tm, D), lambda i, j: (i, 0)),
            pl.BlockSpec((None, D, tn), lambda i, j: (l, 0, j)),
            pl.BlockSpec((tm, tn), lambda i, j: (i, j)),
            pl.BlockSpec((None, None, 1, tn), lambda i, j: (l, row_of(i * tm), 0, 2 * nbr + j)),
        ],
        out_specs=pl.BlockSpec((tm, tn), lambda i, j: (i, j)),
        out_shape=jax.ShapeDtypeStruct((M, D), F32),
        compiler_params=_params(("arbitrary", "arbitrary")),
        name="out_proj",
    )(merged, W["w_out"], x2d, mod4)


def _ffn_kernel(x_ref, nw_ref, sh_ref, sc_ref, g_ref, w1_ref, w2_ref, o_ref, h_scr):
    f = pl.program_id(1)

    @pl.when(f == 0)
    def _():
        x = x_ref[...]
        ms = jnp.mean(x * x, axis=-1, keepdims=True)
        y = x * lax.rsqrt(ms + EPS) * nw_ref[...]
        h_scr[...] = (y * (1.0 + sc_ref[...]) + sh_ref[...]).astype(BF16)
        o_ref[...] = jnp.zeros_like(o_ref)

    a = jnp.dot(h_scr[...], w1_ref[...], preferred_element_type=F32)
    a = jnp.maximum(a, 0.0)
    o_ref[...] += jnp.dot((a * a).astype(BF16), w2_ref[...], preferred_element_type=F32)

    @pl.when(f == pl.num_programs(1) - 1)
    def _():
        o_ref[...] = x_ref[...] + g_ref[...] * o_ref[...]


def _ffn(x2d, mod4, row_of, l, W, tm):
    M, D = x2d.shape
    F = W["w_ff1"].shape[-1]
    tf = _pick(F, (512, 256))
    modv = lambda k: pl.BlockSpec((None, None, 1, D), lambda i, f: (l, row_of(i * tm), 0, k))
    return pl.pallas_call(
        _ffn_kernel,
        grid=(M // tm, F // tf),
        in_specs=[
            pl.BlockSpec((tm, D), lambda i, f: (i, 0)),
            pl.BlockSpec((None, 1, D), lambda i, f: (l, 0, 0)),
            modv(3), modv(4), modv(5),
            pl.BlockSpec((None, D, tf), lambda i, f: (l, 0, f)),
            pl.BlockSpec((None, tf, D), lambda i, f: (l, f, 0)),
        ],
        out_specs=pl.BlockSpec((tm, D), lambda i, f: (i, 0)),
        out_shape=jax.ShapeDtypeStruct((M, D), F32),
        scratch_shapes=[pltpu.VMEM((tm, D), BF16)],
        compiler_params=_params(("arbitrary", "arbitrary")),
        name="ffn",
    )(x2d, W["norm2"], mod4, mod4, mod4, W["w_ff1"], W["w_ff2"])


def _pad_lanes(x, n=LANE):
    return jnp.pad(x, [(0, 0)] * (x.ndim - 1) + [(0, n - x.shape[-1])])


def _rot_half(x):
    h = x.shape[-1] // 2
    return jnp.concatenate([-x[..., h:], x[..., :h]], axis=-1)


def _swap_half(x):
    h = x.shape[-1] // 2
    return jnp.concatenate([x[..., h:], x[..., :h]], axis=-1)


def _norm_gains(g):
    gr = g[:, QK_NOPE:]
    return (g[:, None, :QK_NOPE], _pad_lanes(gr)[:, None, :], _pad_lanes(_swap_half(gr))[:, None, :])


def _block_diag_tiles(w, per):
    *lead, nb, s, _ = w.shape
    w = w.reshape(*lead, nb // per, per, s, s)
    t = jnp.einsum("...kij,km->...kimj", w, jnp.eye(per, dtype=w.dtype))
    return t.reshape(*lead, nb // per, per * s, per * s)


def _pack_weights(w_in, q_a_norm, w_qb, kv_a_norm, w_kvb, q_norm, k_norm, w_mla_o, conv_w, conv_b,
                  lru_wa, lru_ba, lru_wx, lru_bx, lru_lambda, w_lru_o, pool_w, pool_scale, w_pool_o,
                  w_out, w_ff1, w_ff2, norm1_w, norm2_w):
    L, D, _ = w_in.shape
    o_kr = Q_LORA + KV_LORA
    wkr = w_in[..., o_kr:o_kr + QK_ROPE]
    w_in_p = jnp.concatenate(
        [w_in[..., :o_kr], _pad_lanes(wkr), _pad_lanes(_rot_half(wkr)), w_in[..., o_kr + QK_ROPE:]], axis=-1)

    wq = w_qb.reshape(L, Q_LORA, N_HEADS, QK_DIM)
    wq_rope = wq[..., QK_NOPE:]
    w_q = jnp.concatenate([wq[..., :QK_NOPE].reshape(L, Q_LORA, -1),
                           _pad_lanes(wq_rope).reshape(L, Q_LORA, -1)], axis=-1)
    w_qrot = _pad_lanes(_rot_half(wq_rope)).reshape(L, Q_LORA, -1)

    wkv = w_kvb.reshape(L, KV_LORA, N_HEADS, QK_NOPE + V_DIM)
    w_kv = jnp.concatenate([wkv[..., :QK_NOPE].reshape(L, KV_LORA, -1),
                            wkv[..., QK_NOPE:].reshape(L, KV_LORA, -1)], axis=-1)

    qg = _norm_gains(q_norm)
    kg = _norm_gains(k_norm)

    per = MXU_TILE // lru_wa.shape[-1]
    ta = _block_diag_tiles(lru_wa, per)
    tx = _block_diag_tiles(lru_wx, per)
    lru_w = jnp.concatenate([ta[:, 0], tx[:, 0], ta[:, 1], tx[:, 1]], axis=-1)
    nct = lru_w.shape[1]
    ba = lru_ba.reshape(L, 2, nct, 1, MXU_TILE)
    bx = lru_bx.reshape(L, 2, nct, 1, MXU_TILE)
    lru_b = jnp.concatenate([ba[:, 0], bx[:, 0], ba[:, 1], bx[:, 1]], axis=-1)

    return dict(
        w_in=w_in_p.astype(BF16), norm1=norm1_w[:, None, :], norm2=norm2_w[:, None, :],
        q_a_norm=q_a_norm[:, None, :], kv_a_norm=kv_a_norm[:, None, :],
        w_q=w_q.astype(BF16), w_qrot=w_qrot.astype(BF16), w_kv=w_kv.astype(BF16),
        qg_n=qg[0], qg_r=qg[1], qg_s=qg[2], kg_n=kg[0], kg_r=kg[1], kg_s=kg[2],
        w_mla_o=w_mla_o.astype(BF16), w_lru_o=w_lru_o.astype(BF16), w_pool_o=w_pool_o.astype(BF16),
        conv_w=conv_w, conv_b=conv_b[:, None, :], lru_w=lru_w.astype(BF16), lru_b=lru_b,
        lru_lambda=lru_lambda, pool_w=pool_w.astype(BF16), pool_scale=pool_scale[:, None, :],
        w_out=w_out.astype(BF16), w_ff1=w_ff1.astype(BF16), w_ff2=w_ff2.astype(BF16),
    )


def _rope_tables(T):
    rows = T // GRID_W
    row = jnp.repeat(jnp.arange(rows), GRID_W).astype(F32)
    col = jnp.tile(jnp.arange(GRID_W), rows).astype(F32)
    n_freq = QK_ROPE // 4
    inv = ROPE_BASE ** (-(jnp.arange(n_freq, dtype=F32) / n_freq))
    ang = jnp.concatenate([row[:, None] * inv, col[:, None] * inv], axis=-1)
    cos, sin = jnp.cos(ang), jnp.sin(ang)
    return (_pad_lanes(jnp.concatenate([cos, cos], axis=-1)), _pad_lanes(jnp.concatenate([sin, sin], axis=-1)))


def _trunk_layer(x2d, B, T, mod4, row_of, l, W, rope, ctx, h0):
    tm = _pick(T, (1024, 512, 256, 128))
    tmp = _pick(T, (256, 128))
    z = _in_proj(x2d, mod4, row_of, W["norm1"], W["w_in"], l, tm)
    z3 = z.reshape(B, T, -1)

    q = _q_prep(z, B, T, l, W, rope, tmp)
    k, v, ckv = _k_prep(
        z, z, B, T, l, W, rope, tmp,
        kv_spec=pl.BlockSpec((tmp, KV_LORA), lambda i: (i, OFF_KV // KV_LORA)),
        kr_spec=pl.BlockSpec((tmp, LANE), lambda i: (i, OFF_KR // LANE)),
        krot_spec=pl.BlockSpec((tmp, LANE), lambda i: (i, OFF_KROT // LANE)),
        normalize=True)
    if ctx is None:
        kc = vc = None
    else:
        ckv_c, kr_c = ctx
        P = ckv_c.shape[2]
        tp = _pick(P, (256, 128))
        npb = P // tp
        kc, vc = _k_prep(
            ckv_c, kr_c, B, P, l, W, None, tp,
            kv_spec=pl.BlockSpec((None, None, tp, KV_LORA), lambda i: (i // npb, l, i % npb, 0)),
            kr_spec=pl.BlockSpec((None, None, tp, LANE), lambda i: (i // npb, l, i % npb, 0)),
            krot_spec=None, normalize=False)
    attn = _attention(q, k, v, kc, vc, _pick(T, (512, 256, 128)))

    lru, state = _lru_mixer(z3, h0, l, W, _pick(T, (256, 128)))
    pool = _pool_mixer(z3, l, W)

    M = B * T
    merged = _merge(attn.reshape(M, -1), lru.reshape(M, -1), pool.reshape(M, -1), z, l, W, tm)
    x2d = _out_proj(merged, x2d, mod4, row_of, l, W, tm)
    x2d = _ffn(x2d, mod4, row_of, l, W, _pick(T, (512, 256, 128)))
    return x2d, z, ckv, state


def kernel(x_prompt, x_sample, c, cache_ckv, cache_krope, state_lru, c_ctx, w_mod, b_mod, norm1_w, norm2_w, w_in, q_a_norm, w_qb, kv_a_norm, w_kvb, q_norm, k_norm, w_mla_o, conv_w, conv_b, lru_wa, lru_ba, lru_wx, lru_bx, lru_lambda, w_lru_o, pool_w, pool_scale, w_pool_o, w_out, w_ff1, w_ff2):
    L = w_in.shape[0]
    Bc, Tc, D = x_prompt.shape
    Bs, Ts, _ = x_sample.shape

    W = _pack_weights(w_in, q_a_norm, w_qb, kv_a_norm, w_kvb, q_norm, k_norm, w_mla_o, conv_w, conv_b,
                      lru_wa, lru_ba, lru_wx, lru_bx, lru_lambda, w_lru_o, pool_w, pool_scale, w_pool_o,
                      w_out, w_ff1, w_ff2, norm1_w, norm2_w)

    rows = Bs + 1
    rpad = -rows % 8
    cond = jnp.concatenate([c, c_ctx[None, :], jnp.zeros((rpad, D), F32)], axis=0)
    mod = _modulation(cond, w_mod, b_mod)
    mod4 = mod.reshape(L, rows + rpad, 1, N_MOD * D)

    rope = _rope_tables(Ts)
    kr_cache = _pad_lanes(cache_krope)
    zero_state = jnp.zeros((Bc, 2, LRU_WIDTH), F32)

    row_smp = lambda t: t // Ts
    row_ctx = lambda t: Bs

    xp = x_prompt.reshape(Bc * Tc, D)
    xs = x_sample.reshape(Bs * Ts, D)
    ckv_list, kr_list, st_list = [], [], []
    for l in range(L):
        xp, zc, ckv, st = _trunk_layer(xp, Bc, Tc, mod4, row_ctx, l, W, None, None, zero_state)
        ckv_list.append(ckv.reshape(Bc, Tc, KV_LORA))
        kr_list.append(zc[:, OFF_KR:OFF_KR + QK_ROPE].reshape(Bc, Tc, QK_ROPE))
        st_list.append(st)
        xs, _, _, _ = _trunk_layer(xs, Bs, Ts, mod4, row_smp, l, W, rope,
                                   (cache_ckv, kr_cache), state_lru[:, l])
    return (xp.reshape(Bc, Tc, D), xs.reshape(Bs, Ts, D), jnp.stack(ckv_list, axis=1),
            jnp.stack(kr_list, axis=1), jnp.stack(st_list, axis=1))
```

```python
import functools

import jax
import jax.numpy as jnp
from jax import lax
from jax.experimental import pallas as pl
from jax.experimental.pallas import tpu as pltpu

F32 = jnp.float32
BF16 = jnp.bfloat16

EPS = 1e-6
GRID_W = 64
N_HEADS = 8
QK_NOPE = 128
QK_ROPE = 64
V_DIM = 128
QK_DIM = QK_NOPE + QK_ROPE
Q_LORA = 512
KV_LORA = 256
ROPE_BASE = 10000.0
LRU_WIDTH = 1024
LRU_C = 8.0
POOL_WIDTH = 1024
POOL_WINDOWS = (2, 4, 8, 16)
N_MOD = 6

LANE = 128
SUBLANE = 8
MXU_TILE = 256
VMEM_LIMIT = 56 * 2**20
ATTN_CHAIN_ROWS = 256
Q_PRESCALE = QK_DIM ** -0.5 * 1.4426950408889634

OFF_Q = 0
OFF_KV = Q_LORA
OFF_KR = OFF_KV + KV_LORA
OFF_KROT = OFF_KR + LANE
OFF_LRU = OFF_KROT + LANE
OFF_GATE = OFF_LRU + LRU_WIDTH
OFF_POOL = OFF_GATE + LRU_WIDTH
OFF_BR = OFF_POOL + POOL_WIDTH


def _params(sem):
    return pltpu.CompilerParams(dimension_semantics=sem, vmem_limit_bytes=VMEM_LIMIT)


def _pick(n, prefs):
    for p in prefs:
        if n % p == 0:
            return p
    return n


def _mod_kernel(c_ref, w_ref, b_ref, o_ref):
    c = c_ref[...]
    s = c * jax.nn.sigmoid(c)
    o_ref[...] = jnp.dot(s.astype(BF16), w_ref[...].astype(BF16), preferred_element_type=F32) + b_ref[...]


def _modulation(cond, w_mod, b_mod):
    L, D, N = w_mod.shape
    R = cond.shape[0]
    tn = _pick(N, (1024, 512, 256, 128))
    return pl.pallas_call(
        _mod_kernel,
        grid=(L, N // tn),
        in_specs=[
            pl.BlockSpec((R, D), lambda l, j: (0, 0)),
            pl.BlockSpec((None, D, tn), lambda l, j: (l, 0, j)),
            pl.BlockSpec((None, 1, tn), lambda l, j: (l, 0, j)),
        ],
        out_specs=pl.BlockSpec((None, R, tn), lambda l, j: (l, 0, j)),
        out_shape=jax.ShapeDtypeStruct((L, R, N), F32),
        compiler_params=_params(("arbitrary", "arbitrary")),
        name="modulation",
    )(cond, w_mod, b_mod.reshape(L, 1, N))


def _inproj_kernel(x_ref, nw_ref, sh_ref, sc_ref, w_ref, o_ref, h_scr):
    @pl.when(pl.program_id(1) == 0)
    def _():
        x = x_ref[...]
        ms = jnp.mean(x * x, axis=-1, keepdims=True)
        y = x * lax.rsqrt(ms + EPS) * nw_ref[...]
        h_scr[...] = (y * (1.0 + sc_ref[...]) + sh_ref[...]).astype(BF16)

    o_ref[...] = jnp.dot(h_scr[...], w_ref[...], preferred_element_type=F32)


def _in_proj(x2d, mod4, row_of, norm_w, w_in_p, l, tm):
    M, D = x2d.shape
    N = w_in_p.shape[-1]
    tn = _pick(N, (1024, 512, 256))
    return pl.pallas_call(
        _inproj_kernel,
        grid=(M // tm, N // tn),
        in_specs=[
            pl.BlockSpec((tm, D), lambda i, j: (i, 0)),
            pl.BlockSpec((None, 1, D), lambda i, j: (l, 0, 0)),
            pl.BlockSpec((None, None, 1, D), lambda i, j: (l, row_of(i * tm), 0, 0)),
            pl.BlockSpec((None, None, 1, D), lambda i, j: (l, row_of(i * tm), 0, 1)),
            pl.BlockSpec((None, D, tn), lambda i, j: (l, 0, j)),
        ],
        out_specs=pl.BlockSpec((tm, tn), lambda i, j: (i, j)),
        out_shape=jax.ShapeDtypeStruct((M, N), F32),
        scratch_shapes=[pltpu.VMEM((tm, D), BF16)],
        compiler_params=_params(("arbitrary", "arbitrary")),
        name="in_proj",
    )(x2d, norm_w, mod4, mod4, w_in_p)


def _head_scale(nope, rope_sq_sum):
    ss = jnp.sum(nope * nope, axis=-1, keepdims=True) + rope_sq_sum
    return lax.rsqrt(ss * (1.0 / QK_DIM) + EPS)


def _qprep_kernel(*refs, use_rope):
    if use_rope:
        ql_ref, an_ref, w_ref, wrot_ref, gn_ref, gr_ref, gs_ref, cos_ref, sin_ref, q_ref = refs
    else:
        ql_ref, an_ref, w_ref, gn_ref, gr_ref, q_ref = refs
    ql = ql_ref[...]
    ms = jnp.mean(ql * ql, axis=-1, keepdims=True)
    qn = (ql * lax.rsqrt(ms + EPS) * an_ref[...]).astype(BF16)
    qq = jnp.dot(qn, w_ref[...], preferred_element_type=F32)
    if use_rope:
        qrot = jnp.dot(qn, wrot_ref[...], preferred_element_type=F32)
        cos = cos_ref[...]
        sin = sin_ref[...]
        gs = gs_ref[...]
    gn = gn_ref[...]
    gr = gr_ref[...]
    hw = N_HEADS * LANE
    for h in range(N_HEADS):
        nope = qq[:, h * LANE:(h + 1) * LANE]
        rope = qq[:, hw + h * LANE:hw + (h + 1) * LANE]
        s = _head_scale(nope, jnp.sum(rope * rope, axis=-1, keepdims=True))
        if use_rope:
            r = gr * rope * cos + gs * qrot[:, h * LANE:(h + 1) * LANE] * sin
        else:
            r = gr * rope
        s = s * Q_PRESCALE
        q_ref[h, :, 0:LANE] = (nope * s * gn).astype(BF16)
        q_ref[h, :, LANE:2 * LANE] = (r * s).astype(BF16)


def _q_prep(z, B, T, l, W, rope, tm):
    M = z.shape[0]
    nb = T // tm
    use_rope = rope is not None
    wspec = lambda n: pl.BlockSpec((None, Q_LORA, n), lambda i: (l, 0, 0))
    vec = pl.BlockSpec((None, 1, LANE), lambda i: (l, 0, 0))
    in_specs = [
        pl.BlockSpec((tm, Q_LORA), lambda i: (i, OFF_Q // Q_LORA)),
        pl.BlockSpec((None, 1, Q_LORA), lambda i: (l, 0, 0)),
        wspec(2 * N_HEADS * LANE),
    ]
    args = [z, W["q_a_norm"], W["w_q"]]
    if use_rope:
        tab = pl.BlockSpec((tm, LANE), lambda i: (i % nb, 0))
        in_specs += [wspec(N_HEADS * LANE), vec, vec, vec, tab, tab]
        args += [W["w_qrot"], W["qg_n"], W["qg_r"], W["qg_s"], rope[0], rope[1]]
    else:
        in_specs += [vec, vec]
        args += [W["qg_n"], W["qg_r"]]
    return pl.pallas_call(
        functools.partial(_qprep_kernel, use_rope=use_rope),
        grid=(M // tm,),
        in_specs=in_specs,
        out_specs=pl.BlockSpec((None, N_HEADS, tm, 2 * LANE), lambda i: (i // nb, 0, i % nb, 0)),
        out_shape=jax.ShapeDtypeStruct((B, N_HEADS, T, 2 * LANE), BF16),
        compiler_params=_params(("arbitrary",)),
        name="q_prep",
    )(*args)


def _kprep_kernel(*refs, use_rope, normalize):
    refs = list(refs)
    kv_ref = refs.pop(0)
    kr_ref = refs.pop(0)
    krot_ref = refs.pop(0) if use_rope else None
    an_ref = refs.pop(0) if normalize else None
    w_ref = refs.pop(0)
    gn_ref = refs.pop(0)
    gr_ref = refs.pop(0)
    if use_rope:
        gs_ref, cos_ref, sin_ref = refs.pop(0), refs.pop(0), refs.pop(0)
    k_ref = refs.pop(0)
    v_ref = refs.pop(0)
    ckv_ref = refs.pop(0) if normalize else None

    kv = kv_ref[...]
    if normalize:
        ms = jnp.mean(kv * kv, axis=-1, keepdims=True)
        kv = kv * lax.rsqrt(ms + EPS) * an_ref[...]
        ckv_ref[...] = kv
    kk = jnp.dot(kv.astype(BF16), w_ref[...], preferred_element_type=F32)
    kr = kr_ref[...]
    ssr = jnp.sum(kr * kr, axis=-1, keepdims=True)
    if use_rope:
        base = gr_ref[...] * kr * cos_ref[...] + gs_ref[...] * krot_ref[...] * sin_ref[...]
    else:
        base = gr_ref[...] * kr
    gn = gn_ref[...]
    hw = N_HEADS * LANE
    for h in range(N_HEADS):
        nope = kk[:, h * LANE:(h + 1) * LANE]
        s = _head_scale(nope, ssr)
        k_ref[h, :, 0:LANE] = (nope * s * gn).astype(BF16)
        k_ref[h, :, LANE:2 * LANE] = (base * s).astype(BF16)
        v_ref[h] = kk[:, hw + h * LANE:hw + (h + 1) * LANE].astype(BF16)


def _k_prep(kv_src, kr_src, B, T, l, W, rope, tm, *, kv_spec, kr_spec, krot_spec, normalize):
    M = B * T
    nb = T // tm
    use_rope = rope is not None
    vec = pl.BlockSpec((None, 1, LANE), lambda i: (l, 0, 0))
    in_specs = [kv_spec, kr_spec]
    args = [kv_src, kr_src]
    if use_rope:
        in_specs.append(krot_spec)
        args.append(kr_src)
    if normalize:
        in_specs.append(pl.BlockSpec((None, 1, KV_LORA), lambda i: (l, 0, 0)))
        args.append(W["kv_a_norm"])
    in_specs += [pl.BlockSpec((None, KV_LORA, 2 * N_HEADS * LANE), lambda i: (l, 0, 0)), vec, vec]
    args += [W["w_kv"], W["kg_n"], W["kg_r"]]
    if use_rope:
        tab = pl.BlockSpec((tm, LANE), lambda i: (i % nb, 0))
        in_specs += [vec, tab, tab]
        args += [W["kg_s"], rope[0], rope[1]]
    out_specs = [
        pl.BlockSpec((None, N_HEADS, tm, 2 * LANE), lambda i: (i // nb, 0, i % nb, 0)),
        pl.BlockSpec((None, N_HEADS, tm, LANE), lambda i: (i // nb, 0, i % nb, 0)),
    ]
    out_shape = [
        jax.ShapeDtypeStruct((B, N_HEADS, T, 2 * LANE), BF16),
        jax.ShapeDtypeStruct((B, N_HEADS, T, LANE), BF16),
    ]
    if normalize:
        out_specs.append(pl.BlockSpec((tm, KV_LORA), lambda i: (i, 0)))
        out_shape.append(jax.ShapeDtypeStruct((M, KV_LORA), F32))
    return pl.pallas_call(
        functools.partial(_kprep_kernel, use_rope=use_rope, normalize=normalize),
        grid=(M // tm,),
        in_specs=in_specs,
        out_specs=out_specs,
        out_shape=out_shape,
        compiler_params=_params(("arbitrary",)),
        name="k_prep",
    )(*args)


def _attn_kernel(*refs, has_ctx, nsub):
    if has_ctx:
        q_ref, k_ref, v_ref, kc_ref, vc_ref, o_ref = refs
    else:
        q_ref, k_ref, v_ref, o_ref = refs
    nt = (((1,), (1,)), ((), ()))
    tq = q_ref.shape[0]
    ts = tq // nsub
    for r in range(nsub):
        q = q_ref[r * ts:(r + 1) * ts, :]
        s = lax.dot_general(q, k_ref[...], nt, preferred_element_type=F32)
        m = jnp.max(s, axis=-1, keepdims=True)
        if has_ctx:
            sc = lax.dot_general(q, kc_ref[...], nt, preferred_element_type=F32)
            m = jnp.maximum(m, jnp.max(sc, axis=-1, keepdims=True))
        p = jnp.exp2(s - m)
        den = jnp.sum(p, axis=-1, keepdims=True)
        o = jnp.dot(p.astype(BF16), v_ref[...], preferred_element_type=F32)
        if has_ctx:
            pc = jnp.exp2(sc - m)
            den = den + jnp.sum(pc, axis=-1, keepdims=True)
            o = o + jnp.dot(pc.astype(BF16), vc_ref[...], preferred_element_type=F32)
        o_ref[r * ts:(r + 1) * ts, :] = (o / den).astype(BF16)


def _attention(q, k, v, kc, vc, tq):
    B, H, T, _ = q.shape
    S = k.shape[2]
    has_ctx = kc is not None
    in_specs = [
        pl.BlockSpec((None, None, tq, 2 * LANE), lambda b, h, i: (b, h, i, 0)),
        pl.BlockSpec((None, None, S, 2 * LANE), lambda b, h, i: (b, h, 0, 0)),
        pl.BlockSpec((None, None, S, LANE), lambda b, h, i: (b, h, 0, 0)),
    ]
    args = [q, k, v]
    if has_ctx:
        P = kc.shape[2]
        in_specs += [
            pl.BlockSpec((None, None, P, 2 * LANE), lambda b, h, i: (b, h, 0, 0)),
            pl.BlockSpec((None, None, P, LANE), lambda b, h, i: (b, h, 0, 0)),
        ]
        args += [kc, vc]
    return pl.pallas_call(
        functools.partial(_attn_kernel, has_ctx=has_ctx, nsub=max(1, tq // ATTN_CHAIN_ROWS)),
        grid=(B, H, T // tq),
        in_specs=in_specs,
        out_specs=pl.BlockSpec((None, tq, LANE), lambda b, h, i: (b, i, h)),
        out_shape=jax.ShapeDtypeStruct((B, T, H * LANE), BF16),
        compiler_params=_params(("arbitrary", "arbitrary", "arbitrary")),
        name="attention",
    )(*args)


def _shift_rows(x, k, row):
    n = x.shape[0]
    rolled = pltpu.roll(x, k % n, 0)
    valid = (row >= k) if k > 0 else (row < n + k)
    return jnp.where(valid, rolled, 0.0)


def _scan_chunk(a, b, carry, sub, reverse):
    n = a.shape[0]
    ng = n // SUBLANE
    hs = [None] * ng
    for j in (range(ng - 1, -1, -1) if reverse else range(ng)):
        aj = a[j * SUBLANE:(j + 1) * SUBLANE]
        bj = b[j * SUBLANE:(j + 1) * SUBLANE]
        d = 1
        while d < SUBLANE:
            shift = SUBLANE - d if reverse else d
            valid = (sub < SUBLANE - d) if reverse else (sub >= d)
            a_s = pltpu.roll(aj, shift, 0)
            b_s = pltpu.roll(bj, shift, 0)
            bj = jnp.where(valid, aj * b_s, 0.0) + bj
            aj = jnp.where(valid, aj * a_s, aj)
            d *= 2
        h = aj * carry + bj
        hs[j] = h
        carry = h[0:1] if reverse else h[SUBLANE - 1:SUBLANE]
    return jnp.concatenate(hs, axis=0), carry


def _sigmoid(x):
    return 0.5 * jnp.tanh(0.5 * x) + 0.5


def _softplus(x):
    return jnp.maximum(x, 0.0) + jnp.log(1.0 + jnp.exp(-jnp.abs(x)))


def _gelu_tanh(x):
    return 0.5 * x * (1.0 + jnp.tanh(0.7978845608028654 * (x + 0.044715 * x * x * x)))


def _lru_kernel(u_ref, ug_ref, cw_ref, cb_ref, w_ref, b_ref, lam_ref, h0_ref, y_ref, st_ref, xc_scr, hf_scr, *, tc):
    T, C = xc_scr.shape
    nch = T // tc
    row_t = lax.broadcasted_iota(jnp.int32, (T, C), 0)
    u = u_ref[...]
    cw = cw_ref[...]
    xc = (cb_ref[...] + cw[0:1] * _shift_rows(u, 2, row_t) + cw[1:2] * _shift_rows(u, 1, row_t)
          + cw[2:3] * u + cw[3:4] * _shift_rows(u, -1, row_t))
    xc_scr[...] = xc

    sub = lax.broadcasted_iota(jnp.int32, (SUBLANE, C), 0)
    sp = _softplus(-lam_ref[...])
    h0 = h0_ref[...]

    def gates(xcj, d):
        g = jnp.dot(xcj.astype(BF16), w_ref[:, 2 * d * C:2 * (d + 1) * C], preferred_element_type=F32)
        g = g + b_ref[:, 2 * d * C:2 * (d + 1) * C]
        r = _sigmoid(g[:, :C])
        i = _sigmoid(g[:, C:])
        log_a = -LRU_C * r * sp[d:d + 1]
        a = jnp.exp(log_a)
        b = jnp.sqrt(jnp.tanh(-log_a) * (a * a + 1.0)) * (i * xcj)
        return a, b

    def fwd(j, carry):
        t0 = pl.multiple_of(j * tc, tc)
        xcj = xc_scr[pl.ds(t0, tc), :]
        a, b = gates(xcj, 0)
        h, carry = _scan_chunk(a, b, carry, sub, False)
        hf_scr[pl.ds(t0, tc), :] = h
        return carry

    hf_last = lax.fori_loop(0, nch, fwd, h0[0:1])
    st_ref[0:1, :] = hf_last

    def bwd(jj, carry):
        j = nch - 1 - jj
        t0 = pl.multiple_of(j * tc, tc)
        xcj = xc_scr[pl.ds(t0, tc), :]
        a, b = gates(xcj, 1)
        h, carry = _scan_chunk(a, b, carry, sub, True)
        y = (hf_scr[pl.ds(t0, tc), :] + h) * _gelu_tanh(ug_ref[pl.ds(t0, tc), :])
        y_ref[pl.ds(t0, tc), :] = y.astype(BF16)
        return carry

    hb_first = lax.fori_loop(0, nch, bwd, h0[1:2])
    st_ref[1:2, :] = hb_first


def _lru_mixer(z3, h0, l, W, tc):
    B, T, _ = z3.shape
    C = MXU_TILE
    nct = LRU_WIDTH // C
    return pl.pallas_call(
        functools.partial(_lru_kernel, tc=tc),
        grid=(B, nct),
        in_specs=[
            pl.BlockSpec((None, T, C), lambda b, c: (b, 0, OFF_LRU // C + c)),
            pl.BlockSpec((None, T, C), lambda b, c: (b, 0, OFF_GATE // C + c)),
            pl.BlockSpec((None, 4, C), lambda b, c: (l, 0, c)),
            pl.BlockSpec((None, 1, C), lambda b, c: (l, 0, c)),
            pl.BlockSpec((None, None, C, 4 * C), lambda b, c: (l, c, 0, 0)),
            pl.BlockSpec((None, None, 1, 4 * C), lambda b, c: (l, c, 0, 0)),
            pl.BlockSpec((None, 2, C), lambda b, c: (l, 0, c)),
            pl.BlockSpec((None, 2, C), lambda b, c: (b, 0, c)),
        ],
        out_specs=[
            pl.BlockSpec((None, T, C), lambda b, c: (b, 0, c)),
            pl.BlockSpec((None, 2, C), lambda b, c: (b, 0, c)),
        ],
        out_shape=[
            jax.ShapeDtypeStruct((B, T, LRU_WIDTH), BF16),
            jax.ShapeDtypeStruct((B, 2, LRU_WIDTH), F32),
        ],
        scratch_shapes=[pltpu.VMEM((T, C), F32), pltpu.VMEM((T, C), F32)],
        compiler_params=_params(("arbitrary", "arbitrary")),
        name="lru_mixer",
    )(z3, z3, W["conv_w"], W["conv_b"], W["lru_w"], W["lru_b"], W["lru_lambda"], h0)


def _pool_kernel(u_ref, w_ref, sc_ref, o_ref):
    T, C = u_ref.shape
    g = pl.program_id(1)
    for gi, win in enumerate(POOL_WINDOWS):
        @pl.when(g == gi)
        def _(win=win):
            row = lax.broadcasted_iota(jnp.int32, (T, C), 0)
            u = u_ref[...]
            s = u + _shift_rows(u, 1, row)
            w = 4
            while w <= win:
                q = w // 4
                s = _shift_rows(s, q, row) + _shift_rows(s, -q, row)
                w *= 2
            half = win // 2
            cnt = (jnp.minimum(row + half, T) - jnp.maximum(row - half, 0)).astype(F32)
            d = s / cnt - u
            y = jnp.dot(d.astype(BF16), w_ref[...], preferred_element_type=F32) * sc_ref[...]
            o_ref[...] = y.astype(BF16)


def _pool_mixer(z3, l, W):
    B, T, _ = z3.shape
    C = MXU_TILE
    ng = POOL_WIDTH // C
    return pl.pallas_call(
        _pool_kernel,
        grid=(B, ng),
        in_specs=[
            pl.BlockSpec((None, T, C), lambda b, g: (b, 0, OFF_POOL // C + g)),
            pl.BlockSpec((None, None, C, C), lambda b, g: (l, g, 0, 0)),
            pl.BlockSpec((None, 1, C), lambda b, g: (l, 0, g)),
        ],
        out_specs=pl.BlockSpec((None, T, C), lambda b, g: (b, 0, g)),
        out_shape=jax.ShapeDtypeStruct((B, T, POOL_WIDTH), BF16),
        compiler_params=_params(("arbitrary", "arbitrary")),
        name="pool_mixer",
    )(z3, W["pool_w"], W["pool_scale"])


def _merge_kernel(a_ref, r_ref, p_ref, g0_ref, g1_ref, g2_ref, wa_ref, wr_ref, wp_ref, o_ref):
    ya = jnp.dot(a_ref[...], wa_ref[...], preferred_element_type=F32)
    yr = jnp.dot(r_ref[...], wr_ref[...], preferred_element_type=F32)
    yp = jnp.dot(p_ref[...], wp_ref[...], preferred_element_type=F32)
    m = _sigmoid(g0_ref[...]) * ya + _sigmoid(g1_ref[...]) * yr + _sigmoid(g2_ref[...]) * yp
    o_ref[...] = m.astype(BF16)


def _merge(attn, lru, pool, z, l, W, tm):
    M, K = attn.shape
    D = W["w_mla_o"].shape[-1]
    tn = _pick(D, (512, 256))
    nbr = D // tn
    act = pl.BlockSpec((tm, K), lambda i, j: (i, 0))
    wsp = pl.BlockSpec((None, K, tn), lambda i, j: (l, 0, j))
    gsp = lambda k: pl.BlockSpec((tm, tn), lambda i, j: (i, OFF_BR // tn + k * nbr + j))
    return pl.pallas_call(
        _merge_kernel,
        grid=(M // tm, D // tn),
        in_specs=[act, act, act, gsp(0), gsp(1), gsp(2), wsp, wsp, wsp],
        out_specs=pl.BlockSpec((tm, tn), lambda i, j: (i, j)),
        out_shape=jax.ShapeDtypeStruct((M, D), BF16),
        compiler_params=_params(("arbitrary", "arbitrary")),
        name="merge",
    )(attn, lru, pool, z, z, z, W["w_mla_o"], W["w_lru_o"], W["w_pool_o"])


def _outproj_kernel(m_ref, w_ref, x_ref, g_ref, o_ref):
    y = jnp.dot(m_ref[...], w_ref[...], preferred_element_type=F32)
    o_ref[...] = x_ref[...] + g_ref[...] * y


def _out_proj(merged, x2d, mod4, row_of, l, W, tm):
    M, D = x2d.shape
    tn = _pick(D, (512, 256))
    nbr = D // tn
    return pl.pallas_call(
        _outproj_kernel,
        grid=(M // tm, D // tn),
        in_specs=[
            pl.BlockSpec((tm, D), lambda i, j: (i, 0)),
            pl.BlockSpec((None, D, tn), lambda i, j: (l, 0, j)),
            pl.BlockSpec((tm, tn), lambda i, j: (i, j)),
            pl.BlockSpec((None, None, 1, tn), lambda i, j: (l, row_of(i * tm), 0, 2 * nbr + j)),
        ],
        out_specs=pl.BlockSpec((tm, tn), lambda i, j: (i, j)),
        out_shape=jax.ShapeDtypeStruct((M, D), F32),
        compiler_params=_params(("arbitrary", "arbitrary")),
        name="out_proj",
    )(merged, W["w_out"], x2d, mod4)


def _ffn_kernel(x_ref, nw_ref, sh_ref, sc_ref, g_ref, w1_ref, w2_ref, o_ref, h_scr):
    f = pl.program_id(1)

    @pl.when(f == 0)
    def _():
        x = x_ref[...]
        ms = jnp.mean(x * x, axis=-1, keepdims=True)
        y = x * lax.rsqrt(ms + EPS) * nw_ref[...]
        h_scr[...] = (y * (1.0 + sc_ref[...]) + sh_ref[...]).astype(BF16)
        o_ref[...] = jnp.zeros_like(o_ref)

    a = jnp.dot(h_scr[...], w1_ref[...], preferred_element_type=F32)
    a = jnp.maximum(a, 0.0)
    o_ref[...] += jnp.dot((a * a).astype(BF16), w2_ref[...], preferred_element_type=F32)

    @pl.when(f == pl.num_programs(1) - 1)
    def _():
        o_ref[...] = x_ref[...] + g_ref[...] * o_ref[...]


def _ffn(x2d, mod4, row_of, l, W, tm):
    M, D = x2d.shape
    F = W["w_ff1"].shape[-1]
    tf = _pick(F, (1024, 512, 256))
    modv = lambda k: pl.BlockSpec((None, None, 1, D), lambda i, f: (l, row_of(i * tm), 0, k))
    return pl.pallas_call(
        _ffn_kernel,
        grid=(M // tm, F // tf),
        in_specs=[
            pl.BlockSpec((tm, D), lambda i, f: (i, 0)),
            pl.BlockSpec((None, 1, D), lambda i, f: (l, 0, 0)),
            modv(3), modv(4), modv(5),
            pl.BlockSpec((None, D, tf), lambda i, f: (l, 0, f)),
            pl.BlockSpec((None, tf, D), lambda i, f: (l, f, 0)),
        ],
        out_specs=pl.BlockSpec((tm, D), lambda i, f: (i, 0)),
        out_shape=jax.ShapeDtypeStruct((M, D), F32),
        scratch_shapes=[pltpu.VMEM((tm, D), BF16)],
        compiler_params=_params(("arbitrary", "arbitrary")),
        name="ffn",
    )(x2d, W["norm2"], mod4, mod4, mod4, W["w_ff1"], W["w_ff2"])


def _pad_lanes(x, n=LANE):
    return jnp.pad(x, [(0, 0)] * (x.ndim - 1) + [(0, n - x.shape[-1])])


def _rot_half(x):
    h = x.shape[-1] // 2
    return jnp.concatenate([-x[..., h:], x[..., :h]], axis=-1)


def _swap_half(x):
    h = x.shape[-1] // 2
    return jnp.concatenate([x[..., h:], x[..., :h]], axis=-1)


def _norm_gains(g):
    gr = g[:, QK_NOPE:]
    return (g[:, None, :QK_NOPE], _pad_lanes(gr)[:, None, :], _pad_lanes(_swap_half(gr))[:, None, :])


def _block_diag_tiles(w, per):
    *lead, nb, s, _ = w.shape
    w = w.reshape(*lead, nb // per, per, s, s)
    t = jnp.einsum("...kij,km->...kimj", w, jnp.eye(per, dtype=w.dtype))
    return t.reshape(*lead, nb // per, per * s, per * s)


def _pack_weights(w_in, q_a_norm, w_qb, kv_a_norm, w_kvb, q_norm, k_norm, w_mla_o, conv_w, conv_b,
                  lru_wa, lru_ba, lru_wx, lru_bx, lru_lambda, w_lru_o, pool_w, pool_scale, w_pool_o,
                  w_out, w_ff1, w_ff2, norm1_w, norm2_w):
    L, D, _ = w_in.shape
    o_kr = Q_LORA + KV_LORA
    wkr = w_in[..., o_kr:o_kr + QK_ROPE]
    w_in_p = jnp.concatenate(
        [w_in[..., :o_kr], _pad_lanes(wkr), _pad_lanes(_rot_half(wkr)), w_in[..., o_kr + QK_ROPE:]], axis=-1)

    wq = w_qb.reshape(L, Q_LORA, N_HEADS, QK_DIM)
    wq_rope = wq[..., QK_NOPE:]
    w_q = jnp.concatenate([wq[..., :QK_NOPE].reshape(L, Q_LORA, -1),
                           _pad_lanes(wq_rope).reshape(L, Q_LORA, -1)], axis=-1)
    w_qrot = _pad_lanes(_rot_half(wq_rope)).reshape(L, Q_LORA, -1)

    wkv = w_kvb.reshape(L, KV_LORA, N_HEADS, QK_NOPE + V_DIM)
    w_kv = jnp.concatenate([wkv[..., :QK_NOPE].reshape(L, KV_LORA, -1),
                            wkv[..., QK_NOPE:].reshape(L, KV_LORA, -1)], axis=-1)

    qg = _norm_gains(q_norm)
    kg = _norm_gains(k_norm)

    per = MXU_TILE // lru_wa.shape[-1]
    ta = _block_diag_tiles(lru_wa, per)
    tx = _block_diag_tiles(lru_wx, per)
    lru_w = jnp.concatenate([ta[:, 0], tx[:, 0], ta[:, 1], tx[:, 1]], axis=-1)
    nct = lru_w.shape[1]
    ba = lru_ba.reshape(L, 2, nct, 1, MXU_TILE)
    bx = lru_bx.reshape(L, 2, nct, 1, MXU_TILE)
    lru_b = jnp.concatenate([ba[:, 0], bx[:, 0], ba[:, 1], bx[:, 1]], axis=-1)

    return dict(
        w_in=w_in_p.astype(BF16), norm1=norm1_w[:, None, :], norm2=norm2_w[:, None, :],
        q_a_norm=q_a_norm[:, None, :], kv_a_norm=kv_a_norm[:, None, :],
        w_q=w_q.astype(BF16), w_qrot=w_qrot.astype(BF16), w_kv=w_kv.astype(BF16),
        qg_n=qg[0], qg_r=qg[1], qg_s=qg[2], kg_n=kg[0], kg_r=kg[1], kg_s=kg[2],
        w_mla_o=w_mla_o.astype(BF16), w_lru_o=w_lru_o.astype(BF16), w_pool_o=w_pool_o.astype(BF16),
        conv_w=conv_w, conv_b=conv_b[:, None, :], lru_w=lru_w.astype(BF16), lru_b=lru_b,
        lru_lambda=lru_lambda, pool_w=pool_w.astype(BF16), pool_scale=pool_scale[:, None, :],
        w_out=w_out.astype(BF16), w_ff1=w_ff1.astype(BF16), w_ff2=w_ff2.astype(BF16),
    )


def _rope_tables(T):
    rows = T // GRID_W
    row = jnp.repeat(jnp.arange(rows), GRID_W).astype(F32)
    col = jnp.tile(jnp.arange(GRID_W), rows).astype(F32)
    n_freq = QK_ROPE // 4
    inv = ROPE_BASE ** (-(jnp.arange(n_freq, dtype=F32) / n_freq))
    ang = jnp.concatenate([row[:, None] * inv, col[:, None] * inv], axis=-1)
    cos, sin = jnp.cos(ang), jnp.sin(ang)
    return (_pad_lanes(jnp.concatenate([cos, cos], axis=-1)), _pad_lanes(jnp.concatenate([sin, sin], axis=-1)))


def _trunk_layer(x2d, B, T, mod4, row_of, l, W, rope, ctx, h0):
    tm = _pick(T, (1024, 512, 256, 128))
    tmp = _pick(T, (256, 128))
    z = _in_proj(x2d, mod4, row_of, W["norm1"], W["w_in"], l, tm)
    z3 = z.reshape(B, T, -1)

    q = _q_prep(z, B, T, l, W, rope, tmp)
    k, v, ckv = _k_prep(
        z, z, B, T, l, W, rope, tmp,
        kv_spec=pl.BlockSpec((tmp, KV_LORA), lambda i: (i, OFF_KV // KV_LORA)),
        kr_spec=pl.BlockSpec((tmp, LANE), lambda i: (i, OFF_KR // LANE)),
        krot_spec=pl.BlockSpec((tmp, LANE), lambda i: (i, OFF_KROT // LANE)),
        normalize=True)
    if ctx is None:
        kc = vc = None
    else:
        ckv_c, kr_c = ctx
        P = ckv_c.shape[2]
        tp = _pick(P, (256, 128))
        npb = P // tp
        kc, vc = _k_prep(
            ckv_c, kr_c, B, P, l, W, None, tp,
            kv_spec=pl.BlockSpec((None, None, tp, KV_LORA), lambda i: (i // npb, l, i % npb, 0)),
            kr_spec=pl.BlockSpec((None, None, tp, LANE), lambda i: (i // npb, l, i % npb, 0)),
            krot_spec=None, normalize=False)
    attn = _attention(q, k, v, kc, vc, _pick(T, (1024, 512, 256, 128)))

    lru, state = _lru_mixer(z3, h0, l, W, _pick(T, (256, 128)))
    pool = _pool_mixer(z3, l, W)

    M = B * T
    merged = _merge(attn.reshape(M, -1), lru.reshape(M, -1), pool.reshape(M, -1), z, l, W, tm)
    x2d = _out_proj(merged, x2d, mod4, row_of, l, W, tm)
    x2d = _ffn(x2d, mod4, row_of, l, W, _pick(T, (512, 256, 128)))
    return x2d, z, ckv, state


def kernel(x_prompt, x_sample, c, cache_ckv, cache_krope, state_lru, c_ctx, w_mod, b_mod, norm1_w, norm2_w, w_in, q_a_norm, w_qb, kv_a_norm, w_kvb, q_norm, k_norm, w_mla_o, conv_w, conv_b, lru_wa, lru_ba, lru_wx, lru_bx, lru_lambda, w_lru_o, pool_w, pool_scale, w_pool_o, w_out, w_ff1, w_ff2):
    L = w_in.shape[0]
    Bc, Tc, D = x_prompt.shape
    Bs, Ts, _ = x_sample.shape

    W = _pack_weights(w_in, q_a_norm, w_qb, kv_a_norm, w_kvb, q_norm, k_norm, w_mla_o, conv_w, conv_b,
                      lru_wa, lru_ba, lru_wx, lru_bx, lru_lambda, w_lru_o, pool_w, pool_scale, w_pool_o,
                      w_out, w_ff1, w_ff2, norm1_w, norm2_w)

    rows = Bs + 1
    rpad = -rows % 8
    cond = jnp.concatenate([c, c_ctx[None, :], jnp.zeros((rpad, D), F32)], axis=0)
    mod = _modulation(cond, w_mod, b_mod)
    mod4 = mod.reshape(L, rows + rpad, 1, N_MOD * D)

    rope = _rope_tables(Ts)
    kr_cache = _pad_lanes(cache_krope)
    zero_state = jnp.zeros((Bc, 2, LRU_WIDTH), F32)

    row_smp = lambda t: t // Ts
    row_ctx = lambda t: Bs

    xp = x_prompt.reshape(Bc * Tc, D)
    xs = x_sample.reshape(Bs * Ts, D)
    ckv_list, kr_list, st_list = [], [], []
    for l in range(L):
        xp, zc, ckv, st = _trunk_layer(xp, Bc, Tc, mod4, row_ctx, l, W, None, None, zero_state)
        ckv_list.append(ckv.reshape(Bc, Tc, KV_LORA))
        kr_list.append(zc[:, OFF_KR:OFF_KR + QK_ROPE].reshape(Bc, Tc, QK_ROPE))
        st_list.append(st)
        xs, _, _, _ = _trunk_layer(xs, Bs, Ts, mod4, row_smp, l, W, rope,
                                   (cache_ckv, kr_cache), state_lru[:, l])
    return (xp.reshape(Bc, Tc, D), xs.reshape(Bs, Ts, D), jnp.stack(ckv_list, axis=1),
            jnp.stack(kr_list, axis=1), jnp.stack(st_list, axis=1))
```

```python
import functools

import jax
import jax.numpy as jnp
from jax import lax
from jax.experimental import pallas as pl
from jax.experimental.pallas import tpu as pltpu

F32 = jnp.float32
BF16 = jnp.bfloat16

EPS = 1e-6
GRID_W = 64
N_HEADS = 8
QK_NOPE = 128
QK_ROPE = 64
V_DIM = 128
QK_DIM = QK_NOPE + QK_ROPE
Q_LORA = 512
KV_LORA = 256
ROPE_BASE = 10000.0
LRU_WIDTH = 1024
LRU_C = 8.0
POOL_WIDTH = 1024
POOL_WINDOWS = (2, 4, 8, 16)
N_MOD = 6

LANE = 128
SUBLANE = 8
MXU_TILE = 256
VMEM_LIMIT = 56 * 2**20
ATTN_CHAIN_ROWS = 256
ATTN_LOOKAHEAD = 4
Q_PRESCALE =QK_DIM ** -0.5 * 1.4426950408889634

OFF_Q = 0
OFF_KV = Q_LORA
OFF_KR = OFF_KV + KV_LORA
OFF_KROT = OFF_KR + LANE
OFF_LRU = OFF_KROT + LANE
OFF_GATE = OFF_LRU + LRU_WIDTH
OFF_POOL = OFF_GATE + LRU_WIDTH
OFF_BR = OFF_POOL + POOL_WIDTH


def _params(sem):
    return pltpu.CompilerParams(dimension_semantics=sem, vmem_limit_bytes=VMEM_LIMIT)


def _pick(n, prefs):
    for p in prefs:
        if n % p == 0:
            return p
    return n


def _mod_kernel(c_ref, w_ref, b_ref, o_ref):
    c = c_ref[...]
    s = c * jax.nn.sigmoid(c)
    o_ref[...] = jnp.dot(s.astype(BF16), w_ref[...].astype(BF16), preferred_element_type=F32) + b_ref[...]


def _modulation(cond, w_mod, b_mod):
    L, D, N = w_mod.shape
    R = cond.shape[0]
    tn = _pick(N, (1024, 512, 256, 128))
    return pl.pallas_call(
        _mod_kernel,
        grid=(L, N // tn),
        in_specs=[
            pl.BlockSpec((R, D), lambda l, j: (0, 0)),
            pl.BlockSpec((None, D, tn), lambda l, j: (l, 0, j)),
            pl.BlockSpec((None, 1, tn), lambda l, j: (l, 0, j)),
        ],
        out_specs=pl.BlockSpec((None, R, tn), lambda l, j: (l, 0, j)),
        out_shape=jax.ShapeDtypeStruct((L, R, N), F32),
        compiler_params=_params(("arbitrary", "arbitrary")),
        name="modulation",
    )(cond, w_mod, b_mod.reshape(L, 1, N))


def _inproj_kernel(x_ref, nw_ref, sh_ref, sc_ref, w_ref, z_ref, g_ref, h_scr, *, n_lin):
    j = pl.program_id(1)

    @pl.when(j == 0)
    def _():
        x = x_ref[...]
        ms = jnp.mean(x * x, axis=-1, keepdims=True)
        y = x * lax.rsqrt(ms + EPS) * nw_ref[...]
        h_scr[...] = (y * (1.0 + sc_ref[...]) + sh_ref[...]).astype(BF16)

    @pl.when(j < n_lin)
    def _():
        z_ref[...] = jnp.dot(h_scr[...], w_ref[...], preferred_element_type=F32)

    @pl.when(j >= n_lin)
    def _():
        g_ref[...] = _sigmoid(jnp.dot(h_scr[...], w_ref[...], preferred_element_type=F32)).astype(BF16)


def _in_proj(x2d, mod4, row_of, norm_w, w_in_p, l, tm):
    M, D = x2d.shape
    N = w_in_p.shape[-1]
    tn = _pick(OFF_BR, (1024, 512, 256))
    n_lin = OFF_BR // tn
    return pl.pallas_call(
        functools.partial(_inproj_kernel, n_lin=n_lin),
        grid=(M // tm, N // tn),
        in_specs=[
            pl.BlockSpec((tm, D), lambda i, j: (i, 0)),
            pl.BlockSpec((None, 1, D), lambda i, j: (l, 0, 0)),
            pl.BlockSpec((None, None, 1, D), lambda i, j: (l, row_of(i * tm), 0, 0)),
            pl.BlockSpec((None, None, 1, D), lambda i, j: (l, row_of(i * tm), 0, 1)),
            pl.BlockSpec((None, D, tn), lambda i, j: (l, 0, j)),
        ],
        out_specs=[
            pl.BlockSpec((tm, tn), lambda i, j: (i, jnp.minimum(j, n_lin - 1))),
            pl.BlockSpec((tm, tn), lambda i, j: (i, jnp.maximum(j - n_lin, 0))),
        ],
        out_shape=[
            jax.ShapeDtypeStruct((M, OFF_BR), F32),
            jax.ShapeDtypeStruct((M, N - OFF_BR), BF16),
        ],
        scratch_shapes=[pltpu.VMEM((tm, D), BF16)],
        compiler_params=_params(("arbitrary", "arbitrary")),
        name="in_proj",
    )(x2d, norm_w, mod4, mod4, w_in_p)


def _head_scale(nope, rope_sq_sum):
    ss = jnp.sum(nope * nope, axis=-1, keepdims=True) + rope_sq_sum
    return lax.rsqrt(ss * (1.0 / QK_DIM) + EPS)


def _qprep_kernel(*refs, use_rope):
    if use_rope:
        ql_ref, an_ref, w_ref, wrot_ref, gn_ref, gr_ref, gs_ref, cos_ref, sin_ref, q_ref = refs
    else:
        ql_ref, an_ref, w_ref, gn_ref, gr_ref, q_ref = refs
    ql = ql_ref[...]
    ms = jnp.mean(ql * ql, axis=-1, keepdims=True)
    qn = (ql * lax.rsqrt(ms + EPS) * an_ref[...]).astype(BF16)
    qq = jnp.dot(qn, w_ref[...], preferred_element_type=F32)
    if use_rope:
        qrot = jnp.dot(qn, wrot_ref[...], preferred_element_type=F32)
        cos = cos_ref[...]
        sin = sin_ref[...]
        gs = gs_ref[...]
    gn = gn_ref[...]
    gr = gr_ref[...]
    hw = N_HEADS * LANE
    for h in range(N_HEADS):
        nope = qq[:, h * LANE:(h + 1) * LANE]
        rope = qq[:, hw + h * LANE:hw + (h + 1) * LANE]
        s = _head_scale(nope, jnp.sum(rope * rope, axis=-1, keepdims=True))
        if use_rope:
            r = gr * rope * cos + gs * qrot[:, h * LANE:(h + 1) * LANE] * sin
        else:
            r = gr * rope
        s = s * Q_PRESCALE
        q_ref[h, :, 0:LANE] = (nope * s * gn).astype(BF16)
        q_ref[h, :, LANE:2 * LANE] = (r * s).astype(BF16)


def _q_prep(z, B, T, l, W, rope, tm):
    M = z.shape[0]
    nb = T // tm
    use_rope = rope is not None
    wspec = lambda n: pl.BlockSpec((None, Q_LORA, n), lambda i: (l, 0, 0))
    vec = pl.BlockSpec((None, 1, LANE), lambda i: (l, 0, 0))
    in_specs = [
        pl.BlockSpec((tm, Q_LORA), lambda i: (i, OFF_Q // Q_LORA)),
        pl.BlockSpec((None, 1, Q_LORA), lambda i: (l, 0, 0)),
        wspec(2 * N_HEADS * LANE),
    ]
    args = [z, W["q_a_norm"], W["w_q"]]
    if use_rope:
        tab = pl.BlockSpec((tm, LANE), lambda i: (i % nb, 0))
        in_specs += [wspec(N_HEADS * LANE), vec, vec, vec, tab, tab]
        args += [W["w_qrot"], W["qg_n"], W["qg_r"], W["qg_s"], rope[0], rope[1]]
    else:
        in_specs += [vec, vec]
        args += [W["qg_n"], W["qg_r"]]
    return pl.pallas_call(
        functools.partial(_qprep_kernel, use_rope=use_rope),
        grid=(M // tm,),
        in_specs=in_specs,
        out_specs=pl.BlockSpec((None, N_HEADS, tm, 2 * LANE), lambda i: (i // nb, 0, i % nb, 0)),
        out_shape=jax.ShapeDtypeStruct((B, N_HEADS, T, 2 * LANE), BF16),
        compiler_params=_params(("arbitrary",)),
        name="q_prep",
    )(*args)


def _kprep_kernel(*refs, use_rope, normalize):
    refs = list(refs)
    kv_ref = refs.pop(0)
    kr_ref = refs.pop(0)
    krot_ref = refs.pop(0) if use_rope else None
    an_ref = refs.pop(0) if normalize else None
    w_ref = refs.pop(0)
    gn_ref = refs.pop(0)
    gr_ref = refs.pop(0)
    if use_rope:
        gs_ref, cos_ref, sin_ref = refs.pop(0), refs.pop(0), refs.pop(0)
    k_ref = refs.pop(0)
    v_ref = refs.pop(0)
    ckv_ref = refs.pop(0) if normalize else None

    kv = kv_ref[...]
    if normalize:
        ms = jnp.mean(kv * kv, axis=-1, keepdims=True)
        kv = kv * lax.rsqrt(ms + EPS) * an_ref[...]
        ckv_ref[...] = kv
    kk = jnp.dot(kv.astype(BF16), w_ref[...], preferred_element_type=F32)
    kr = kr_ref[...]
    ssr = jnp.sum(kr * kr, axis=-1, keepdims=True)
    if use_rope:
        base = gr_ref[...] * kr * cos_ref[...] + gs_ref[...] * krot_ref[...] * sin_ref[...]
    else:
        base = gr_ref[...] * kr
    gn = gn_ref[...]
    hw = N_HEADS * LANE
    for h in range(N_HEADS):
        nope = kk[:, h * LANE:(h + 1) * LANE]
        s = _head_scale(nope, ssr)
        k_ref[h, :, 0:LANE] = (nope * s * gn).astype(BF16)
        k_ref[h, :, LANE:2 * LANE] = (base * s).astype(BF16)
        v_ref[h] = kk[:, hw + h * LANE:hw + (h + 1) * LANE].astype(BF16)


def _k_prep(kv_src, kr_src, B, T, l, W, rope, tm, *, kv_spec, kr_spec, krot_spec, normalize):
    M = B * T
    nb = T // tm
    use_rope = rope is not None
    vec = pl.BlockSpec((None, 1, LANE), lambda i: (l, 0, 0))
    in_specs = [kv_spec, kr_spec]
    args = [kv_src, kr_src]
    if use_rope:
        in_specs.append(krot_spec)
        args.append(kr_src)
    if normalize:
        in_specs.append(pl.BlockSpec((None, 1, KV_LORA), lambda i: (l, 0, 0)))
        args.append(W["kv_a_norm"])
    in_specs += [pl.BlockSpec((None, KV_LORA, 2 * N_HEADS * LANE), lambda i: (l, 0, 0)), vec, vec]
    args += [W["w_kv"], W["kg_n"], W["kg_r"]]
    if use_rope:
        tab = pl.BlockSpec((tm, LANE), lambda i: (i % nb, 0))
        in_specs += [vec, tab, tab]
        args += [W["kg_s"], rope[0], rope[1]]
    out_specs = [
        pl.BlockSpec((None, N_HEADS, tm, 2 * LANE), lambda i: (i // nb, 0, i % nb, 0)),
        pl.BlockSpec((None, N_HEADS, tm, LANE), lambda i: (i // nb, 0, i % nb, 0)),
    ]
    out_shape = [
        jax.ShapeDtypeStruct((B, N_HEADS, T, 2 * LANE), BF16),
        jax.ShapeDtypeStruct((B, N_HEADS, T, LANE), BF16),
    ]
    if normalize:
        out_specs.append(pl.BlockSpec((tm, KV_LORA), lambda i: (i, 0)))
        out_shape.append(jax.ShapeDtypeStruct((M, KV_LORA), F32))
    return pl.pallas_call(
        functools.partial(_kprep_kernel, use_rope=use_rope, normalize=normalize),
        grid=(M // tm,),
        in_specs=in_specs,
        out_specs=out_specs,
        out_shape=out_shape,
        compiler_params=_params(("arbitrary",)),
        name="k_prep",
    )(*args)


def _attn_kernel(*refs, has_ctx, nsub):
    if has_ctx:
        q_ref, k_ref, v_ref, kc_ref, vc_ref, o_ref = refs
    else:
        q_ref, k_ref, v_ref, o_ref = refs
    nt = (((1,), (1,)), ((), ()))
    tq = q_ref.shape[0]
    ts = tq // nsub
    def scores(r):
        q = q_ref[r * ts:(r + 1) * ts, :]
        s = lax.dot_general(q, k_ref[...], nt, preferred_element_type=F32)
        sc = lax.dot_general(q, kc_ref[...], nt, preferred_element_type=F32) if has_ctx else None
        return s, sc

    ahead = min(ATTN_LOOKAHEAD, nsub)
    pending = [scores(r) for r in range(ahead)]
    for r in range(nsub):
        s, sc = pending.pop(0)
        if r + ahead < nsub:
            pending.append(scores(r + ahead))
        m = jnp.max(s, axis=-1, keepdims=True)
        if has_ctx:
            m = jnp.maximum(m, jnp.max(sc, axis=-1, keepdims=True))
        p = jnp.exp2(s - m)
        den = jnp.sum(p, axis=-1, keepdims=True)
        o = jnp.dot(p.astype(BF16), v_ref[...], preferred_element_type=F32)
        if has_ctx:
            pc = jnp.exp2(sc - m)
            den = den + jnp.sum(pc, axis=-1, keepdims=True)
            o = o + jnp.dot(pc.astype(BF16), vc_ref[...], preferred_element_type=F32)
        o_ref[r * ts:(r + 1) * ts, :] = (o / den).astype(BF16)


def _attention(q, k, v, kc, vc, tq):
    B, H, T, _ = q.shape
    S = k.shape[2]
    has_ctx = kc is not None
    in_specs = [
        pl.BlockSpec((None, None, tq, 2 * LANE), lambda b, h, i: (b, h, i, 0)),
        pl.BlockSpec((None, None, S, 2 * LANE), lambda b, h, i: (b, h, 0, 0)),
        pl.BlockSpec((None, None, S, LANE), lambda b, h, i: (b, h, 0, 0)),
    ]
    args = [q, k, v]
    if has_ctx:
        P = kc.shape[2]
        in_specs += [
            pl.BlockSpec((None, None, P, 2 * LANE), lambda b, h, i: (b, h, 0, 0)),
            pl.BlockSpec((None, None, P, LANE), lambda b, h, i: (b, h, 0, 0)),
        ]
        args += [kc, vc]
    return pl.pallas_call(
        functools.partial(_attn_kernel, has_ctx=has_ctx, nsub=max(1, tq // ATTN_CHAIN_ROWS)),
        grid=(B, H, T // tq),
        in_specs=in_specs,
        out_specs=pl.BlockSpec((None, tq, LANE), lambda b, h, i: (b, i, h)),
        out_shape=jax.ShapeDtypeStruct((B, T, H * LANE), BF16),
        compiler_params=_params(("arbitrary", "arbitrary", "arbitrary")),
        name="attention",
    )(*args)


def _shift_rows(x, k, row):
    n = x.shape[0]
    rolled = pltpu.roll(x, k % n, 0)
    valid = (row >= k) if k > 0 else (row < n + k)
    return jnp.where(valid, rolled, 0.0)


def _scan_chunk(a, b, carry, sub, reverse):
    n = a.shape[0]
    ng = n // SUBLANE
    hs = [None] * ng
    for j in (range(ng - 1, -1, -1) if reverse else range(ng)):
        aj = a[j * SUBLANE:(j + 1) * SUBLANE]
        bj = b[j * SUBLANE:(j + 1) * SUBLANE]
        d = 1
        while d < SUBLANE:
            shift = SUBLANE - d if reverse else d
            valid = (sub < SUBLANE - d) if reverse else (sub >= d)
            a_s = pltpu.roll(aj, shift, 0)
            b_s = pltpu.roll(bj, shift, 0)
            bj = jnp.where(valid, aj * b_s, 0.0) + bj
            aj = jnp.where(valid, aj * a_s, aj)
            d *= 2
        h = aj * carry + bj
        hs[j] = h
        carry = h[0:1] if reverse else h[SUBLANE - 1:SUBLANE]
    return jnp.concatenate(hs, axis=0), carry


def _sigmoid(x):
    return 0.5 * jnp.tanh(0.5 * x) + 0.5


def _softplus(x):
    return jnp.maximum(x, 0.0) + jnp.log(1.0 + jnp.exp(-jnp.abs(x)))


def _gelu_tanh(x):
    return 0.5 * x * (1.0 + jnp.tanh(0.7978845608028654 * (x + 0.044715 * x * x * x)))


def _lru_kernel(u_ref, ug_ref, cw_ref, cb_ref, w_ref, b_ref, lam_ref, h0_ref, y_ref, st_ref, xc_scr, hf_scr, *, tc):
    T, C = xc_scr.shape
    nch = T // tc
    row_t = lax.broadcasted_iota(jnp.int32, (T, C), 0)
    u = u_ref[...]
    cw = cw_ref[...]
    xc = (cb_ref[...] + cw[0:1] * _shift_rows(u, 2, row_t) + cw[1:2] * _shift_rows(u, 1, row_t)
          + cw[2:3] * u + cw[3:4] * _shift_rows(u, -1, row_t))
    xc_scr[...] = xc

    sub = lax.broadcasted_iota(jnp.int32, (SUBLANE, C), 0)
    sp = _softplus(-lam_ref[...])
    h0 = h0_ref[...]

    def gates(xcj, d):
        g = jnp.dot(xcj.astype(BF16), w_ref[:, 2 * d * C:2 * (d + 1) * C], preferred_element_type=F32)
        g = g + b_ref[:, 2 * d * C:2 * (d + 1) * C]
        r = _sigmoid(g[:, :C])
        i = _sigmoid(g[:, C:])
        log_a = -LRU_C * r * sp[d:d + 1]
        a = jnp.exp(log_a)
        b = jnp.sqrt(jnp.tanh(-log_a) * (a * a + 1.0)) * (i * xcj)
        return a, b

    def fwd(j, carry):
        t0 = pl.multiple_of(j * tc, tc)
        xcj = xc_scr[pl.ds(t0, tc), :]
        a, b = gates(xcj, 0)
        h, carry = _scan_chunk(a, b, carry, sub, False)
        hf_scr[pl.ds(t0, tc), :] = h
        return carry

    hf_last = lax.fori_loop(0, nch, fwd, h0[0:1])
    st_ref[0:1, :] = hf_last

    def bwd(jj, carry):
        j = nch - 1 - jj
        t0 = pl.multiple_of(j * tc, tc)
        xcj = xc_scr[pl.ds(t0, tc), :]
        a, b = gates(xcj, 1)
        h, carry = _scan_chunk(a, b, carry, sub, True)
        y = (hf_scr[pl.ds(t0, tc), :] + h) * _gelu_tanh(ug_ref[pl.ds(t0, tc), :])
        y_ref[pl.ds(t0, tc), :] = y.astype(BF16)
        return carry

    hb_first = lax.fori_loop(0, nch, bwd, h0[1:2])
    st_ref[1:2, :] = hb_first


def _lru_mixer(z3, h0, l, W, tc):
    B, T, _ = z3.shape
    C = MXU_TILE
    nct = LRU_WIDTH // C
    return pl.pallas_call(
        functools.partial(_lru_kernel, tc=tc),
        grid=(B, nct),
        in_specs=[
            pl.BlockSpec((None, T, C), lambda b, c: (b, 0, OFF_LRU // C + c)),
            pl.BlockSpec((None, T, C), lambda b, c: (b, 0, OFF_GATE // C + c)),
            pl.BlockSpec((None, 4, C), lambda b, c: (l, 0, c)),
            pl.BlockSpec((None, 1, C), lambda b, c: (l, 0, c)),
            pl.BlockSpec((None, None, C, 4 * C), lambda b, c: (l, c, 0, 0)),
            pl.BlockSpec((None, None, 1, 4 * C), lambda b, c: (l, c, 0, 0)),
            pl.BlockSpec((None, 2, C), lambda b, c: (l, 0, c)),
            pl.BlockSpec((None, 2, C), lambda b, c: (b, 0, c)),
        ],
        out_specs=[
            pl.BlockSpec((None, T, C), lambda b, c: (b, 0, c)),
            pl.BlockSpec((None, 2, C), lambda b, c: (b, 0, c)),
        ],
        out_shape=[
            jax.ShapeDtypeStruct((B, T, LRU_WIDTH), BF16),
            jax.ShapeDtypeStruct((B, 2, LRU_WIDTH), F32),
        ],
        scratch_shapes=[pltpu.VMEM((T, C), F32), pltpu.VMEM((T, C), F32)],
        compiler_params=_params(("arbitrary", "arbitrary")),
        name="lru_mixer",
    )(z3, z3, W["conv_w"], W["conv_b"], W["lru_w"], W["lru_b"], W["lru_lambda"], h0)


def _pool_kernel(u_ref, w_ref, sc_ref, o_ref):
    T, C = u_ref.shape
    g = pl.program_id(1)
    for gi, win in enumerate(POOL_WINDOWS):
        @pl.when(g == gi)
        def _(win=win):
            row = lax.broadcasted_iota(jnp.int32, (T, C), 0)
            u = u_ref[...]
            s = u + _shift_rows(u, 1, row)
            w = 4
            while w <= win:
                q = w // 4
                s = _shift_rows(s, q, row) + _shift_rows(s, -q, row)
                w *= 2
            half = win // 2
            cnt = (jnp.minimum(row + half, T) - jnp.maximum(row - half, 0)).astype(F32)
            d = s / cnt - u
            y = jnp.dot(d.astype(BF16), w_ref[...], preferred_element_type=F32) * sc_ref[...]
            o_ref[...] = y.astype(BF16)


def _pool_mixer(z3, l, W):
    B, T, _ = z3.shape
    C = MXU_TILE
    ng = POOL_WIDTH // C
    return pl.pallas_call(
        _pool_kernel,
        grid=(B, ng),
        in_specs=[
            pl.BlockSpec((None, T, C), lambda b, g: (b, 0, OFF_POOL // C + g)),
            pl.BlockSpec((None, None, C, C), lambda b, g: (l, g, 0, 0)),
            pl.BlockSpec((None, 1, C), lambda b, g: (l, 0, g)),
        ],
        out_specs=pl.BlockSpec((None, T, C), lambda b, g: (b, 0, g)),
        out_shape=jax.ShapeDtypeStruct((B, T, POOL_WIDTH), BF16),
        compiler_params=_params(("arbitrary", "arbitrary")),
        name="pool_mixer",
    )(z3, W["pool_w"], W["pool_scale"])


def _merge_kernel(a_ref, r_ref, p_ref, g0_ref, g1_ref, g2_ref, wa_ref, wr_ref, wp_ref, o_ref):
    ya = jnp.dot(a_ref[...], wa_ref[...], preferred_element_type=F32)
    yr = jnp.dot(r_ref[...], wr_ref[...], preferred_element_type=F32)
    yp = jnp.dot(p_ref[...], wp_ref[...], preferred_element_type=F32)
    m = g0_ref[...].astype(F32) * ya + g1_ref[...].astype(F32) * yr + g2_ref[...].astype(F32) * yp
    o_ref[...] = m.astype(BF16)


def _merge(attn, lru, pool, gates, l, W, tm):
    M, K = attn.shape
    D = W["w_mla_o"].shape[-1]
    tn = _pick(D, (512, 256))
    nbr = D // tn
    act = pl.BlockSpec((tm, K), lambda i, j: (i, 0))
    wsp = pl.BlockSpec((None, K, tn), lambda i, j: (l, 0, j))
    gsp = lambda k: pl.BlockSpec((tm, tn), lambda i, j: (i, k * nbr + j))
    return pl.pallas_call(
        _merge_kernel,
        grid=(M // tm, D // tn),
        in_specs=[act, act, act, gsp(0), gsp(1), gsp(2), wsp, wsp, wsp],
        out_specs=pl.BlockSpec((tm, tn), lambda i, j: (i, j)),
        out_shape=jax.ShapeDtypeStruct((M, D), BF16),
        compiler_params=_params(("arbitrary", "arbitrary")),
        name="merge",
    )(attn, lru, pool, gates, gates, gates, W["w_mla_o"], W["w_lru_o"], W["w_pool_o"])


def _outproj_kernel(m_ref, w_ref, x_ref, g_ref, o_ref):
    y = jnp.dot(m_ref[...], w_ref[...], preferred_element_type=F32)
    o_ref[...] = x_ref[...] + g_ref[...] * y


def _out_proj(merged, x2d, mod4, row_of, l, W, tm):
    M, D = x2d.shape
    tn = _pick(D, (512, 256))
    nbr = D // tn
    return pl.pallas_call(
        _outproj_kernel,
        grid=(M // tm, D // tn),
        in_specs=[
            pl.BlockSpec((tm, D), lambda i, j: (i, 0)),
            pl.BlockSpec((None, D, tn), lambda i, j: (l, 0, j)),
            pl.BlockSpec((tm, tn), lambda i, j: (i, j)),
            pl.BlockSpec((None, None, 1, tn), lambda i, j: (l, row_of(i * tm), 0, 2 * nbr + j)),
        ],
        out_specs=pl.BlockSpec((tm, tn), lambda i, j: (i, j)),
        out_shape=jax.ShapeDtypeStruct((M, D), F32),
        compiler_params=_params(("arbitrary", "arbitrary")),
        name="out_proj",
    )(merged, W["w_out"], x2d, mod4)


def _ffn_kernel(x_ref, nw_ref, sh_ref, sc_ref, g_ref, w1_ref, w2_ref, o_ref, h_scr):
    f = pl.program_id(1)

    @pl.when(f == 0)
    def _():
        x = x_ref[...]
        ms = jnp.mean(x * x, axis=-1, keepdims=True)
        y = x * lax.rsqrt(ms + EPS) * nw_ref[...]
        h_scr[...] = (y * (1.0 + sc_ref[...]) + sh_ref[...]).astype(BF16)
        o_ref[...] = jnp.zeros_like(o_ref)

    a = jnp.dot(h_scr[...], w1_ref[...], preferred_element_type=F32)
    a = jnp.maximum(a, 0.0)
    o_ref[...] += jnp.dot((a * a).astype(BF16), w2_ref[...], preferred_element_type=F32)

    @pl.when(f == pl.num_programs(1) - 1)
    def _():
        o_ref[...] = x_ref[...] + g_ref[...] * o_ref[...]


def _ffn(x2d, mod4, row_of, l, W, tm):
    M, D = x2d.shape
    F = W["w_ff1"].shape[-1]
    tf = _pick(F, (1024, 512, 256))
    modv = lambda k: pl.BlockSpec((None, None, 1, D), lambda i, f: (l, row_of(i * tm), 0, k))
    return pl.pallas_call(
        _ffn_kernel,
        grid=(M // tm, F // tf),
        in_specs=[
            pl.BlockSpec((tm, D), lambda i, f: (i, 0)),
            pl.BlockSpec((None, 1, D), lambda i, f: (l, 0, 0)),
            modv(3), modv(4), modv(5),
            pl.BlockSpec((None, D, tf), lambda i, f: (l, 0, f)),
            pl.BlockSpec((None, tf, D), lambda i, f: (l, f, 0)),
        ],
        out_specs=pl.BlockSpec((tm, D), lambda i, f: (i, 0)),
        out_shape=jax.ShapeDtypeStruct((M, D), F32),
        scratch_shapes=[pltpu.VMEM((tm, D), BF16)],
        compiler_params=_params(("arbitrary", "arbitrary")),
        name="ffn",
    )(x2d, W["norm2"], mod4, mod4, mod4, W["w_ff1"], W["w_ff2"])


def _pad_lanes(x, n=LANE):
    return jnp.pad(x, [(0, 0)] * (x.ndim - 1) + [(0, n - x.shape[-1])])


def _rot_half(x):
    h = x.shape[-1] // 2
    return jnp.concatenate([-x[..., h:], x[..., :h]], axis=-1)


def _swap_half(x):
    h = x.shape[-1] // 2
    return jnp.concatenate([x[..., h:], x[..., :h]], axis=-1)


def _norm_gains(g):
    gr = g[:, QK_NOPE:]
    return (g[:, None, :QK_NOPE], _pad_lanes(gr)[:, None, :], _pad_lanes(_swap_half(gr))[:, None, :])


def _block_diag_tiles(w, per):
    *lead, nb, s, _ = w.shape
    w = w.reshape(*lead, nb // per, per, s, s)
    t = jnp.einsum("...kij,km->...kimj", w, jnp.eye(per, dtype=w.dtype))
    return t.reshape(*lead, nb // per, per * s, per * s)


def _pack_weights(w_in, q_a_norm, w_qb, kv_a_norm, w_kvb, q_norm, k_norm, w_mla_o, conv_w, conv_b,
                  lru_wa, lru_ba, lru_wx, lru_bx, lru_lambda, w_lru_o, pool_w, pool_scale, w_pool_o,
                  w_out, w_ff1, w_ff2, norm1_w, norm2_w):
    L, D, _ = w_in.shape
    o_kr = Q_LORA + KV_LORA
    wkr = w_in[..., o_kr:o_kr + QK_ROPE]
    w_in_p = jnp.concatenate(
        [w_in[..., :o_kr], _pad_lanes(wkr), _pad_lanes(_rot_half(wkr)), w_in[..., o_kr + QK_ROPE:]], axis=-1)

    wq = w_qb.reshape(L, Q_LORA, N_HEADS, QK_DIM)
    wq_rope = wq[..., QK_NOPE:]
    w_q = jnp.concatenate([wq[..., :QK_NOPE].reshape(L, Q_LORA, -1),
                           _pad_lanes(wq_rope).reshape(L, Q_LORA, -1)], axis=-1)
    w_qrot = _pad_lanes(_rot_half(wq_rope)).reshape(L, Q_LORA, -1)

    wkv = w_kvb.reshape(L, KV_LORA, N_HEADS, QK_NOPE + V_DIM)
    w_kv = jnp.concatenate([wkv[..., :QK_NOPE].reshape(L, KV_LORA, -1),
                            wkv[..., QK_NOPE:].reshape(L, KV_LORA, -1)], axis=-1)

    qg = _norm_gains(q_norm)
    kg = _norm_gains(k_norm)

    per = MXU_TILE // lru_wa.shape[-1]
    ta = _block_diag_tiles(lru_wa, per)
    tx = _block_diag_tiles(lru_wx, per)
    lru_w = jnp.concatenate([ta[:, 0], tx[:, 0], ta[:, 1], tx[:, 1]], axis=-1)
    nct = lru_w.shape[1]
    ba = lru_ba.reshape(L, 2, nct, 1, MXU_TILE)
    bx = lru_bx.reshape(L, 2, nct, 1, MXU_TILE)
    lru_b = jnp.concatenate([ba[:, 0], bx[:, 0], ba[:, 1], bx[:, 1]], axis=-1)

    return dict(
        w_in=w_in_p.astype(BF16), norm1=norm1_w[:, None, :], norm2=norm2_w[:, None, :],
        q_a_norm=q_a_norm[:, None, :], kv_a_norm=kv_a_norm[:, None, :],
        w_q=w_q.astype(BF16), w_qrot=w_qrot.astype(BF16), w_kv=w_kv.astype(BF16),
        qg_n=qg[0], qg_r=qg[1], qg_s=qg[2], kg_n=kg[0], kg_r=kg[1], kg_s=kg[2],
        w_mla_o=w_mla_o.astype(BF16), w_lru_o=w_lru_o.astype(BF16), w_pool_o=w_pool_o.astype(BF16),
        conv_w=conv_w, conv_b=conv_b[:, None, :], lru_w=lru_w.astype(BF16), lru_b=lru_b,
        lru_lambda=lru_lambda, pool_w=pool_w.astype(BF16), pool_scale=pool_scale[:, None, :],
        w_out=w_out.astype(BF16), w_ff1=w_ff1.astype(BF16), w_ff2=w_ff2.astype(BF16),
    )


def _rope_tables(T):
    rows = T // GRID_W
    row = jnp.repeat(jnp.arange(rows), GRID_W).astype(F32)
    col = jnp.tile(jnp.arange(GRID_W), rows).astype(F32)
    n_freq = QK_ROPE // 4
    inv = ROPE_BASE ** (-(jnp.arange(n_freq, dtype=F32) / n_freq))
    ang = jnp.concatenate([row[:, None] * inv, col[:, None] * inv], axis=-1)
    cos, sin = jnp.cos(ang), jnp.sin(ang)
    return (_pad_lanes(jnp.concatenate([cos, cos], axis=-1)), _pad_lanes(jnp.concatenate([sin, sin], axis=-1)))


def _trunk_layer(x2d, B, T, mod4, row_of, shared_cond, l, W, rope, ctx, h0):
    M = B * T
    span = M if shared_cond else T
    tm = _pick(span, (1024, 512, 256, 128))
    tmp = _pick(T, (512, 256, 128))
    z, gates = _in_proj(x2d, mod4, row_of, W["norm1"], W["w_in"], l, tm)
    z3 = z.reshape(B, T, -1)

    q = _q_prep(z, B, T, l, W, rope, tmp)
    k, v, ckv = _k_prep(
        z, z, B, T, l, W, rope, tmp,
        kv_spec=pl.BlockSpec((tmp, KV_LORA), lambda i: (i, OFF_KV // KV_LORA)),
        kr_spec=pl.BlockSpec((tmp, LANE), lambda i: (i, OFF_KR // LANE)),
        krot_spec=pl.BlockSpec((tmp, LANE), lambda i: (i, OFF_KROT // LANE)),
        normalize=True)
    if ctx is None:
        kc = vc = None
    else:
        ckv_c, kr_c = ctx
        P = ckv_c.shape[2]
        tp = _pick(P, (256, 128))
        npb = P // tp
        kc, vc = _k_prep(
            ckv_c, kr_c, B, P, l, W, None, tp,
            kv_spec=pl.BlockSpec((None, None, tp, KV_LORA), lambda i: (i // npb, l, i % npb, 0)),
            kr_spec=pl.BlockSpec((None, None, tp, LANE), lambda i: (i // npb, l, i % npb, 0)),
            krot_spec=None, normalize=False)
    attn = _attention(q, k, v, kc, vc, _pick(T, (1024, 512, 256, 128)))

    lru, state = _lru_mixer(z3, h0, l, W, _pick(T, (256, 128)))
    pool = _pool_mixer(z3, l, W)

    merged = _merge(attn.reshape(M, -1), lru.reshape(M, -1), pool.reshape(M, -1), gates, l, W, tm)
    x2d = _out_proj(merged, x2d, mod4, row_of, l, W, tm)
    x2d = _ffn(x2d, mod4, row_of, l, W, _pick(span, (512, 256, 128)))
    return x2d, z, ckv, state


def kernel(x_prompt, x_sample, c, cache_ckv, cache_krope, state_lru, c_ctx, w_mod, b_mod, norm1_w, norm2_w, w_in, q_a_norm, w_qb, kv_a_norm, w_kvb, q_norm, k_norm, w_mla_o, conv_w, conv_b, lru_wa, lru_ba, lru_wx, lru_bx, lru_lambda, w_lru_o, pool_w, pool_scale, w_pool_o, w_out, w_ff1, w_ff2):
    L = w_in.shape[0]
    Bc, Tc, D = x_prompt.shape
    Bs, Ts, _ = x_sample.shape

    W = _pack_weights(w_in, q_a_norm, w_qb, kv_a_norm, w_kvb, q_norm, k_norm, w_mla_o, conv_w, conv_b,
                      lru_wa, lru_ba, lru_wx, lru_bx, lru_lambda, w_lru_o, pool_w, pool_scale, w_pool_o,
                      w_out, w_ff1, w_ff2, norm1_w, norm2_w)

    rows = Bs + 1
    rpad = -rows % 8
    cond = jnp.concatenate([c, c_ctx[None, :], jnp.zeros((rpad, D), F32)], axis=0)
    mod = _modulation(cond, w_mod, b_mod)
    mod4 = mod.reshape(L, rows + rpad, 1, N_MOD * D)

    rope = _rope_tables(Ts)
    kr_cache = _pad_lanes(cache_krope)
    zero_state = jnp.zeros((Bc, 2, LRU_WIDTH), F32)

    row_smp = lambda t: t // Ts
    row_ctx = lambda t: Bs

    xp = x_prompt.reshape(Bc * Tc, D)
    xs = x_sample.reshape(Bs * Ts, D)
    ckv_list, kr_list, st_list = [], [], []
    for l in range(L):
        xp, zc, ckv, st = _trunk_layer(xp, Bc, Tc, mod4, row_ctx, True, l, W, None, None, zero_state)
        ckv_list.append(ckv.reshape(Bc, Tc, KV_LORA))
        kr_list.append(zc[:, OFF_KR:OFF_KR + QK_ROPE].reshape(Bc, Tc, QK_ROPE))
        st_list.append(st)
        xs, _, _, _ = _trunk_layer(xs, Bs, Ts, mod4, row_smp, False, l, W, rope,
                                   (cache_ckv, kr_cache), state_lru[:, l])
    return (xp.reshape(Bc, Tc, D), xs.reshape(Bs, Ts, D), jnp.stack(ckv_list, axis=1),
            jnp.stack(kr_list, axis=1), jnp.stack(st_list, axis=1))
```

```python
import functools

import jax
import jax.numpy as jnp
from jax import lax
from jax.experimental import pallas as pl
from jax.experimental.pallas import tpu as pltpu

F32 = jnp.float32
BF16 = jnp.bfloat16

EPS = 1e-6
GRID_W = 64
N_HEADS = 8
QK_NOPE = 128
QK_ROPE = 64
V_DIM = 128
QK_DIM = QK_NOPE + QK_ROPE
Q_LORA = 512
KV_LORA = 256
ROPE_BASE = 10000.0
LRU_WIDTH = 1024
LRU_C = 8.0
POOL_WIDTH = 1024
POOL_WINDOWS = (2, 4, 8, 16)
N_MOD = 6

LANE = 128
SUBLANE = 8
MXU_TILE = 256
VMEM_LIMIT = 56 * 2**20
ATTN_CHAIN_ROWS = 256
ATTN_LOOKAHEAD = 2
Q_PRESCALE =QK_DIM ** -0.5 * 1.4426950408889634

OFF_Q = 0
OFF_KV = Q_LORA
OFF_KR = OFF_KV + KV_LORA
OFF_KROT = OFF_KR + LANE
OFF_LRU = OFF_KROT + LANE
OFF_GATE = OFF_LRU + LRU_WIDTH
OFF_POOL = OFF_GATE + LRU_WIDTH
OFF_BR = OFF_POOL + POOL_WIDTH


def _params(sem):
    return pltpu.CompilerParams(dimension_semantics=sem, vmem_limit_bytes=VMEM_LIMIT)


def _pick(n, prefs):
    for p in prefs:
        if n % p == 0:
            return p
    return n


def _mod_kernel(c_ref, w_ref, b_ref, o_ref):
    c = c_ref[...]
    s = c * jax.nn.sigmoid(c)
    o_ref[...] = jnp.dot(s.astype(BF16), w_ref[...].astype(BF16), preferred_element_type=F32) + b_ref[...]


def _modulation(cond, w_mod, b_mod):
    L, D, N = w_mod.shape
    R = cond.shape[0]
    tn = _pick(N, (1024, 512, 256, 128))
    return pl.pallas_call(
        _mod_kernel,
        grid=(L, N // tn),
        in_specs=[
            pl.BlockSpec((R, D), lambda l, j: (0, 0)),
            pl.BlockSpec((None, D, tn), lambda l, j: (l, 0, j)),
            pl.BlockSpec((None, 1, tn), lambda l, j: (l, 0, j)),
        ],
        out_specs=pl.BlockSpec((None, R, tn), lambda l, j: (l, 0, j)),
        out_shape=jax.ShapeDtypeStruct((L, R, N), F32),
        compiler_params=_params(("arbitrary", "arbitrary")),
        name="modulation",
    )(cond, w_mod, b_mod.reshape(L, 1, N))


def _inproj_kernel(x_ref, nw_ref, sh_ref, sc_ref, w_ref, z_ref, g_ref, h_scr, *, n_lin):
    j = pl.program_id(1)

    @pl.when(j == 0)
    def _():
        x = x_ref[...]
        ms = jnp.mean(x * x, axis=-1, keepdims=True)
        y = x * lax.rsqrt(ms + EPS) * nw_ref[...]
        h_scr[...] = (y * (1.0 + sc_ref[...]) + sh_ref[...]).astype(BF16)

    @pl.when(j < n_lin)
    def _():
        z_ref[...] = jnp.dot(h_scr[...], w_ref[...], preferred_element_type=F32)

    @pl.when(j >= n_lin)
    def _():
        g_ref[...] = _sigmoid(jnp.dot(h_scr[...], w_ref[...], preferred_element_type=F32)).astype(BF16)


def _in_proj(x2d, mod4, row_of, norm_w, w_in_p, l, tm):
    M, D = x2d.shape
    N = w_in_p.shape[-1]
    tn = _pick(OFF_BR, (1024, 512, 256))
    n_lin = OFF_BR // tn
    return pl.pallas_call(
        functools.partial(_inproj_kernel, n_lin=n_lin),
        grid=(M // tm, N // tn),
        in_specs=[
            pl.BlockSpec((tm, D), lambda i, j: (i, 0)),
            pl.BlockSpec((None, 1, D), lambda i, j: (l, 0, 0)),
            pl.BlockSpec((None, None, 1, D), lambda i, j: (l, row_of(i * tm), 0, 0)),
            pl.BlockSpec((None, None, 1, D), lambda i, j: (l, row_of(i * tm), 0, 1)),
            pl.BlockSpec((None, D, tn), lambda i, j: (l, 0, j)),
        ],
        out_specs=[
            pl.BlockSpec((tm, tn), lambda i, j: (i, jnp.minimum(j, n_lin - 1))),
            pl.BlockSpec((tm, tn), lambda i, j: (i, jnp.maximum(j - n_lin, 0))),
        ],
        out_shape=[
            jax.ShapeDtypeStruct((M, OFF_BR), F32),
            jax.ShapeDtypeStruct((M, N - OFF_BR), BF16),
        ],
        scratch_shapes=[pltpu.VMEM((tm, D), BF16)],
        compiler_params=_params(("arbitrary", "arbitrary")),
        name="in_proj",
    )(x2d, norm_w, mod4, mod4, w_in_p)


def _head_scale(nope, rope_sq_sum):
    ss = jnp.sum(nope * nope, axis=-1, keepdims=True) + rope_sq_sum
    return lax.rsqrt(ss * (1.0 / QK_DIM) + EPS)


def _qprep_kernel(*refs, use_rope):
    if use_rope:
        ql_ref, an_ref, w_ref, wrot_ref, gn_ref, gr_ref, gs_ref, cos_ref, sin_ref, q_ref = refs
    else:
        ql_ref, an_ref, w_ref, gn_ref, gr_ref, q_ref = refs
    ql = ql_ref[...]
    ms = jnp.mean(ql * ql, axis=-1, keepdims=True)
    qn = (ql * lax.rsqrt(ms + EPS) * an_ref[...]).astype(BF16)
    qq = jnp.dot(qn, w_ref[...], preferred_element_type=F32)
    if use_rope:
        qrot = jnp.dot(qn, wrot_ref[...], preferred_element_type=F32)
        cos = cos_ref[...]
        sin = sin_ref[...]
        gs = gs_ref[...]
    gn = gn_ref[...]
    gr = gr_ref[...]
    hw = N_HEADS * LANE
    for h in range(N_HEADS):
        nope = qq[:, h * LANE:(h + 1) * LANE]
        rope = qq[:, hw + h * LANE:hw + (h + 1) * LANE]
        s = _head_scale(nope, jnp.sum(rope * rope, axis=-1, keepdims=True))
        if use_rope:
            r = gr * rope * cos + gs * qrot[:, h * LANE:(h + 1) * LANE] * sin
        else:
            r = gr * rope
        s = s * Q_PRESCALE
        q_ref[h, :, 0:LANE] = (nope * s * gn).astype(BF16)
        q_ref[h, :, LANE:2 * LANE] = (r * s).astype(BF16)


def _q_prep(z, B, T, l, W, rope, tm):
    M = z.shape[0]
    nb = T // tm
    use_rope = rope is not None
    wspec = lambda n: pl.BlockSpec((None, Q_LORA, n), lambda i: (l, 0, 0))
    vec = pl.BlockSpec((None, 1, LANE), lambda i: (l, 0, 0))
    in_specs = [
        pl.BlockSpec((tm, Q_LORA), lambda i: (i, OFF_Q // Q_LORA)),
        pl.BlockSpec((None, 1, Q_LORA), lambda i: (l, 0, 0)),
        wspec(2 * N_HEADS * LANE),
    ]
    args = [z, W["q_a_norm"], W["w_q"]]
    if use_rope:
        tab = pl.BlockSpec((tm, LANE), lambda i: (i % nb, 0))
        in_specs += [wspec(N_HEADS * LANE), vec, vec, vec, tab, tab]
        args += [W["w_qrot"], W["qg_n"], W["qg_r"], W["qg_s"], rope[0], rope[1]]
    else:
        in_specs += [vec, vec]
        args += [W["qg_n"], W["qg_r"]]
    return pl.pallas_call(
        functools.partial(_qprep_kernel, use_rope=use_rope),
        grid=(M // tm,),
        in_specs=in_specs,
        out_specs=pl.BlockSpec((None, N_HEADS, tm, 2 * LANE), lambda i: (i // nb, 0, i % nb, 0)),
        out_shape=jax.ShapeDtypeStruct((B, N_HEADS, T, 2 * LANE), BF16),
        compiler_params=_params(("arbitrary",)),
        name="q_prep",
    )(*args)


def _kprep_kernel(*refs, use_rope, normalize):
    refs = list(refs)
    kv_ref = refs.pop(0)
    kr_ref = refs.pop(0)
    krot_ref = refs.pop(0) if use_rope else None
    an_ref = refs.pop(0) if normalize else None
    w_ref = refs.pop(0)
    gn_ref = refs.pop(0)
    gr_ref = refs.pop(0)
    if use_rope:
        gs_ref, cos_ref, sin_ref = refs.pop(0), refs.pop(0), refs.pop(0)
    k_ref = refs.pop(0)
    v_ref = refs.pop(0)
    ckv_ref = refs.pop(0) if normalize else None

    kv = kv_ref[...]
    if normalize:
        ms = jnp.mean(kv * kv, axis=-1, keepdims=True)
        kv = kv * lax.rsqrt(ms + EPS) * an_ref[...]
        ckv_ref[...] = kv
    kk = jnp.dot(kv.astype(BF16), w_ref[...], preferred_element_type=F32)
    kr = kr_ref[...]
    ssr = jnp.sum(kr * kr, axis=-1, keepdims=True)
    if use_rope:
        base = gr_ref[...] * kr * cos_ref[...] + gs_ref[...] * krot_ref[...] * sin_ref[...]
    else:
        base = gr_ref[...] * kr
    gn = gn_ref[...]
    hw = N_HEADS * LANE
    for h in range(N_HEADS):
        nope = kk[:, h * LANE:(h + 1) * LANE]
        s = _head_scale(nope, ssr)
        k_ref[h, :, 0:LANE] = (nope * s * gn).astype(BF16)
        k_ref[h, :, LANE:2 * LANE] = (base * s).astype(BF16)
        v_ref[h] = kk[:, hw + h * LANE:hw + (h + 1) * LANE].astype(BF16)


def _k_prep(kv_src, kr_src, B, T, l, W, rope, tm, *, kv_spec, kr_spec, krot_spec, normalize):
    M = B * T
    nb = T // tm
    use_rope = rope is not None
    vec = pl.BlockSpec((None, 1, LANE), lambda i: (l, 0, 0))
    in_specs = [kv_spec, kr_spec]
    args = [kv_src, kr_src]
    if use_rope:
        in_specs.append(krot_spec)
        args.append(kr_src)
    if normalize:
        in_specs.append(pl.BlockSpec((None, 1, KV_LORA), lambda i: (l, 0, 0)))
        args.append(W["kv_a_norm"])
    in_specs += [pl.BlockSpec((None, KV_LORA, 2 * N_HEADS * LANE), lambda i: (l, 0, 0)), vec, vec]
    args += [W["w_kv"], W["kg_n"], W["kg_r"]]
    if use_rope:
        tab = pl.BlockSpec((tm, LANE), lambda i: (i % nb, 0))
        in_specs += [vec, tab, tab]
        args += [W["kg_s"], rope[0], rope[1]]
    out_specs = [
        pl.BlockSpec((None, N_HEADS, tm, 2 * LANE), lambda i: (i // nb, 0, i % nb, 0)),
        pl.BlockSpec((None, N_HEADS, tm, LANE), lambda i: (i // nb, 0, i % nb, 0)),
    ]
    out_shape = [
        jax.ShapeDtypeStruct((B, N_HEADS, T, 2 * LANE), BF16),
        jax.ShapeDtypeStruct((B, N_HEADS, T, LANE), BF16),
    ]
    if normalize:
        out_specs.append(pl.BlockSpec((tm, KV_LORA), lambda i: (i, 0)))
        out_shape.append(jax.ShapeDtypeStruct((M, KV_LORA), F32))
    return pl.pallas_call(
        functools.partial(_kprep_kernel, use_rope=use_rope, normalize=normalize),
        grid=(M // tm,),
        in_specs=in_specs,
        out_specs=out_specs,
        out_shape=out_shape,
        compiler_params=_params(("arbitrary",)),
        name="k_prep",
    )(*args)


def _attn_kernel(*refs, has_ctx, nsub):
    if has_ctx:
        q_ref, k_ref, v_ref, kc_ref, vc_ref, o_ref = refs
    else:
        q_ref, k_ref, v_ref, o_ref = refs
    nt = (((1,), (1,)), ((), ()))
    ts = q_ref.shape[0] // nsub
    tiles = []
    for kref, vref in ((k_ref, v_ref), (kc_ref, vc_ref)) if has_ctx else ((k_ref, v_ref),):
        kt = min(MXU_TILE, kref.shape[0])
        tiles += [(kref, vref, t * kt, kt) for t in range(kref.shape[0] // kt)]

    def fold_lanes(x, op):
        out = x[:, :LANE]
        for c in range(1, x.shape[1] // LANE):
            out = op(out, x[:, c * LANE:(c + 1) * LANE])
        return out

    def scores(r):
        q = q_ref[r * ts:(r + 1) * ts, :]
        s_tiles = []
        mrun = None
        for kref, _, off, kt in tiles:
            st = lax.dot_general(q, kref[off:off + kt, :], nt, preferred_element_type=F32)
            s_tiles.append(st)
            tmax = fold_lanes(st, jnp.maximum)
            mrun = tmax if mrun is None else jnp.maximum(mrun, tmax)
        return s_tiles, jnp.max(mrun, axis=-1, keepdims=True)

    def weights_and_values(r, s_tiles, m):
        lrun = None
        o = None
        for st, (_, vref, off, kt) in zip(s_tiles, tiles):
            p = jnp.exp2(st - m)
            psum = fold_lanes(p, jnp.add)
            lrun = psum if lrun is None else lrun + psum
            ot = jnp.dot(p.astype(BF16), vref[off:off + kt, :], preferred_element_type=F32)
            o = ot if o is None else o + ot
        den = jnp.sum(lrun, axis=-1, keepdims=True)
        o_ref[r * ts:(r + 1) * ts, :] = (o / den).astype(BF16)

    ahead = min(ATTN_LOOKAHEAD, nsub)
    pending = [scores(r) for r in range(ahead)]
    for r in range(nsub):
        s_tiles, m = pending.pop(0)
        if r + ahead < nsub:
            pending.append(scores(r + ahead))
        weights_and_values(r, s_tiles, m)


def _attention(q, k, v, kc, vc, tq):
    B, H, T, _ = q.shape
    S = k.shape[2]
    has_ctx = kc is not None
    in_specs = [
        pl.BlockSpec((None, None, tq, 2 * LANE), lambda b, h, i: (b, h, i, 0)),
        pl.BlockSpec((None, None, S, 2 * LANE), lambda b, h, i: (b, h, 0, 0)),
        pl.BlockSpec((None, None, S, LANE), lambda b, h, i: (b, h, 0, 0)),
    ]
    args = [q, k, v]
    if has_ctx:
        P = kc.shape[2]
        in_specs += [
            pl.BlockSpec((None, None, P, 2 * LANE), lambda b, h, i: (b, h, 0, 0)),
            pl.BlockSpec((None, None, P, LANE), lambda b, h, i: (b, h, 0, 0)),
        ]
        args += [kc, vc]
    nsub = max(1, tq // ATTN_CHAIN_ROWS)
    return pl.pallas_call(
        functools.partial(_attn_kernel, has_ctx=has_ctx, nsub=nsub),
        grid=(B, H, T // tq),
        in_specs=in_specs,
        out_specs=pl.BlockSpec((None, tq, LANE), lambda b, h, i: (b, i, h)),
        out_shape=jax.ShapeDtypeStruct((B, T, H * LANE), BF16),
        compiler_params=_params(("arbitrary", "arbitrary", "arbitrary")),
        name="attention",
    )(*args)


def _shift_rows(x, k, row):
    n = x.shape[0]
    rolled = pltpu.roll(x, k % n, 0)
    valid = (row >= k) if k > 0 else (row < n + k)
    return jnp.where(valid, rolled, 0.0)


def _scan_chunk(a, b, carry, sub, reverse):
    n = a.shape[0]
    ng = n // SUBLANE
    hs = [None] * ng
    for j in (range(ng - 1, -1, -1) if reverse else range(ng)):
        aj = a[j * SUBLANE:(j + 1) * SUBLANE]
        bj = b[j * SUBLANE:(j + 1) * SUBLANE]
        d = 1
        while d < SUBLANE:
            shift = SUBLANE - d if reverse else d
            valid = (sub < SUBLANE - d) if reverse else (sub >= d)
            a_s = pltpu.roll(aj, shift, 0)
            b_s = pltpu.roll(bj, shift, 0)
            bj = jnp.where(valid, aj * b_s, 0.0) + bj
            aj = jnp.where(valid, aj * a_s, aj)
            d *= 2
        h = aj * carry + bj
        hs[j] = h
        carry = h[0:1] if reverse else h[SUBLANE - 1:SUBLANE]
    return jnp.concatenate(hs, axis=0), carry


def _sigmoid(x):
    return 0.5 * jnp.tanh(0.5 * x) + 0.5


def _softplus(x):
    return jnp.maximum(x, 0.0) + jnp.log(1.0 + jnp.exp(-jnp.abs(x)))


def _gelu_tanh(x):
    return 0.5 * x * (1.0 + jnp.tanh(0.7978845608028654 * (x + 0.044715 * x * x * x)))


def _lru_kernel(u_ref, ug_ref, cw_ref, cb_ref, w_ref, b_ref, lam_ref, h0_ref, y_ref, st_ref, xc_scr, hf_scr, *, tc):
    T, C = xc_scr.shape
    nch = T // tc
    row_t = lax.broadcasted_iota(jnp.int32, (T, C), 0)
    u = u_ref[...]
    cw = cw_ref[...]
    xc = (cb_ref[...] + cw[0:1] * _shift_rows(u, 2, row_t) + cw[1:2] * _shift_rows(u, 1, row_t)
          + cw[2:3] * u + cw[3:4] * _shift_rows(u, -1, row_t))
    xc_scr[...] = xc

    sub = lax.broadcasted_iota(jnp.int32, (SUBLANE, C), 0)
    sp = _softplus(-lam_ref[...])
    h0 = h0_ref[...]

    def gates(xcj, d):
        g = jnp.dot(xcj.astype(BF16), w_ref[:, 2 * d * C:2 * (d + 1) * C], preferred_element_type=F32)
        g = g + b_ref[:, 2 * d * C:2 * (d + 1) * C]
        r = _sigmoid(g[:, :C])
        i = _sigmoid(g[:, C:])
        log_a = -LRU_C * r * sp[d:d + 1]
        a = jnp.exp(log_a)
        v = jnp.tanh(-log_a) * (a * a + 1.0)
        b = jnp.where(v > 0.0, v * lax.rsqrt(v), 0.0) * (i * xcj)
        return a, b

    def fwd(j, carry):
        t0 = pl.multiple_of(j * tc, tc)
        xcj = xc_scr[pl.ds(t0, tc), :]
        a, b = gates(xcj, 0)
        h, carry = _scan_chunk(a, b, carry, sub, False)
        hf_scr[pl.ds(t0, tc), :] = h
        return carry

    hf_last = lax.fori_loop(0, nch, fwd, h0[0:1])
    st_ref[0:1, :] = hf_last

    def bwd(jj, carry):
        j = nch - 1 - jj
        t0 = pl.multiple_of(j * tc, tc)
        xcj = xc_scr[pl.ds(t0, tc), :]
        a, b = gates(xcj, 1)
        h, carry = _scan_chunk(a, b, carry, sub, True)
        y = (hf_scr[pl.ds(t0, tc), :] + h) * _gelu_tanh(ug_ref[pl.ds(t0, tc), :])
        y_ref[pl.ds(t0, tc), :] = y.astype(BF16)
        return carry

    hb_first = lax.fori_loop(0, nch, bwd, h0[1:2])
    st_ref[1:2, :] = hb_first


def _lru_mixer(z3, h0, l, W, tc):
    B, T, _ = z3.shape
    C = MXU_TILE
    nct = LRU_WIDTH // C
    return pl.pallas_call(
        functools.partial(_lru_kernel, tc=tc),
        grid=(B, nct),
        in_specs=[
            pl.BlockSpec((None, T, C), lambda b, c: (b, 0, OFF_LRU // C + c)),
            pl.BlockSpec((None, T, C), lambda b, c: (b, 0, OFF_GATE // C + c)),
            pl.BlockSpec((None, 4, C), lambda b, c: (l, 0, c)),
            pl.BlockSpec((None, 1, C), lambda b, c: (l, 0, c)),
            pl.BlockSpec((None, None, C, 4 * C), lambda b, c: (l, c, 0, 0)),
            pl.BlockSpec((None, None, 1, 4 * C), lambda b, c: (l, c, 0, 0)),
            pl.BlockSpec((None, 2, C), lambda b, c: (l, 0, c)),
            pl.BlockSpec((None, 2, C), lambda b, c: (b, 0, c)),
        ],
        out_specs=[
            pl.BlockSpec((None, T, C), lambda b, c: (b, 0, c)),
            pl.BlockSpec((None, 2, C), lambda b, c: (b, 0, c)),
        ],
        out_shape=[
            jax.ShapeDtypeStruct((B, T, LRU_WIDTH), BF16),
            jax.ShapeDtypeStruct((B, 2, LRU_WIDTH), F32),
        ],
        scratch_shapes=[pltpu.VMEM((T, C), F32), pltpu.VMEM((T, C), F32)],
        compiler_params=_params(("arbitrary", "arbitrary")),
        name="lru_mixer",
    )(z3, z3, W["conv_w"], W["conv_b"], W["lru_w"], W["lru_b"], W["lru_lambda"], h0)


def _pool_kernel(u_ref, w_ref, sc_ref, o_ref):
    T, C = u_ref.shape
    g = pl.program_id(1)
    for gi, win in enumerate(POOL_WINDOWS):
        @pl.when(g == gi)
        def _(win=win):
            row = lax.broadcasted_iota(jnp.int32, (T, C), 0)
            u = u_ref[...]
            s = u + _shift_rows(u, 1, row)
            w = 4
            while w <= win:
                q = w // 4
                s = _shift_rows(s, q, row) + _shift_rows(s, -q, row)
                w *= 2
            half = win // 2
            cnt = (jnp.minimum(row + half, T) - jnp.maximum(row - half, 0)).astype(F32)
            d = s / cnt - u
            y = jnp.dot(d.astype(BF16), w_ref[...], preferred_element_type=F32) * sc_ref[...]
            o_ref[...] = y.astype(BF16)


def _pool_mixer(z3, l, W):
    B, T, _ = z3.shape
    C = MXU_TILE
    ng = POOL_WIDTH // C
    return pl.pallas_call(
        _pool_kernel,
        grid=(B, ng),
        in_specs=[
            pl.BlockSpec((None, T, C), lambda b, g: (b, 0, OFF_POOL // C + g)),
            pl.BlockSpec((None, None, C, C), lambda b, g: (l, g, 0, 0)),
            pl.BlockSpec((None, 1, C), lambda b, g: (l, 0, g)),
        ],
        out_specs=pl.BlockSpec((None, T, C), lambda b, g: (b, 0, g)),
        out_shape=jax.ShapeDtypeStruct((B, T, POOL_WIDTH), BF16),
        compiler_params=_params(("arbitrary", "arbitrary")),
        name="pool_mixer",
    )(z3, W["pool_w"], W["pool_scale"])


def _merge_kernel(a_ref, r_ref, p_ref, g0_ref, g1_ref, g2_ref, wa_ref, wr_ref, wp_ref, o_ref):
    ya = jnp.dot(a_ref[...], wa_ref[...], preferred_element_type=F32)
    yr = jnp.dot(r_ref[...], wr_ref[...], preferred_element_type=F32)
    yp = jnp.dot(p_ref[...], wp_ref[...], preferred_element_type=F32)
    m = g0_ref[...].astype(F32) * ya + g1_ref[...].astype(F32) * yr + g2_ref[...].astype(F32) * yp
    o_ref[...] = m.astype(BF16)


def _merge(attn, lru, pool, gates, l, W, tm):
    M, K = attn.shape
    D = W["w_mla_o"].shape[-1]
    tn = _pick(D, (1024, 512, 256))
    nbr = D // tn
    act = pl.BlockSpec((tm, K), lambda i, j: (i, 0))
    wsp = pl.BlockSpec((None, K, tn), lambda i, j: (l, 0, j))
    gsp = lambda k: pl.BlockSpec((tm, tn), lambda i, j: (i, k * nbr + j))
    return pl.pallas_call(
        _merge_kernel,
        grid=(M // tm, D // tn),
        in_specs=[act, act, act, gsp(0), gsp(1), gsp(2), wsp, wsp, wsp],
        out_specs=pl.BlockSpec((tm, tn), lambda i, j: (i, j)),
        out_shape=jax.ShapeDtypeStruct((M, D), BF16),
        compiler_params=_params(("arbitrary", "arbitrary")),
        name="merge",
    )(attn, lru, pool, gates, gates, gates, W["w_mla_o"], W["w_lru_o"], W["w_pool_o"])


def _outproj_kernel(m_ref, w_ref, x_ref, g_ref, o_ref):
    y = jnp.dot(m_ref[...], w_ref[...], preferred_element_type=F32)
    o_ref[...] = x_ref[...] + g_ref[...] * y


def _out_proj(merged, x2d, mod4, row_of, l, W, tm):
    M, D = x2d.shape
    tn = _pick(D, (1024, 512, 256))
    nbr = D // tn
    return pl.pallas_call(
        _outproj_kernel,
        grid=(M // tm, D // tn),
        in_specs=[
            pl.BlockSpec((tm, D), lambda i, j: (i, 0)),
            pl.BlockSpec((None, D, tn), lambda i, j: (l, 0, j)),
            pl.BlockSpec((tm, tn), lambda i, j: (i, j)),
            pl.BlockSpec((None, None, 1, tn), lambda i, j: (l, row_of(i * tm), 0, 2 * nbr + j)),
        ],
        out_specs=pl.BlockSpec((tm, tn), lambda i, j: (i, j)),
        out_shape=jax.ShapeDtypeStruct((M, D), F32),
        compiler_params=_params(("arbitrary", "arbitrary")),
        name="out_proj",
    )(merged, W["w_out"], x2d, mod4)


def _ffn_kernel(x_ref, nw_ref, sh_ref, sc_ref, g_ref, w1_ref, w2_ref, o_ref, h_scr):
    f = pl.program_id(1)

    @pl.when(f == 0)
    def _():
        x = x_ref[...]
        ms = jnp.mean(x * x, axis=-1, keepdims=True)
        y = x * lax.rsqrt(ms + EPS) * nw_ref[...]
        h_scr[...] = (y * (1.0 + sc_ref[...]) + sh_ref[...]).astype(BF16)
        o_ref[...] = jnp.zeros_like(o_ref)

    a = jnp.dot(h_scr[...], w1_ref[...], preferred_element_type=F32)
    a = jnp.maximum(a, 0.0)
    o_ref[...] += jnp.dot((a * a).astype(BF16), w2_ref[...], preferred_element_type=F32)

    @pl.when(f == pl.num_programs(1) - 1)
    def _():
        o_ref[...] = x_ref[...] + g_ref[...] * o_ref[...]


def _ffn(x2d, mod4, row_of, l, W, tm):
    M, D = x2d.shape
    F = W["w_ff1"].shape[-1]
    tf = _pick(F, (1024, 512, 256))
    modv = lambda k: pl.BlockSpec((None, None, 1, D), lambda i, f: (l, row_of(i * tm), 0, k))
    return pl.pallas_call(
        _ffn_kernel,
        grid=(M // tm, F // tf),
        in_specs=[
            pl.BlockSpec((tm, D), lambda i, f: (i, 0)),
            pl.BlockSpec((None, 1, D), lambda i, f: (l, 0, 0)),
            modv(3), modv(4), modv(5),
            pl.BlockSpec((None, D, tf), lambda i, f: (l, 0, f)),
            pl.BlockSpec((None, tf, D), lambda i, f: (l, f, 0)),
        ],
        out_specs=pl.BlockSpec((tm, D), lambda i, f: (i, 0)),
        out_shape=jax.ShapeDtypeStruct((M, D), F32),
        scratch_shapes=[pltpu.VMEM((tm, D), BF16)],
        compiler_params=_params(("arbitrary", "arbitrary")),
        name="ffn",
    )(x2d, W["norm2"], mod4, mod4, mod4, W["w_ff1"], W["w_ff2"])


def _pad_lanes(x, n=LANE):
    return jnp.pad(x, [(0, 0)] * (x.ndim - 1) + [(0, n - x.shape[-1])])


def _rot_half(x):
    h = x.shape[-1] // 2
    return jnp.concatenate([-x[..., h:], x[..., :h]], axis=-1)


def _swap_half(x):
    h = x.shape[-1] // 2
    return jnp.concatenate([x[..., h:], x[..., :h]], axis=-1)


def _norm_gains(g):
    gr = g[:, QK_NOPE:]
    return (g[:, None, :QK_NOPE], _pad_lanes(gr)[:, None, :], _pad_lanes(_swap_half(gr))[:, None, :])


def _block_diag_tiles(w, per):
    *lead, nb, s, _ = w.shape
    w = w.reshape(*lead, nb // per, per, s, s)
    t = jnp.einsum("...kij,km->...kimj", w, jnp.eye(per, dtype=w.dtype))
    return t.reshape(*lead, nb // per, per * s, per * s)


def _pack_weights(w_in, q_a_norm, w_qb, kv_a_norm, w_kvb, q_norm, k_norm, w_mla_o, conv_w, conv_b,
                  lru_wa, lru_ba, lru_wx, lru_bx, lru_lambda, w_lru_o, pool_w, pool_scale, w_pool_o,
                  w_out, w_ff1, w_ff2, norm1_w, norm2_w):
    L, D, _ = w_in.shape
    o_kr = Q_LORA + KV_LORA
    w_in = w_in.astype(BF16)
    wkr = w_in[..., o_kr:o_kr + QK_ROPE]
    w_in_p = jnp.concatenate(
        [w_in[..., :o_kr], _pad_lanes(wkr), _pad_lanes(_rot_half(wkr)), w_in[..., o_kr + QK_ROPE:]], axis=-1)

    wq = w_qb.reshape(L, Q_LORA, N_HEADS, QK_DIM)
    wq_rope = wq[..., QK_NOPE:]
    w_q = jnp.concatenate([wq[..., :QK_NOPE].reshape(L, Q_LORA, -1),
                           _pad_lanes(wq_rope).reshape(L, Q_LORA, -1)], axis=-1)
    w_qrot = _pad_lanes(_rot_half(wq_rope)).reshape(L, Q_LORA, -1)

    wkv = w_kvb.reshape(L, KV_LORA, N_HEADS, QK_NOPE + V_DIM)
    w_kv = jnp.concatenate([wkv[..., :QK_NOPE].reshape(L, KV_LORA, -1),
                            wkv[..., QK_NOPE:].reshape(L, KV_LORA, -1)], axis=-1)

    qg = _norm_gains(q_norm)
    kg = _norm_gains(k_norm)

    per = MXU_TILE // lru_wa.shape[-1]
    ta = _block_diag_tiles(lru_wa, per)
    tx = _block_diag_tiles(lru_wx, per)
    lru_w = jnp.concatenate([ta[:, 0], tx[:, 0], ta[:, 1], tx[:, 1]], axis=-1)
    nct = lru_w.shape[1]
    ba = lru_ba.reshape(L, 2, nct, 1, MXU_TILE)
    bx = lru_bx.reshape(L, 2, nct, 1, MXU_TILE)
    lru_b = jnp.concatenate([ba[:, 0], bx[:, 0], ba[:, 1], bx[:, 1]], axis=-1)

    return dict(
        w_in=w_in_p.astype(BF16), norm1=norm1_w[:, None, :], norm2=norm2_w[:, None, :],
        q_a_norm=q_a_norm[:, None, :], kv_a_norm=kv_a_norm[:, None, :],
        w_q=w_q.astype(BF16), w_qrot=w_qrot.astype(BF16), w_kv=w_kv.astype(BF16),
        qg_n=qg[0], qg_r=qg[1], qg_s=qg[2], kg_n=kg[0], kg_r=kg[1], kg_s=kg[2],
        w_mla_o=w_mla_o.astype(BF16), w_lru_o=w_lru_o.astype(BF16), w_pool_o=w_pool_o.astype(BF16),
        conv_w=conv_w, conv_b=conv_b[:, None, :], lru_w=lru_w.astype(BF16), lru_b=lru_b,
        lru_lambda=lru_lambda, pool_w=pool_w.astype(BF16), pool_scale=pool_scale[:, None, :],
        w_out=w_out.astype(BF16), w_ff1=w_ff1.astype(BF16), w_ff2=w_ff2.astype(BF16),
    )


def _rope_tables(T):
    rows = T // GRID_W
    row = jnp.repeat(jnp.arange(rows), GRID_W).astype(F32)
    col = jnp.tile(jnp.arange(GRID_W), rows).astype(F32)
    n_freq = QK_ROPE // 4
    inv = ROPE_BASE ** (-(jnp.arange(n_freq, dtype=F32) / n_freq))
    ang = jnp.concatenate([row[:, None] * inv, col[:, None] * inv], axis=-1)
    cos, sin = jnp.cos(ang), jnp.sin(ang)
    return (_pad_lanes(jnp.concatenate([cos, cos], axis=-1)), _pad_lanes(jnp.concatenate([sin, sin], axis=-1)))


def _trunk_layer(x2d, B, T, mod4, row_of, shared_cond, l, W, rope, ctx, h0):
    M = B * T
    span = M if shared_cond else T
    tm = _pick(span, (1024, 512, 256, 128))
    tmp = _pick(T, (512, 256, 128))
    z, gates = _in_proj(x2d, mod4, row_of, W["norm1"], W["w_in"], l, tm)
    z3 = z.reshape(B, T, -1)

    q = _q_prep(z, B, T, l, W, rope, tmp)
    k, v, ckv = _k_prep(
        z, z, B, T, l, W, rope, tmp,
        kv_spec=pl.BlockSpec((tmp, KV_LORA), lambda i: (i, OFF_KV // KV_LORA)),
        kr_spec=pl.BlockSpec((tmp, LANE), lambda i: (i, OFF_KR // LANE)),
        krot_spec=pl.BlockSpec((tmp, LANE), lambda i: (i, OFF_KROT // LANE)),
        normalize=True)
    if ctx is None:
        kc = vc = None
    else:
        ckv_c, kr_c = ctx
        P = ckv_c.shape[2]
        tp = _pick(P, (256, 128))
        npb = P // tp
        kc, vc = _k_prep(
            ckv_c, kr_c, B, P, l, W, None, tp,
            kv_spec=pl.BlockSpec((None, None, tp, KV_LORA), lambda i: (i // npb, l, i % npb, 0)),
            kr_spec=pl.BlockSpec((None, None, tp, LANE), lambda i: (i // npb, l, i % npb, 0)),
            krot_spec=None, normalize=False)
    attn = _attention(q, k, v, kc, vc, _pick(T, (2048, 1024, 512, 256, 128)))

    lru, state = _lru_mixer(z3, h0, l, W, _pick(T, (256, 128)))
    pool = _pool_mixer(z3, l, W)

    merged = _merge(attn.reshape(M, -1), lru.reshape(M, -1), pool.reshape(M, -1), gates, l, W, tm)
    x2d = _out_proj(merged, x2d, mod4, row_of, l, W, tm)
    x2d = _ffn(x2d, mod4, row_of, l, W, _pick(span, (512, 256, 128)))
    return x2d, z, ckv, state


def kernel(x_prompt, x_sample, c, cache_ckv, cache_krope, state_lru, c_ctx, w_mod, b_mod, norm1_w, norm2_w, w_in, q_a_norm, w_qb, kv_a_norm, w_kvb, q_norm, k_norm, w_mla_o, conv_w, conv_b, lru_wa, lru_ba, lru_wx, lru_bx, lru_lambda, w_lru_o, pool_w, pool_scale, w_pool_o, w_out, w_ff1, w_ff2):
    L = w_in.shape[0]
    Bc, Tc, D = x_prompt.shape
    Bs, Ts, _ = x_sample.shape

    W = _pack_weights(w_in, q_a_norm, w_qb, kv_a_norm, w_kvb, q_norm, k_norm, w_mla_o, conv_w, conv_b,
                      lru_wa, lru_ba, lru_wx, lru_bx, lru_lambda, w_lru_o, pool_w, pool_scale, w_pool_o,
                      w_out, w_ff1, w_ff2, norm1_w, norm2_w)

    rows = Bs + 1
    rpad = -rows % 8
    cond = jnp.concatenate([c, c_ctx[None, :], jnp.zeros((rpad, D), F32)], axis=0)
    mod = _modulation(cond, w_mod, b_mod)
    mod4 = mod.reshape(L, rows + rpad, 1, N_MOD * D)

    rope = _rope_tables(Ts)
    kr_cache = _pad_lanes(cache_krope)
    zero_state = jnp.zeros((Bc, 2, LRU_WIDTH), F32)

    row_smp = lambda t: t // Ts
    row_ctx = lambda t: Bs

    xp = x_prompt.reshape(Bc * Tc, D)
    xs = x_sample.reshape(Bs * Ts, D)
    ckv_list, kr_list, st_list = [], [], []
    for l in range(L):
        xp, zc, ckv, st = _trunk_layer(xp, Bc, Tc, mod4, row_ctx, True, l, W, None, None, zero_state)
        ckv_list.append(ckv.reshape(Bc, Tc, KV_LORA))
        kr_list.append(zc[:, OFF_KR:OFF_KR + QK_ROPE].reshape(Bc, Tc, QK_ROPE))
        st_list.append(st)
        xs, _, _, _ = _trunk_layer(xs, Bs, Ts, mod4, row_smp, False, l, W, rope,
                                   (cache_ckv, kr_cache), state_lru[:, l])
    return (xp.reshape(Bc, Tc, D), xs.reshape(Bs, Ts, D), jnp.stack(ckv_list, axis=1),
            jnp.stack(kr_list, axis=1), jnp.stack(st_list, axis=1))
```

```python
import functools

import jax
import jax.numpy as jnp
from jax import lax
from jax.experimental import pallas as pl
from jax.experimental.pallas import tpu as pltpu

F32 = jnp.float32
BF16 = jnp.bfloat16

EPS = 1e-6
GRID_W = 64
N_HEADS = 8
QK_NOPE = 128
QK_ROPE = 64
V_DIM = 128
QK_DIM = QK_NOPE + QK_ROPE
Q_LORA = 512
KV_LORA = 256
ROPE_BASE = 10000.0
LRU_WIDTH = 1024
LRU_C = 8.0
POOL_WIDTH = 1024
POOL_WINDOWS = (2, 4, 8, 16)
N_MOD = 6

LANE = 128
SUBLANE = 8
MXU_TILE = 256
VMEM_LIMIT = 56 * 2**20
ATTN_CHAIN_ROWS = 256
ATTN_LOOKAHEAD = 2
Q_PRESCALE =QK_DIM ** -0.5 * 1.4426950408889634

OFF_Q = 0
OFF_KV = Q_LORA
OFF_KR = OFF_KV + KV_LORA
OFF_KROT = OFF_KR + LANE
OFF_LRU = OFF_KROT + LANE
OFF_GATE = OFF_LRU + LRU_WIDTH
OFF_POOL = OFF_GATE + LRU_WIDTH
OFF_BR = OFF_POOL + POOL_WIDTH


def _params(sem):
    return pltpu.CompilerParams(dimension_semantics=sem, vmem_limit_bytes=VMEM_LIMIT)


def _pick(n, prefs):
    for p in prefs:
        if n % p == 0:
            return p
    return n


def _mod_kernel(c_ref, w_ref, b_ref, o_ref):
    c = c_ref[...]
    s = c * jax.nn.sigmoid(c)
    o_ref[...] = jnp.dot(s.astype(BF16), w_ref[...].astype(BF16), preferred_element_type=F32) + b_ref[...]


def _modulation(cond, w_mod, b_mod):
    L, D, N = w_mod.shape
    R = cond.shape[0]
    tn = _pick(N, (1024, 512, 256, 128))
    return pl.pallas_call(
        _mod_kernel,
        grid=(L, N // tn),
        in_specs=[
            pl.BlockSpec((R, D), lambda l, j: (0, 0)),
            pl.BlockSpec((None, D, tn), lambda l, j: (l, 0, j)),
            pl.BlockSpec((None, 1, tn), lambda l, j: (l, 0, j)),
        ],
        out_specs=pl.BlockSpec((None, R, tn), lambda l, j: (l, 0, j)),
        out_shape=jax.ShapeDtypeStruct((L, R, N), F32),
        compiler_params=_params(("arbitrary", "arbitrary")),
        name="modulation",
    )(cond, w_mod, b_mod.reshape(L, 1, N))


def _inproj_kernel(x_ref, nw_ref, sh_ref, sc_ref, w_ref, z_ref, g_ref, h_scr, *, n_lin):
    j = pl.program_id(1)

    @pl.when(j == 0)
    def _():
        h_scr[...] = _norm_mod(x_ref[...], nw_ref[...], sc_ref[...], sh_ref[...])

    @pl.when(j < n_lin)
    def _():
        z_ref[...] = jnp.dot(h_scr[...], w_ref[...], preferred_element_type=F32)

    @pl.when(j >= n_lin)
    def _():
        g_ref[...] = _sigmoid(jnp.dot(h_scr[...], w_ref[...], preferred_element_type=F32)).astype(BF16)


def _in_proj(x2d, mod4, row_of, norm_w, w_in_p, l, tm):
    M, D = x2d.shape
    N = w_in_p.shape[-1]
    tn = _pick(OFF_BR, (1024, 512, 256))
    n_lin = OFF_BR // tn
    return pl.pallas_call(
        functools.partial(_inproj_kernel, n_lin=n_lin),
        grid=(M // tm, N // tn),
        in_specs=[
            pl.BlockSpec((tm, D), lambda i, j: (i, 0)),
            pl.BlockSpec((None, 1, D), lambda i, j: (l, 0, 0)),
            pl.BlockSpec((None, None, 1, D), lambda i, j: (l, row_of(i * tm), 0, 0)),
            pl.BlockSpec((None, None, 1, D), lambda i, j: (l, row_of(i * tm), 0, 1)),
            pl.BlockSpec((None, D, tn), lambda i, j: (l, 0, j)),
        ],
        out_specs=[
            pl.BlockSpec((tm, tn), lambda i, j: (i, jnp.minimum(j, n_lin - 1))),
            pl.BlockSpec((tm, tn), lambda i, j: (i, jnp.maximum(j - n_lin, 0))),
        ],
        out_shape=[
            jax.ShapeDtypeStruct((M, OFF_BR), F32),
            jax.ShapeDtypeStruct((M, N - OFF_BR), BF16),
        ],
        scratch_shapes=[pltpu.VMEM((tm, D), BF16)],
        compiler_params=_params(("arbitrary", "arbitrary")),
        name="in_proj",
    )(x2d, norm_w, mod4, mod4, w_in_p)


def _head_scale(nope, rope_sq_sum):
    ss = jnp.sum(nope * nope, axis=-1, keepdims=True) + rope_sq_sum
    return lax.rsqrt(ss * (1.0 / QK_DIM) + EPS)


def _qprep_kernel(*refs, use_rope):
    if use_rope:
        ql_ref, an_ref, w_ref, wrot_ref, gn_ref, gr_ref, gs_ref, cos_ref, sin_ref, q_ref = refs
    else:
        ql_ref, an_ref, w_ref, gn_ref, gr_ref, q_ref = refs
    ql = ql_ref[...]
    ms = jnp.mean(ql * ql, axis=-1, keepdims=True)
    qn = (ql * lax.rsqrt(ms + EPS) * an_ref[...]).astype(BF16)
    qq = jnp.dot(qn, w_ref[...], preferred_element_type=F32)
    if use_rope:
        qrot = jnp.dot(qn, wrot_ref[...], preferred_element_type=F32)
        cos = cos_ref[...]
        sin = sin_ref[...]
        gs = gs_ref[...]
    gn = gn_ref[...]
    gr = gr_ref[...]
    hw = N_HEADS * LANE
    for h in range(N_HEADS):
        nope = qq[:, h * LANE:(h + 1) * LANE]
        rope = qq[:, hw + h * LANE:hw + (h + 1) * LANE]
        s = _head_scale(nope, jnp.sum(rope * rope, axis=-1, keepdims=True))
        if use_rope:
            r = gr * rope * cos + gs * qrot[:, h * LANE:(h + 1) * LANE] * sin
        else:
            r = gr * rope
        s = s * Q_PRESCALE
        q_ref[h, :, 0:LANE] = (nope * s * gn).astype(BF16)
        q_ref[h, :, LANE:2 * LANE] = (r * s).astype(BF16)


def _q_prep(z, B, T, l, W, rope, tm):
    M = z.shape[0]
    nb = T // tm
    use_rope = rope is not None
    wspec = lambda n: pl.BlockSpec((None, Q_LORA, n), lambda i: (l, 0, 0))
    vec = pl.BlockSpec((None, 1, LANE), lambda i: (l, 0, 0))
    in_specs = [
        pl.BlockSpec((tm, Q_LORA), lambda i: (i, OFF_Q // Q_LORA)),
        pl.BlockSpec((None, 1, Q_LORA), lambda i: (l, 0, 0)),
        wspec(2 * N_HEADS * LANE),
    ]
    args = [z, W["q_a_norm"], W["w_q"]]
    if use_rope:
        tab = pl.BlockSpec((tm, LANE), lambda i: (i % nb, 0))
        in_specs += [wspec(N_HEADS * LANE), vec, vec, vec, tab, tab]
        args += [W["w_qrot"], W["qg_n"], W["qg_r"], W["qg_s"], rope[0], rope[1]]
    else:
        in_specs += [vec, vec]
        args += [W["qg_n"], W["qg_r"]]
    return pl.pallas_call(
        functools.partial(_qprep_kernel, use_rope=use_rope),
        grid=(M // tm,),
        in_specs=in_specs,
        out_specs=pl.BlockSpec((None, N_HEADS, tm, 2 * LANE), lambda i: (i // nb, 0, i % nb, 0)),
        out_shape=jax.ShapeDtypeStruct((B, N_HEADS, T, 2 * LANE), BF16),
        compiler_params=_params(("arbitrary",)),
        name="q_prep",
    )(*args)


def _kprep_kernel(*refs, use_rope, normalize):
    refs = list(refs)
    kv_ref = refs.pop(0)
    kr_ref = refs.pop(0)
    krot_ref = refs.pop(0) if use_rope else None
    an_ref = refs.pop(0) if normalize else None
    w_ref = refs.pop(0)
    gn_ref = refs.pop(0)
    gr_ref = refs.pop(0)
    if use_rope:
        gs_ref, cos_ref, sin_ref = refs.pop(0), refs.pop(0), refs.pop(0)
    k_ref = refs.pop(0)
    v_ref = refs.pop(0)
    ckv_ref = refs.pop(0) if normalize else None

    kv = kv_ref[...]
    if normalize:
        ms = jnp.mean(kv * kv, axis=-1, keepdims=True)
        kv = kv * lax.rsqrt(ms + EPS) * an_ref[...]
        ckv_ref[...] = kv
    kk = jnp.dot(kv.astype(BF16), w_ref[...], preferred_element_type=F32)
    kr = kr_ref[...]
    ssr = jnp.sum(kr * kr, axis=-1, keepdims=True)
    if use_rope:
        base = gr_ref[...] * kr * cos_ref[...] + gs_ref[...] * krot_ref[...] * sin_ref[...]
    else:
        base = gr_ref[...] * kr
    gn = gn_ref[...]
    hw = N_HEADS * LANE
    for h in range(N_HEADS):
        nope = kk[:, h * LANE:(h + 1) * LANE]
        s = _head_scale(nope, ssr)
        k_ref[h, :, 0:LANE] = (nope * s * gn).astype(BF16)
        k_ref[h, :, LANE:2 * LANE] = (base * s).astype(BF16)
        v_ref[h] = kk[:, hw + h * LANE:hw + (h + 1) * LANE].astype(BF16)


def _k_prep(kv_src, kr_src, B, T, l, W, rope, tm, *, kv_spec, kr_spec, krot_spec, normalize):
    M = B * T
    nb = T // tm
    use_rope = rope is not None
    vec = pl.BlockSpec((None, 1, LANE), lambda i: (l, 0, 0))
    in_specs = [kv_spec, kr_spec]
    args = [kv_src, kr_src]
    if use_rope:
        in_specs.append(krot_spec)
        args.append(kr_src)
    if normalize:
        in_specs.append(pl.BlockSpec((None, 1, KV_LORA), lambda i: (l, 0, 0)))
        args.append(W["kv_a_norm"])
    in_specs += [pl.BlockSpec((None, KV_LORA, 2 * N_HEADS * LANE), lambda i: (l, 0, 0)), vec, vec]
    args += [W["w_kv"], W["kg_n"], W["kg_r"]]
    if use_rope:
        tab = pl.BlockSpec((tm, LANE), lambda i: (i % nb, 0))
        in_specs += [vec, tab, tab]
        args += [W["kg_s"], rope[0], rope[1]]
    out_specs = [
        pl.BlockSpec((None, N_HEADS, tm, 2 * LANE), lambda i: (i // nb, 0, i % nb, 0)),
        pl.BlockSpec((None, N_HEADS, tm, LANE), lambda i: (i // nb, 0, i % nb, 0)),
    ]
    out_shape = [
        jax.ShapeDtypeStruct((B, N_HEADS, T, 2 * LANE), BF16),
        jax.ShapeDtypeStruct((B, N_HEADS, T, LANE), BF16),
    ]
    if normalize:
        out_specs.append(pl.BlockSpec((tm, KV_LORA), lambda i: (i, 0)))
        out_shape.append(jax.ShapeDtypeStruct((M, KV_LORA), F32))
    return pl.pallas_call(
        functools.partial(_kprep_kernel, use_rope=use_rope, normalize=normalize),
        grid=(M // tm,),
        in_specs=in_specs,
        out_specs=out_specs,
        out_shape=out_shape,
        compiler_params=_params(("arbitrary",)),
        name="k_prep",
    )(*args)


def _attn_kernel(*refs, has_ctx, nsub):
    if has_ctx:
        q_ref, k_ref, v_ref, kc_ref, vc_ref, o_ref = refs
    else:
        q_ref, k_ref, v_ref, o_ref = refs
    nt = (((1,), (1,)), ((), ()))
    ts = q_ref.shape[0] // nsub
    tiles = []
    for kref, vref in ((k_ref, v_ref), (kc_ref, vc_ref)) if has_ctx else ((k_ref, v_ref),):
        kt = min(MXU_TILE, kref.shape[0])
        tiles += [(kref, vref, t * kt, kt) for t in range(kref.shape[0] // kt)]

    def fold_lanes(x, op):
        out = x[:, :LANE]
        for c in range(1, x.shape[1] // LANE):
            out = op(out, x[:, c * LANE:(c + 1) * LANE])
        return out

    def scores(r):
        q = q_ref[r * ts:(r + 1) * ts, :]
        s_tiles = []
        mrun = None
        for kref, _, off, kt in tiles:
            st = lax.dot_general(q, kref[off:off + kt, :], nt, preferred_element_type=F32)
            s_tiles.append(st)
            tmax = fold_lanes(st, jnp.maximum)
            mrun = tmax if mrun is None else jnp.maximum(mrun, tmax)
        return s_tiles, jnp.max(mrun, axis=-1, keepdims=True)

    def weights_and_values(r, s_tiles, m):
        lrun = None
        o = None
        for st, (_, vref, off, kt) in zip(s_tiles, tiles):
            p = jnp.exp2(st - m)
            psum = fold_lanes(p, jnp.add)
            lrun = psum if lrun is None else lrun + psum
            ot = jnp.dot(p.astype(BF16), vref[off:off + kt, :], preferred_element_type=F32)
            o = ot if o is None else o + ot
        den = jnp.sum(lrun, axis=-1, keepdims=True)
        o_ref[r * ts:(r + 1) * ts, :] = (o / den).astype(BF16)

    ahead = min(ATTN_LOOKAHEAD, nsub)
    pending = [scores(r) for r in range(ahead)]
    for r in range(nsub):
        s_tiles, m = pending.pop(0)
        if r + ahead < nsub:
            pending.append(scores(r + ahead))
        weights_and_values(r, s_tiles, m)


def _attention(q, k, v, kc, vc, tq):
    B, H, T, _ = q.shape
    S = k.shape[2]
    has_ctx = kc is not None
    in_specs = [
        pl.BlockSpec((None, None, tq, 2 * LANE), lambda b, h, i: (b, h, i, 0)),
        pl.BlockSpec((None, None, S, 2 * LANE), lambda b, h, i: (b, h, 0, 0)),
        pl.BlockSpec((None, None, S, LANE), lambda b, h, i: (b, h, 0, 0)),
    ]
    args = [q, k, v]
    if has_ctx:
        P = kc.shape[2]
        in_specs += [
            pl.BlockSpec((None, None, P, 2 * LANE), lambda b, h, i: (b, h, 0, 0)),
            pl.BlockSpec((None, None, P, LANE), lambda b, h, i: (b, h, 0, 0)),
        ]
        args += [kc, vc]
    nsub = max(1, tq // ATTN_CHAIN_ROWS)
    return pl.pallas_call(
        functools.partial(_attn_kernel, has_ctx=has_ctx, nsub=nsub),
        grid=(B, H, T // tq),
        in_specs=in_specs,
        out_specs=pl.BlockSpec((None, tq, LANE), lambda b, h, i: (b, i, h)),
        out_shape=jax.ShapeDtypeStruct((B, T, H * LANE), BF16),
        compiler_params=_params(("arbitrary", "arbitrary", "arbitrary")),
        name="attention",
    )(*args)


def _shift_rows(x, k, row):
    n = x.shape[0]
    rolled = pltpu.roll(x, k % n, 0)
    valid = (row >= k) if k > 0 else (row < n + k)
    return jnp.where(valid, rolled, 0.0)


def _scan_chunk(a, b, carry, sub, reverse):
    n = a.shape[0]
    ng = n // SUBLANE
    hs = [None] * ng
    for j in (range(ng - 1, -1, -1) if reverse else range(ng)):
        aj = a[j * SUBLANE:(j + 1) * SUBLANE]
        bj = b[j * SUBLANE:(j + 1) * SUBLANE]
        d = 1
        while d < SUBLANE:
            shift = SUBLANE - d if reverse else d
            valid = (sub < SUBLANE - d) if reverse else (sub >= d)
            a_s = pltpu.roll(aj, shift, 0)
            b_s = pltpu.roll(bj, shift, 0)
            bj = jnp.where(valid, aj * b_s, 0.0) + bj
            aj = jnp.where(valid, aj * a_s, aj)
            d *= 2
        h = aj * carry + bj
        hs[j] = h
        carry = h[0:1] if reverse else h[SUBLANE - 1:SUBLANE]
    return jnp.concatenate(hs, axis=0), carry


def _sigmoid(x):
    return 0.5 * jnp.tanh(0.5 * x) + 0.5


def _softplus(x):
    return jnp.maximum(x, 0.0) + jnp.log(1.0 + jnp.exp(-jnp.abs(x)))


def _gelu_tanh(x):
    return 0.5 * x * (1.0 + jnp.tanh(0.7978845608028654 * (x + 0.044715 * x * x * x)))


def _lru_kernel(u_ref, ug_ref, cw_ref, cb_ref, w_ref, b_ref, lam_ref, h0_ref, y_ref, st_ref,
                xc_scr, hf_scr, hb_scr, *, tc):
    T, C = xc_scr.shape
    nch = T // tc
    row_t = lax.broadcasted_iota(jnp.int32, (T, C), 0)
    u = u_ref[...]
    cw = cw_ref[...]
    xc = (cb_ref[...] + cw[0:1] * _shift_rows(u, 2, row_t) + cw[1:2] * _shift_rows(u, 1, row_t)
          + cw[2:3] * u + cw[3:4] * _shift_rows(u, -1, row_t))
    xc_scr[...] = xc

    sub = lax.broadcasted_iota(jnp.int32, (SUBLANE, C), 0)
    sp = _softplus(-lam_ref[...])
    h0 = h0_ref[...]

    def gates(xcj, d):
        g = jnp.dot(xcj.astype(BF16), w_ref[:, 2 * d * C:2 * (d + 1) * C], preferred_element_type=F32)
        g = g + b_ref[:, 2 * d * C:2 * (d + 1) * C]
        r = _sigmoid(g[:, :C])
        i = _sigmoid(g[:, C:])
        log_a = -LRU_C * r * sp[d:d + 1]
        a = jnp.exp(log_a)
        v = jnp.tanh(-log_a) * (a * a + 1.0)
        b = jnp.where(v > 0.0, v * lax.rsqrt(v), 0.0) * (i * xcj)
        return a, b

    def scan_dir(t0, carry, d):
        a, b = gates(xc_scr[pl.ds(t0, tc), :], d)
        return _scan_chunk(a, b, carry, sub, d == 1)

    def emit(t0, hf, hb):
        y = (hf + hb) * _gelu_tanh(ug_ref[pl.ds(t0, tc), :])
        y_ref[pl.ds(t0, tc), :] = y.astype(BF16)

    def meet(jj, carry):
        tf0 = pl.multiple_of(jj * tc, tc)
        tb0 = pl.multiple_of((nch - 1 - jj) * tc, tc)
        hf, cf = scan_dir(tf0, carry[0], 0)
        hb, cb = scan_dir(tb0, carry[1], 1)
        hf_scr[pl.ds(tf0, tc), :] = hf
        hb_scr[pl.ds(tb0, tc), :] = hb
        return cf, cb

    def part(jj, carry):
        tf0 = pl.multiple_of(jj * tc, tc)
        tb0 = pl.multiple_of((nch - 1 - jj) * tc, tc)
        hf, cf = scan_dir(tf0, carry[0], 0)
        hb, cb = scan_dir(tb0, carry[1], 1)
        emit(tf0, hf, hb_scr[pl.ds(tf0, tc), :])
        emit(tb0, hf_scr[pl.ds(tb0, tc), :], hb)
        return cf, cb

    carry = (h0[0:1], h0[1:2])
    if nch % 2 == 0:
        carry = lax.fori_loop(0, nch // 2, meet, carry)
        carry = lax.fori_loop(nch // 2, nch, part, carry)
    else:
        assert nch == 1
        hf, cf = scan_dir(0, carry[0], 0)
        hb, cb = scan_dir(0, carry[1], 1)
        emit(0, hf, hb)
        carry = (cf, cb)
    st_ref[0:1, :] = carry[0]
    st_ref[1:2, :] = carry[1]


def _lru_mixer(z3, h0, l, W, tc):
    B, T, _ = z3.shape
    C = MXU_TILE
    nct = LRU_WIDTH // C
    return pl.pallas_call(
        functools.partial(_lru_kernel, tc=tc),
        grid=(B, nct),
        in_specs=[
            pl.BlockSpec((None, T, C), lambda b, c: (b, 0, OFF_LRU // C + c)),
            pl.BlockSpec((None, T, C), lambda b, c: (b, 0, OFF_GATE // C + c)),
            pl.BlockSpec((None, 4, C), lambda b, c: (l, 0, c)),
            pl.BlockSpec((None, 1, C), lambda b, c: (l, 0, c)),
            pl.BlockSpec((None, None, C, 4 * C), lambda b, c: (l, c, 0, 0)),
            pl.BlockSpec((None, None, 1, 4 * C), lambda b, c: (l, c, 0, 0)),
            pl.BlockSpec((None, 2, C), lambda b, c: (l, 0, c)),
            pl.BlockSpec((None, 2, C), lambda b, c: (b, 0, c)),
        ],
        out_specs=[
            pl.BlockSpec((None, T, C), lambda b, c: (b, 0, c)),
            pl.BlockSpec((None, 2, C), lambda b, c: (b, 0, c)),
        ],
        out_shape=[
            jax.ShapeDtypeStruct((B, T, LRU_WIDTH), BF16),
            jax.ShapeDtypeStruct((B, 2, LRU_WIDTH), F32),
        ],
        scratch_shapes=[pltpu.VMEM((T, C), F32)] * 3,
        compiler_params=_params(("arbitrary", "arbitrary")),
        name="lru_mixer",
    )(z3, z3, W["conv_w"], W["conv_b"], W["lru_w"], W["lru_b"], W["lru_lambda"], h0)


def _pool_kernel(u_ref, w_ref, sc_ref, o_ref):
    T, C = u_ref.shape
    g = pl.program_id(1)
    for gi, win in enumerate(POOL_WINDOWS):
        @pl.when(g == gi)
        def _(win=win):
            row = lax.broadcasted_iota(jnp.int32, (T, C), 0)
            u = u_ref[...]
            s = u + _shift_rows(u, 1, row)
            w = 4
            while w <= win:
                q = w // 4
                s = _shift_rows(s, q, row) + _shift_rows(s, -q, row)
                w *= 2
            half = win // 2
            cnt = (jnp.minimum(row + half, T) - jnp.maximum(row - half, 0)).astype(F32)
            d = s / cnt - u
            y = jnp.dot(d.astype(BF16), w_ref[...], preferred_element_type=F32) * sc_ref[...]
            o_ref[...] = y.astype(BF16)


def _pool_mixer(z3, l, W):
    B, T, _ = z3.shape
    C = MXU_TILE
    ng = POOL_WIDTH // C
    return pl.pallas_call(
        _pool_kernel,
        grid=(B, ng),
        in_specs=[
            pl.BlockSpec((None, T, C), lambda b, g: (b, 0, OFF_POOL // C + g)),
            pl.BlockSpec((None, None, C, C), lambda b, g: (l, g, 0, 0)),
            pl.BlockSpec((None, 1, C), lambda b, g: (l, 0, g)),
        ],
        out_specs=pl.BlockSpec((None, T, C), lambda b, g: (b, 0, g)),
        out_shape=jax.ShapeDtypeStruct((B, T, POOL_WIDTH), BF16),
        compiler_params=_params(("arbitrary", "arbitrary")),
        name="pool_mixer",
    )(z3, W["pool_w"], W["pool_scale"])


def _merge_kernel(a_ref, r_ref, p_ref, g0_ref, g1_ref, g2_ref, wa_ref, wr_ref, wp_ref, o_ref):
    ya = jnp.dot(a_ref[...], wa_ref[...], preferred_element_type=F32)
    yr = jnp.dot(r_ref[...], wr_ref[...], preferred_element_type=F32)
    yp = jnp.dot(p_ref[...], wp_ref[...], preferred_element_type=F32)
    m = g0_ref[...].astype(F32) * ya + g1_ref[...].astype(F32) * yr + g2_ref[...].astype(F32) * yp
    o_ref[...] = m.astype(BF16)


def _merge(attn, lru, pool, gates, l, W, tm):
    M, K = attn.shape
    D = W["w_mla_o"].shape[-1]
    tn = _pick(D, (1024, 512, 256))
    nbr = D // tn
    act = pl.BlockSpec((tm, K), lambda i, j: (i, 0))
    wsp = pl.BlockSpec((None, K, tn), lambda i, j: (l, 0, j))
    gsp = lambda k: pl.BlockSpec((tm, tn), lambda i, j: (i, k * nbr + j))
    return pl.pallas_call(
        _merge_kernel,
        grid=(M // tm, D // tn),
        in_specs=[act, act, act, gsp(0), gsp(1), gsp(2), wsp, wsp, wsp],
        out_specs=pl.BlockSpec((tm, tn), lambda i, j: (i, j)),
        out_shape=jax.ShapeDtypeStruct((M, D), BF16),
        compiler_params=_params(("arbitrary", "arbitrary")),
        name="merge",
    )(attn, lru, pool, gates, gates, gates, W["w_mla_o"], W["w_lru_o"], W["w_pool_o"])


def _outproj_kernel(m_ref, w_ref, x_ref, g_ref, o_ref):
    y = jnp.dot(m_ref[...], w_ref[...], preferred_element_type=F32)
    o_ref[...] = x_ref[...] + g_ref[...] * y


def _out_proj(merged, x2d, mod4, row_of, l, W, tm):
    M, D = x2d.shape
    tn = _pick(D, (1024, 512, 256))
    nbr = D // tn
    return pl.pallas_call(
        _outproj_kernel,
        grid=(M // tm, D // tn),
        in_specs=[
            pl.BlockSpec((tm, D), lambda i, j: (i, 0)),
            pl.BlockSpec((None, D, tn), lambda i, j: (l, 0, j)),
            pl.BlockSpec((tm, tn), lambda i, j: (i, j)),
            pl.BlockSpec((None, None, 1, tn), lambda i, j: (l, row_of(i * tm), 0, 2 * nbr + j)),
        ],
        out_specs=pl.BlockSpec((tm, tn), lambda i, j: (i, j)),
        out_shape=jax.ShapeDtypeStruct((M, D), F32),
        compiler_params=_params(("arbitrary", "arbitrary")),
        name="out_proj",
    )(merged, W["w_out"], x2d, mod4)


def _norm_mod(x, nw, sc, sh):
    ms = jnp.mean(x * x, axis=-1, keepdims=True)
    y = x * lax.rsqrt(ms + EPS) * nw
    return (y * (1.0 + sc) + sh).astype(BF16)


def _ffn_kernel(x_ref, xn_ref, nw_ref, sh_ref, sc_ref, shn_ref, scn_ref, g_ref, w1_ref, w2_ref, o_ref, h_scr):
    i = pl.program_id(0)
    f = pl.program_id(1)
    last = pl.num_programs(1) - 1

    @pl.when((i == 0) & (f == 0))
    def _():
        h_scr[...] = _norm_mod(x_ref[...], nw_ref[...], sc_ref[...], sh_ref[...])

    @pl.when(f == 0)
    def _():
        o_ref[...] = jnp.zeros_like(o_ref)

    def partial_sum():
        a = jnp.dot(h_scr[...], w1_ref[...], preferred_element_type=F32)
        a = jnp.maximum(a, 0.0)
        return jnp.dot((a * a).astype(BF16), w2_ref[...], preferred_element_type=F32)

    @pl.when(f < last)
    def _():
        o_ref[...] += partial_sum()

    @pl.when(f == last)
    def _():
        h_next = _norm_mod(xn_ref[...], nw_ref[...], scn_ref[...], shn_ref[...])
        acc = o_ref[...] + partial_sum()
        o_ref[...] = x_ref[...] + g_ref[...] * acc
        h_scr[...] = h_next


def _ffn(x2d, mod4, row_of, l, W, tm):
    M, D = x2d.shape
    F = W["w_ff1"].shape[-1]
    tf = _pick(F, (1024, 512, 256))
    assert F // tf >= 2
    cur = lambda i: i
    nxt = lambda i: jnp.minimum(i + 1, M // tm - 1)
    modv = lambda k, blk: pl.BlockSpec((None, None, 1, D), lambda i, f: (l, row_of(blk(i) * tm), 0, k))
    return pl.pallas_call(
        _ffn_kernel,
        grid=(M // tm, F // tf),
        in_specs=[
            pl.BlockSpec((tm, D), lambda i, f: (i, 0)),
            pl.BlockSpec((tm, D), lambda i, f: (nxt(i), 0)),
            pl.BlockSpec((None, 1, D), lambda i, f: (l, 0, 0)),
            modv(3, cur), modv(4, cur), modv(3, nxt), modv(4, nxt), modv(5, cur),
            pl.BlockSpec((None, D, tf), lambda i, f: (l, 0, f)),
            pl.BlockSpec((None, tf, D), lambda i, f: (l, f, 0)),
        ],
        out_specs=pl.BlockSpec((tm, D), lambda i, f: (i, 0)),
        out_shape=jax.ShapeDtypeStruct((M, D), F32),
        scratch_shapes=[pltpu.VMEM((tm, D), BF16)],
        compiler_params=_params(("arbitrary", "arbitrary")),
        name="ffn",
    )(x2d, x2d, W["norm2"], mod4, mod4, mod4, mod4, mod4, W["w_ff1"], W["w_ff2"])


def _pad_lanes(x, n=LANE):
    return jnp.pad(x, [(0, 0)] * (x.ndim - 1) + [(0, n - x.shape[-1])])


def _rot_half(x):
    h = x.shape[-1] // 2
    return jnp.concatenate([-x[..., h:], x[..., :h]], axis=-1)


def _swap_half(x):
    h = x.shape[-1] // 2
    return jnp.concatenate([x[..., h:], x[..., :h]], axis=-1)


def _norm_gains(g):
    gr = g[:, QK_NOPE:]
    return (g[:, None, :QK_NOPE], _pad_lanes(gr)[:, None, :], _pad_lanes(_swap_half(gr))[:, None, :])


def _block_diag_tiles(w, per):
    *lead, nb, s, _ = w.shape
    w = w.reshape(*lead, nb // per, per, s, s)
    t = jnp.einsum("...kij,km->...kimj", w, jnp.eye(per, dtype=w.dtype))
    return t.reshape(*lead, nb // per, per * s, per * s)


def _pack_weights(w_in, q_a_norm, w_qb, kv_a_norm, w_kvb, q_norm, k_norm, w_mla_o, conv_w, conv_b,
                  lru_wa, lru_ba, lru_wx, lru_bx, lru_lambda, w_lru_o, pool_w, pool_scale, w_pool_o,
                  w_out, w_ff1, w_ff2, norm1_w, norm2_w):
    L, D, _ = w_in.shape
    o_kr = Q_LORA + KV_LORA
    w_in = w_in.astype(BF16)
    wkr = w_in[..., o_kr:o_kr + QK_ROPE]
    w_in_p = jnp.concatenate(
        [w_in[..., :o_kr], _pad_lanes(wkr), _pad_lanes(_rot_half(wkr)), w_in[..., o_kr + QK_ROPE:]], axis=-1)

    wq = w_qb.reshape(L, Q_LORA, N_HEADS, QK_DIM)
    wq_rope = wq[..., QK_NOPE:]
    w_q = jnp.concatenate([wq[..., :QK_NOPE].reshape(L, Q_LORA, -1),
                           _pad_lanes(wq_rope).reshape(L, Q_LORA, -1)], axis=-1)
    w_qrot = _pad_lanes(_rot_half(wq_rope)).reshape(L, Q_LORA, -1)

    wkv = w_kvb.reshape(L, KV_LORA, N_HEADS, QK_NOPE + V_DIM)
    w_kv = jnp.concatenate([wkv[..., :QK_NOPE].reshape(L, KV_LORA, -1),
                            wkv[..., QK_NOPE:].reshape(L, KV_LORA, -1)], axis=-1)

    qg = _norm_gains(q_norm)
    kg = _norm_gains(k_norm)

    per = MXU_TILE // lru_wa.shape[-1]
    ta = _block_diag_tiles(lru_wa, per)
    tx = _block_diag_tiles(lru_wx, per)
    lru_w = jnp.concatenate([ta[:, 0], tx[:, 0], ta[:, 1], tx[:, 1]], axis=-1)
    nct = lru_w.shape[1]
    ba = lru_ba.reshape(L, 2, nct, 1, MXU_TILE)
    bx = lru_bx.reshape(L, 2, nct, 1, MXU_TILE)
    lru_b = jnp.concatenate([ba[:, 0], bx[:, 0], ba[:, 1], bx[:, 1]], axis=-1)

    return dict(
        w_in=w_in_p.astype(BF16), norm1=norm1_w[:, None, :], norm2=norm2_w[:, None, :],
        q_a_norm=q_a_norm[:, None, :], kv_a_norm=kv_a_norm[:, None, :],
        w_q=w_q.astype(BF16), w_qrot=w_qrot.astype(BF16), w_kv=w_kv.astype(BF16),
        qg_n=qg[0], qg_r=qg[1], qg_s=qg[2], kg_n=kg[0], kg_r=kg[1], kg_s=kg[2],
        w_mla_o=w_mla_o.astype(BF16), w_lru_o=w_lru_o.astype(BF16), w_pool_o=w_pool_o.astype(BF16),
        conv_w=conv_w, conv_b=conv_b[:, None, :], lru_w=lru_w.astype(BF16), lru_b=lru_b,
        lru_lambda=lru_lambda, pool_w=pool_w.astype(BF16), pool_scale=pool_scale[:, None, :],
        w_out=w_out.astype(BF16), w_ff1=w_ff1.astype(BF16), w_ff2=w_ff2.astype(BF16),
    )


def _rope_tables(T):
    rows = T // GRID_W
    row = jnp.repeat(jnp.arange(rows), GRID_W).astype(F32)
    col = jnp.tile(jnp.arange(GRID_W), rows).astype(F32)
    n_freq = QK_ROPE // 4
    inv = ROPE_BASE ** (-(jnp.arange(n_freq, dtype=F32) / n_freq))
    ang = jnp.concatenate([row[:, None] * inv, col[:, None] * inv], axis=-1)
    cos, sin = jnp.cos(ang), jnp.sin(ang)
    return (_pad_lanes(jnp.concatenate([cos, cos], axis=-1)), _pad_lanes(jnp.concatenate([sin, sin], axis=-1)))


def _trunk_layer(x2d, B, T, mod4, row_of, shared_cond, l, W, rope, ctx, h0):
    M = B * T
    span = M if shared_cond else T
    tm = _pick(span, (1024, 512, 256, 128))
    tmp = _pick(T, (512, 256, 128))
    z, gates = _in_proj(x2d, mod4, row_of, W["norm1"], W["w_in"], l, tm)
    z3 = z.reshape(B, T, -1)

    q = _q_prep(z, B, T, l, W, rope, tmp)
    k, v, ckv = _k_prep(
        z, z, B, T, l, W, rope, tmp,
        kv_spec=pl.BlockSpec((tmp, KV_LORA), lambda i: (i, OFF_KV // KV_LORA)),
        kr_spec=pl.BlockSpec((tmp, LANE), lambda i: (i, OFF_KR // LANE)),
        krot_spec=pl.BlockSpec((tmp, LANE), lambda i: (i, OFF_KROT // LANE)),
        normalize=True)
    if ctx is None:
        kc = vc = None
    else:
        ckv_c, kr_c = ctx
        P = ckv_c.shape[2]
        tp = _pick(P, (256, 128))
        npb = P // tp
        kc, vc = _k_prep(
            ckv_c, kr_c, B, P, l, W, None, tp,
            kv_spec=pl.BlockSpec((None, None, tp, KV_LORA), lambda i: (i // npb, l, i % npb, 0)),
            kr_spec=pl.BlockSpec((None, None, tp, LANE), lambda i: (i // npb, l, i % npb, 0)),
            krot_spec=None, normalize=False)
    attn = _attention(q, k, v, kc, vc, _pick(T, (2048, 1024, 512, 256, 128)))

    lru, state = _lru_mixer(z3, h0, l, W, _pick(T, (256, 128)))
    pool = _pool_mixer(z3, l, W)

    merged = _merge(attn.reshape(M, -1), lru.reshape(M, -1), pool.reshape(M, -1), gates, l, W, tm)
    x2d = _out_proj(merged, x2d, mod4, row_of, l, W, tm)
    x2d = _ffn(x2d, mod4, row_of, l, W, _pick(span, (512, 256, 128)))
    return x2d, z, ckv, state


def kernel(x_prompt, x_sample, c, cache_ckv, cache_krope, state_lru, c_ctx, w_mod, b_mod, norm1_w, norm2_w, w_in, q_a_norm, w_qb, kv_a_norm, w_kvb, q_norm, k_norm, w_mla_o, conv_w, conv_b, lru_wa, lru_ba, lru_wx, lru_bx, lru_lambda, w_lru_o, pool_w, pool_scale, w_pool_o, w_out, w_ff1, w_ff2):
    L = w_in.shape[0]
    Bc, Tc, D = x_prompt.shape
    Bs, Ts, _ = x_sample.shape

    W = _pack_weights(w_in, q_a_norm, w_qb, kv_a_norm, w_kvb, q_norm, k_norm, w_mla_o, conv_w, conv_b,
                      lru_wa, lru_ba, lru_wx, lru_bx, lru_lambda, w_lru_o, pool_w, pool_scale, w_pool_o,
                      w_out, w_ff1, w_ff2, norm1_w, norm2_w)

    rows = Bs + 1
    rpad = -rows % 8
    cond = jnp.concatenate([c, c_ctx[None, :], jnp.zeros((rpad, D), F32)], axis=0)
    mod = _modulation(cond, w_mod, b_mod)
    mod4 = mod.reshape(L, rows + rpad, 1, N_MOD * D)

    rope = _rope_tables(Ts)
    kr_cache = _pad_lanes(cache_krope)
    zero_state = jnp.zeros((Bc, 2, LRU_WIDTH), F32)

    row_smp = lambda t: t // Ts
    row_ctx = lambda t: Bs

    xp = x_prompt.reshape(Bc * Tc, D)
    xs = x_sample.reshape(Bs * Ts, D)
    ckv_list, kr_list, st_list = [], [], []
    for l in range(L):
        xp, zc, ckv, st = _trunk_layer(xp, Bc, Tc, mod4, row_ctx, True, l, W, None, None, zero_state)
        ckv_list.append(ckv.reshape(Bc, Tc, KV_LORA))
        kr_list.append(zc[:, OFF_KR:OFF_KR + QK_ROPE].reshape(Bc, Tc, QK_ROPE))
        st_list.append(st)
        xs, _, _, _ = _trunk_layer(xs, Bs, Ts, mod4, row_smp, False, l, W, rope,
                                   (cache_ckv, kr_cache), state_lru[:, l])
    return (xp.reshape(Bc, Tc, D), xs.reshape(Bs, Ts, D), jnp.stack(ckv_list, axis=1),
            jnp.stack(kr_list, axis=1), jnp.stack(st_list, axis=1))
```

```python
import functools

import jax
import jax.numpy as jnp
from jax import lax
from jax.experimental import pallas as pl
from jax.experimental.pallas import tpu as pltpu

F32 = jnp.float32
BF16 = jnp.bfloat16

EPS = 1e-6
GRID_W = 64
N_HEADS = 8
QK_NOPE = 128
QK_ROPE = 64
V_DIM = 128
QK_DIM = QK_NOPE + QK_ROPE
Q_LORA = 512
KV_LORA = 256
ROPE_BASE = 10000.0
LRU_WIDTH = 1024
LRU_C = 8.0
POOL_WIDTH = 1024
POOL_WINDOWS = (2, 4, 8, 16)
N_MOD = 6

LANE = 128
SUBLANE = 8
MXU_TILE = 256
VMEM_LIMIT = 56 * 2**20
ATTN_STEP_ROWS = 2048
ATTN_CHAIN_ROWS = 256
ATTN_LOOKAHEAD = 2
Q_PRESCALE =QK_DIM ** -0.5 * 1.4426950408889634

OFF_Q = 0
OFF_KV = Q_LORA
OFF_KR = OFF_KV + KV_LORA
OFF_KROT = OFF_KR + LANE
OFF_LRU = OFF_KROT + LANE
OFF_GATE = OFF_LRU + LRU_WIDTH
OFF_POOL = OFF_GATE + LRU_WIDTH
OFF_BR = OFF_POOL + POOL_WIDTH


def _params(sem):
    return pltpu.CompilerParams(dimension_semantics=sem, vmem_limit_bytes=VMEM_LIMIT)


def _pick(n, prefs):
    for p in prefs:
        if n % p == 0:
            return p
    return n


def _mod_kernel(c_ref, w_ref, b_ref, o_ref):
    c = c_ref[...]
    s = c * jax.nn.sigmoid(c)
    o_ref[...] = jnp.dot(s.astype(BF16), w_ref[...].astype(BF16), preferred_element_type=F32) + b_ref[...]


def _modulation(cond, w_mod, b_mod):
    L, D, N = w_mod.shape
    R = cond.shape[0]
    tn = _pick(N, (1024, 512, 256, 128))
    return pl.pallas_call(
        _mod_kernel,
        grid=(L, N // tn),
        in_specs=[
            pl.BlockSpec((R, D), lambda l, j: (0, 0)),
            pl.BlockSpec((None, D, tn), lambda l, j: (l, 0, j)),
            pl.BlockSpec((None, 1, tn), lambda l, j: (l, 0, j)),
        ],
        out_specs=pl.BlockSpec((None, R, tn), lambda l, j: (l, 0, j)),
        out_shape=jax.ShapeDtypeStruct((L, R, N), F32),
        compiler_params=_params(("arbitrary", "arbitrary")),
        name="modulation",
    )(cond, w_mod, b_mod.reshape(L, 1, N))


def _inproj_kernel(x_ref, nw_ref, sh_ref, sc_ref, w_ref, z_ref, g_ref, h_scr, *, n_lin):
    j = pl.program_id(1)

    @pl.when(j == 0)
    def _():
        h_scr[...] = _norm_mod(x_ref[...], nw_ref[...], sc_ref[...], sh_ref[...])

    @pl.when(j < n_lin)
    def _():
        z_ref[...] = jnp.dot(h_scr[...], w_ref[...], preferred_element_type=F32)

    @pl.when(j >= n_lin)
    def _():
        g_ref[...] = _sigmoid(jnp.dot(h_scr[...], w_ref[...], preferred_element_type=F32)).astype(BF16)


def _in_proj(x2d, mod4, row_of, norm_w, w_in_p, l, tm):
    M, D = x2d.shape
    N = w_in_p.shape[-1]
    tn = _pick(OFF_BR, (1024, 512, 256))
    n_lin = OFF_BR // tn
    return pl.pallas_call(
        functools.partial(_inproj_kernel, n_lin=n_lin),
        grid=(M // tm, N // tn),
        in_specs=[
            pl.BlockSpec((tm, D), lambda i, j: (i, 0)),
            pl.BlockSpec((None, 1, D), lambda i, j: (l, 0, 0)),
            pl.BlockSpec((None, None, 1, D), lambda i, j: (l, row_of(i * tm), 0, 0)),
            pl.BlockSpec((None, None, 1, D), lambda i, j: (l, row_of(i * tm), 0, 1)),
            pl.BlockSpec((None, D, tn), lambda i, j: (l, 0, j)),
        ],
        out_specs=[
            pl.BlockSpec((tm, tn), lambda i, j: (i, jnp.minimum(j, n_lin - 1))),
            pl.BlockSpec((tm, tn), lambda i, j: (i, jnp.maximum(j - n_lin, 0))),
        ],
        out_shape=[
            jax.ShapeDtypeStruct((M, OFF_BR), F32),
            jax.ShapeDtypeStruct((M, N - OFF_BR), BF16),
        ],
        scratch_shapes=[pltpu.VMEM((tm, D), BF16)],
        compiler_params=_params(("arbitrary", "arbitrary")),
        name="in_proj",
    )(x2d, norm_w, mod4, mod4, w_in_p)


def _head_scale(nope, rope_sq_sum):
    ss = jnp.sum(nope * nope, axis=-1, keepdims=True) + rope_sq_sum
    return lax.rsqrt(ss * (1.0 / QK_DIM) + EPS)


def _qprep_kernel(*refs, use_rope):
    if use_rope:
        ql_ref, an_ref, w_ref, wrot_ref, gn_ref, gr_ref, gs_ref, cos_ref, sin_ref, q_ref = refs
    else:
        ql_ref, an_ref, w_ref, gn_ref, gr_ref, q_ref = refs
    ql = ql_ref[...]
    ms = jnp.mean(ql * ql, axis=-1, keepdims=True)
    qn = (ql * lax.rsqrt(ms + EPS) * an_ref[...]).astype(BF16)
    qq = jnp.dot(qn, w_ref[...], preferred_element_type=F32)
    if use_rope:
        qrot = jnp.dot(qn, wrot_ref[...], preferred_element_type=F32)
        cos = cos_ref[...]
        sin = sin_ref[...]
        gs = gs_ref[...]
    gn = gn_ref[...]
    gr = gr_ref[...]
    hw = N_HEADS * LANE
    for h in range(N_HEADS):
        nope = qq[:, h * LANE:(h + 1) * LANE]
        rope = qq[:, hw + h * LANE:hw + (h + 1) * LANE]
        s = _head_scale(nope, jnp.sum(rope * rope, axis=-1, keepdims=True))
        if use_rope:
            r = gr * rope * cos + gs * qrot[:, h * LANE:(h + 1) * LANE] * sin
        else:
            r = gr * rope
        s = s * Q_PRESCALE
        q_ref[h, :, 0:LANE] = (nope * s * gn).astype(BF16)
        q_ref[h, :, LANE:2 * LANE] = (r * s).astype(BF16)


def _q_prep_io(z, B, T, l, W, rope, tm):
    nb = T // tm
    use_rope = rope is not None
    wspec = lambda n: pl.BlockSpec((None, Q_LORA, n), lambda i: (l, 0, 0))
    vec = pl.BlockSpec((None, 1, LANE), lambda i: (l, 0, 0))
    in_specs = [
        pl.BlockSpec((tm, Q_LORA), lambda i: (i, OFF_Q // Q_LORA)),
        pl.BlockSpec((None, 1, Q_LORA), lambda i: (l, 0, 0)),
        wspec(2 * N_HEADS * LANE),
    ]
    args = [z, W["q_a_norm"], W["w_q"]]
    if use_rope:
        tab = pl.BlockSpec((tm, LANE), lambda i: (i % nb, 0))
        in_specs += [wspec(N_HEADS * LANE), vec, vec, vec, tab, tab]
        args += [W["w_qrot"], W["qg_n"], W["qg_r"], W["qg_s"], rope[0], rope[1]]
    else:
        in_specs += [vec, vec]
        args += [W["qg_n"], W["qg_r"]]
    out_specs = [pl.BlockSpec((None, N_HEADS, tm, 2 * LANE), lambda i: (i // nb, 0, i % nb, 0))]
    out_shape = [jax.ShapeDtypeStruct((B, N_HEADS, T, 2 * LANE), BF16)]
    return in_specs, args, out_specs, out_shape


def _kprep_kernel(*refs, use_rope, normalize):
    refs = list(refs)
    kv_ref = refs.pop(0)
    kr_ref = refs.pop(0)
    krot_ref = refs.pop(0) if use_rope else None
    an_ref = refs.pop(0) if normalize else None
    w_ref = refs.pop(0)
    gn_ref = refs.pop(0)
    gr_ref = refs.pop(0)
    if use_rope:
        gs_ref, cos_ref, sin_ref = refs.pop(0), refs.pop(0), refs.pop(0)
    k_ref = refs.pop(0)
    v_ref = refs.pop(0)
    ckv_ref = refs.pop(0) if normalize else None

    kv = kv_ref[...]
    if normalize:
        ms = jnp.mean(kv * kv, axis=-1, keepdims=True)
        kv = kv * lax.rsqrt(ms + EPS) * an_ref[...]
        ckv_ref[...] = kv
    kk = jnp.dot(kv.astype(BF16), w_ref[...], preferred_element_type=F32)
    kr = kr_ref[...]
    ssr = jnp.sum(kr * kr, axis=-1, keepdims=True)
    if use_rope:
        base = gr_ref[...] * kr * cos_ref[...] + gs_ref[...] * krot_ref[...] * sin_ref[...]
    else:
        base = gr_ref[...] * kr
    gn = gn_ref[...]
    hw = N_HEADS * LANE
    for h in range(N_HEADS):
        nope = kk[:, h * LANE:(h + 1) * LANE]
        s = _head_scale(nope, ssr)
        k_ref[h, :, 0:LANE] = (nope * s * gn).astype(BF16)
        k_ref[h, :, LANE:2 * LANE] = (base * s).astype(BF16)
        v_ref[h] = kk[:, hw + h * LANE:hw + (h + 1) * LANE].astype(BF16)


def _k_prep_io(kv_src, kr_src, B, T, l, W, rope, tm, *, kv_spec, kr_spec, krot_spec, normalize):
    M = B * T
    nb = T // tm
    use_rope = rope is not None
    vec = pl.BlockSpec((None, 1, LANE), lambda i: (l, 0, 0))
    in_specs = [kv_spec, kr_spec]
    args = [kv_src, kr_src]
    if use_rope:
        in_specs.append(krot_spec)
        args.append(kr_src)
    if normalize:
        in_specs.append(pl.BlockSpec((None, 1, KV_LORA), lambda i: (l, 0, 0)))
        args.append(W["kv_a_norm"])
    in_specs += [pl.BlockSpec((None, KV_LORA, 2 * N_HEADS * LANE), lambda i: (l, 0, 0)), vec, vec]
    args += [W["w_kv"], W["kg_n"], W["kg_r"]]
    if use_rope:
        tab = pl.BlockSpec((tm, LANE), lambda i: (i % nb, 0))
        in_specs += [vec, tab, tab]
        args += [W["kg_s"], rope[0], rope[1]]
    out_specs = [
        pl.BlockSpec((None, N_HEADS, tm, 2 * LANE), lambda i: (i // nb, 0, i % nb, 0)),
        pl.BlockSpec((None, N_HEADS, tm, LANE), lambda i: (i // nb, 0, i % nb, 0)),
    ]
    out_shape = [
        jax.ShapeDtypeStruct((B, N_HEADS, T, 2 * LANE), BF16),
        jax.ShapeDtypeStruct((B, N_HEADS, T, LANE), BF16),
    ]
    if normalize:
        out_specs.append(pl.BlockSpec((tm, KV_LORA), lambda i: (i, 0)))
        out_shape.append(jax.ShapeDtypeStruct((M, KV_LORA), F32))
    return in_specs, args, out_specs, out_shape


def _cache_k_prep(ckv_c, kr_c, B, P, l, W, tp):
    npb = P // tp
    in_specs, args, out_specs, out_shape = _k_prep_io(
        ckv_c, kr_c, B, P, l, W, None, tp,
        kv_spec=pl.BlockSpec((None, None, tp, KV_LORA), lambda i: (i // npb, l, i % npb, 0)),
        kr_spec=pl.BlockSpec((None, None, tp, LANE), lambda i: (i // npb, l, i % npb, 0)),
        krot_spec=None, normalize=False)
    return pl.pallas_call(
        functools.partial(_kprep_kernel, use_rope=False, normalize=False),
        grid=(B * P // tp,),
        in_specs=in_specs,
        out_specs=out_specs,
        out_shape=out_shape,
        compiler_params=_params(("arbitrary",)),
        name="k_prep",
    )(*args)


def _qkprep_kernel(*refs, use_rope, n_q_in, n_k_in):
    q_in, k_in = refs[:n_q_in], refs[n_q_in:n_q_in + n_k_in]
    q_out, k_out = refs[n_q_in + n_k_in], refs[n_q_in + n_k_in + 1:]
    _qprep_kernel(*q_in, q_out, use_rope=use_rope)
    _kprep_kernel(*k_in, *k_out, use_rope=use_rope, normalize=True)


def _qk_prep(z, B, T, l, W, rope, tm):
    q_io = _q_prep_io(z, B, T, l, W, rope, tm)
    k_io = _k_prep_io(
        z, z, B, T, l, W, rope, tm,
        kv_spec=pl.BlockSpec((tm, KV_LORA), lambda i: (i, OFF_KV // KV_LORA)),
        kr_spec=pl.BlockSpec((tm, LANE), lambda i: (i, OFF_KR // LANE)),
        krot_spec=pl.BlockSpec((tm, LANE), lambda i: (i, OFF_KROT // LANE)),
        normalize=True)
    return pl.pallas_call(
        functools.partial(_qkprep_kernel, use_rope=rope is not None, n_q_in=len(q_io[1]), n_k_in=len(k_io[1])),
        grid=(B * T // tm,),
        in_specs=q_io[0] + k_io[0],
        out_specs=q_io[2] + k_io[2],
        out_shape=q_io[3] + k_io[3],
        compiler_params=_params(("arbitrary",)),
        name="qk_prep",
    )(*q_io[1], *k_io[1])


def _attn_kernel(*refs, has_ctx, nsub):
    if has_ctx:
        q_ref, k_ref, v_ref, kc_ref, vc_ref, o_ref = refs
    else:
        q_ref, k_ref, v_ref, o_ref = refs
    nt = (((1,), (1,)), ((), ()))
    hb, tq, _ = q_ref.shape
    ts = tq // nsub

    def key_tiles(h):
        out = []
        for kref, vref in ((k_ref, v_ref), (kc_ref, vc_ref)) if has_ctx else ((k_ref, v_ref),):
            kt = min(MXU_TILE, kref.shape[1])
            out += [(kref.at[h], vref.at[h], t * kt, kt) for t in range(kref.shape[1] // kt)]
        return out

    def fold_lanes(x, op):
        out = x[:, :LANE]
        for c in range(1, x.shape[1] // LANE):
            out = op(out, x[:, c * LANE:(c + 1) * LANE])
        return out

    def scores(h, r):
        q = q_ref[h, r * ts:(r + 1) * ts, :]
        s_tiles = []
        mrun = None
        for kref, _, off, kt in key_tiles(h):
            st = lax.dot_general(q, kref[off:off + kt, :], nt, preferred_element_type=F32)
            s_tiles.append(st)
            tmax = fold_lanes(st, jnp.maximum)
            mrun = tmax if mrun is None else jnp.maximum(mrun, tmax)
        return s_tiles, jnp.max(mrun, axis=-1, keepdims=True)

    def weights_and_values(h, r, s_tiles, m):
        lrun = None
        o = None
        for st, (_, vref, off, kt) in zip(s_tiles, key_tiles(h)):
            p = jnp.exp2(st - m)
            psum = fold_lanes(p, jnp.add)
            lrun = psum if lrun is None else lrun + psum
            ot = jnp.dot(p.astype(BF16), vref[off:off + kt, :], preferred_element_type=F32)
            o = ot if o is None else o + ot
        den = jnp.sum(lrun, axis=-1, keepdims=True)
        o_ref[r * ts:(r + 1) * ts, h * LANE:(h + 1) * LANE] = (o / den).astype(BF16)

    chains = [(h, r) for h in range(hb) for r in range(nsub)]
    ahead = min(ATTN_LOOKAHEAD, len(chains))
    pending = [scores(*c) for c in chains[:ahead]]
    for n, c in enumerate(chains):
        s_tiles, m = pending.pop(0)
        if n + ahead < len(chains):
            pending.append(scores(*chains[n + ahead]))
        weights_and_values(*c, s_tiles, m)


def _attention(q, k, v, kc, vc, tq):
    B, H, T, _ = q.shape
    S = k.shape[2]
    has_ctx = kc is not None
    hb = max(1, min(H, ATTN_STEP_ROWS // tq))
    in_specs = [
        pl.BlockSpec((None, hb, tq, 2 * LANE), lambda b, h, i: (b, h, i, 0)),
        pl.BlockSpec((None, hb, S, 2 * LANE), lambda b, h, i: (b, h, 0, 0)),
        pl.BlockSpec((None, hb, S, LANE), lambda b, h, i: (b, h, 0, 0)),
    ]
    args = [q, k, v]
    if has_ctx:
        P = kc.shape[2]
        in_specs += [
            pl.BlockSpec((None, hb, P, 2 * LANE), lambda b, h, i: (b, h, 0, 0)),
            pl.BlockSpec((None, hb, P, LANE), lambda b, h, i: (b, h, 0, 0)),
        ]
        args += [kc, vc]
    nsub = max(1, tq // ATTN_CHAIN_ROWS)
    return pl.pallas_call(
        functools.partial(_attn_kernel, has_ctx=has_ctx, nsub=nsub),
        grid=(B, H // hb, T // tq),
        in_specs=in_specs,
        out_specs=pl.BlockSpec((None, tq, hb * LANE), lambda b, h, i: (b, i, h)),
        out_shape=jax.ShapeDtypeStruct((B, T, H * LANE), BF16),
        compiler_params=_params(("arbitrary", "arbitrary", "arbitrary")),
        name="attention",
    )(*args)


def _shift_rows(x, k, row):
    n = x.shape[0]
    rolled = pltpu.roll(x, k % n, 0)
    valid = (row >= k) if k > 0 else (row < n + k)
    return jnp.where(valid, rolled, 0.0)


def _scan_chunk(a, b, carry, sub, reverse):
    n = a.shape[0]
    ng = n // SUBLANE
    hs = [None] * ng
    for j in (range(ng - 1, -1, -1) if reverse else range(ng)):
        aj = a[j * SUBLANE:(j + 1) * SUBLANE]
        bj = b[j * SUBLANE:(j + 1) * SUBLANE]
        d = 1
        while d < SUBLANE:
            shift = SUBLANE - d if reverse else d
            valid = (sub < SUBLANE - d) if reverse else (sub >= d)
            a_s = pltpu.roll(aj, shift, 0)
            b_s = pltpu.roll(bj, shift, 0)
            bj = jnp.where(valid, aj * b_s, 0.0) + bj
            aj = jnp.where(valid, aj * a_s, aj)
            d *= 2
        h = aj * carry + bj
        hs[j] = h
        carry = h[0:1] if reverse else h[SUBLANE - 1:SUBLANE]
    return jnp.concatenate(hs, axis=0), carry


def _sigmoid(x):
    return 0.5 * jnp.tanh(0.5 * x) + 0.5


def _softplus(x):
    return jnp.maximum(x, 0.0) + jnp.log(1.0 + jnp.exp(-jnp.abs(x)))


def _gelu_tanh(x):
    return 0.5 * x * (1.0 + jnp.tanh(0.7978845608028654 * (x + 0.044715 * x * x * x)))


def _lru_kernel(u_ref, ug_ref, cw_ref, cb_ref, w_ref, b_ref, lam_ref, h0_ref, y_ref, st_ref,
                xc_scr, hf_scr, hb_scr, *, tc):
    T, C = xc_scr.shape
    nch = T // tc
    row_t = lax.broadcasted_iota(jnp.int32, (T, C), 0)
    u = u_ref[...]
    cw = cw_ref[...]
    xc = (cb_ref[...] + cw[0:1] * _shift_rows(u, 2, row_t) + cw[1:2] * _shift_rows(u, 1, row_t)
          + cw[2:3] * u + cw[3:4] * _shift_rows(u, -1, row_t))
    xc_scr[...] = xc

    sub = lax.broadcasted_iota(jnp.int32, (SUBLANE, C), 0)
    sp = _softplus(-lam_ref[...])
    h0 = h0_ref[...]

    def gates(xcj, d):
        g = jnp.dot(xcj.astype(BF16), w_ref[:, 2 * d * C:2 * (d + 1) * C], preferred_element_type=F32)
        g = g + b_ref[:, 2 * d * C:2 * (d + 1) * C]
        r = _sigmoid(g[:, :C])
        i = _sigmoid(g[:, C:])
        log_a = -LRU_C * r * sp[d:d + 1]
        a = jnp.exp(log_a)
        v = jnp.tanh(-log_a) * (a * a + 1.0)
        b = jnp.where(v > 0.0, v * lax.rsqrt(v), 0.0) * (i * xcj)
        return a, b

    def scan_dir(t0, carry, d):
        a, b = gates(xc_scr[pl.ds(t0, tc), :], d)
        return _scan_chunk(a, b, carry, sub, d == 1)

    def emit(t0, hf, hb):
        y = (hf + hb) * _gelu_tanh(ug_ref[pl.ds(t0, tc), :])
        y_ref[pl.ds(t0, tc), :] = y.astype(BF16)

    def meet(jj, carry):
        tf0 = pl.multiple_of(jj * tc, tc)
        tb0 = pl.multiple_of((nch - 1 - jj) * tc, tc)
        hf, cf = scan_dir(tf0, carry[0], 0)
        hb, cb = scan_dir(tb0, carry[1], 1)
        hf_scr[pl.ds(tf0, tc), :] = hf
        hb_scr[pl.ds(tb0, tc), :] = hb
        return cf, cb

    def part(jj, carry):
        tf0 = pl.multiple_of(jj * tc, tc)
        tb0 = pl.multiple_of((nch - 1 - jj) * tc, tc)
        hf, cf = scan_dir(tf0, carry[0], 0)
        hb, cb = scan_dir(tb0, carry[1], 1)
        emit(tf0, hf, hb_scr[pl.ds(tf0, tc), :])
        emit(tb0, hf_scr[pl.ds(tb0, tc), :], hb)
        return cf, cb

    carry = (h0[0:1], h0[1:2])
    if nch % 2 == 0:
        carry = lax.fori_loop(0, nch // 2, meet, carry)
        carry = lax.fori_loop(nch // 2, nch, part, carry)
    else:
        assert nch == 1
        hf, cf = scan_dir(0, carry[0], 0)
        hb, cb = scan_dir(0, carry[1], 1)
        emit(0, hf, hb)
        carry = (cf, cb)
    st_ref[0:1, :] = carry[0]
    st_ref[1:2, :] = carry[1]


def _lru_mixer(z3, h0, l, W, tc):
    B, T, _ = z3.shape
    C = MXU_TILE
    nct = LRU_WIDTH // C
    return pl.pallas_call(
        functools.partial(_lru_kernel, tc=tc),
        grid=(B, nct),
        in_specs=[
            pl.BlockSpec((None, T, C), lambda b, c: (b, 0, OFF_LRU // C + c)),
            pl.BlockSpec((None, T, C), lambda b, c: (b, 0, OFF_GATE // C + c)),
            pl.BlockSpec((None, 4, C), lambda b, c: (l, 0, c)),
            pl.BlockSpec((None, 1, C), lambda b, c: (l, 0, c)),
            pl.BlockSpec((None, None, C, 4 * C), lambda b, c: (l, c, 0, 0)),
            pl.BlockSpec((None, None, 1, 4 * C), lambda b, c: (l, c, 0, 0)),
            pl.BlockSpec((None, 2, C), lambda b, c: (l, 0, c)),
            pl.BlockSpec((None, 2, C), lambda b, c: (b, 0, c)),
        ],
        out_specs=[
            pl.BlockSpec((None, T, C), lambda b, c: (b, 0, c)),
            pl.BlockSpec((None, 2, C), lambda b, c: (b, 0, c)),
        ],
        out_shape=[
            jax.ShapeDtypeStruct((B, T, LRU_WIDTH), BF16),
            jax.ShapeDtypeStruct((B, 2, LRU_WIDTH), F32),
        ],
        scratch_shapes=[pltpu.VMEM((T, C), F32)] * 3,
        compiler_params=_params(("arbitrary", "arbitrary")),
        name="lru_mixer",
    )(z3, z3, W["conv_w"], W["conv_b"], W["lru_w"], W["lru_b"], W["lru_lambda"], h0)


def _pool_kernel(u_ref, w_ref, sc_ref, o_ref):
    T, C = u_ref.shape
    g = pl.program_id(1)
    for gi, win in enumerate(POOL_WINDOWS):
        @pl.when(g == gi)
        def _(win=win):
            row = lax.broadcasted_iota(jnp.int32, (T, C), 0)
            u = u_ref[...]
            s = u + _shift_rows(u, 1, row)
            w = 4
            while w <= win:
                q = w // 4
                s = _shift_rows(s, q, row) + _shift_rows(s, -q, row)
                w *= 2
            half = win // 2
            cnt = (jnp.minimum(row + half, T) - jnp.maximum(row - half, 0)).astype(F32)
            d = s / cnt - u
            y = jnp.dot(d.astype(BF16), w_ref[...], preferred_element_type=F32) * sc_ref[...]
            o_ref[...] = y.astype(BF16)


def _pool_mixer(z3, l, W):
    B, T, _ = z3.shape
    C = MXU_TILE
    ng = POOL_WIDTH // C
    return pl.pallas_call(
        _pool_kernel,
        grid=(B, ng),
        in_specs=[
            pl.BlockSpec((None, T, C), lambda b, g: (b, 0, OFF_POOL // C + g)),
            pl.BlockSpec((None, None, C, C), lambda b, g: (l, g, 0, 0)),
            pl.BlockSpec((None, 1, C), lambda b, g: (l, 0, g)),
        ],
        out_specs=pl.BlockSpec((None, T, C), lambda b, g: (b, 0, g)),
        out_shape=jax.ShapeDtypeStruct((B, T, POOL_WIDTH), BF16),
        compiler_params=_params(("arbitrary", "arbitrary")),
        name="pool_mixer",
    )(z3, W["pool_w"], W["pool_scale"])


def _merge_kernel(a_ref, r_ref, p_ref, g0_ref, g1_ref, g2_ref, wa_ref, wr_ref, wp_ref, o_ref):
    ya = jnp.dot(a_ref[...], wa_ref[...], preferred_element_type=F32)
    yr = jnp.dot(r_ref[...], wr_ref[...], preferred_element_type=F32)
    yp = jnp.dot(p_ref[...], wp_ref[...], preferred_element_type=F32)
    m = g0_ref[...].astype(F32) * ya + g1_ref[...].astype(F32) * yr + g2_ref[...].astype(F32) * yp
    o_ref[...] = m.astype(BF16)


def _merge(attn, lru, pool, gates, l, W, tm):
    M, K = attn.shape
    D = W["w_mla_o"].shape[-1]
    tn = _pick(D, (1024, 512, 256))
    nbr = D // tn
    act = pl.BlockSpec((tm, K), lambda i, j: (i, 0))
    wsp = pl.BlockSpec((None, K, tn), lambda i, j: (l, 0, j))
    gsp = lambda k: pl.BlockSpec((tm, tn), lambda i, j: (i, k * nbr + j))
    return pl.pallas_call(
        _merge_kernel,
        grid=(M // tm, D // tn),
        in_specs=[act, act, act, gsp(0), gsp(1), gsp(2), wsp, wsp, wsp],
        out_specs=pl.BlockSpec((tm, tn), lambda i, j: (i, j)),
        out_shape=jax.ShapeDtypeStruct((M, D), BF16),
        compiler_params=_params(("arbitrary", "arbitrary")),
        name="merge",
    )(attn, lru, pool, gates, gates, gates, W["w_mla_o"], W["w_lru_o"], W["w_pool_o"])


def _outproj_kernel(m_ref, w_ref, x_ref, g_ref, o_ref):
    y = jnp.dot(m_ref[...], w_ref[...], preferred_element_type=F32)
    o_ref[...] = x_ref[...] + g_ref[...] * y


def _out_proj(merged, x2d, mod4, row_of, l, W, tm):
    M, D = x2d.shape
    tn = _pick(D, (1024, 512, 256))
    nbr = D // tn
    return pl.pallas_call(
        _outproj_kernel,
        grid=(M // tm, D // tn),
        in_specs=[
            pl.BlockSpec((tm, D), lambda i, j: (i, 0)),
            pl.BlockSpec((None, D, tn), lambda i, j: (l, 0, j)),
            pl.BlockSpec((tm, tn), lambda i, j: (i, j)),
            pl.BlockSpec((None, None, 1, tn), lambda i, j: (l, row_of(i * tm), 0, 2 * nbr + j)),
        ],
        out_specs=pl.BlockSpec((tm, tn), lambda i, j: (i, j)),
        out_shape=jax.ShapeDtypeStruct((M, D), F32),
        compiler_params=_params(("arbitrary", "arbitrary")),
        name="out_proj",
    )(merged, W["w_out"], x2d, mod4)


def _norm_mod(x, nw, sc, sh):
    ms = jnp.mean(x * x, axis=-1, keepdims=True)
    return (x * lax.rsqrt(ms + EPS) * (nw * (1.0 + sc)) + sh).astype(BF16)


def _ffn_kernel(x_ref, xn_ref, nw_ref, sh_ref, sc_ref, shn_ref, scn_ref, g_ref, w1_ref, w2_ref, o_ref, h_scr):
    i = pl.program_id(0)
    f = pl.program_id(1)
    last = pl.num_programs(1) - 1

    @pl.when((i == 0) & (f == 0))
    def _():
        h_scr[...] = _norm_mod(x_ref[...], nw_ref[...], sc_ref[...], sh_ref[...])

    @pl.when(f == 0)
    def _():
        o_ref[...] = jnp.zeros_like(o_ref)

    def partial_sum():
        a = jnp.dot(h_scr[...], w1_ref[...], preferred_element_type=F32)
        a = jnp.maximum(a, 0.0)
        return jnp.dot((a * a).astype(BF16), w2_ref[...], preferred_element_type=F32)

    @pl.when(f < last)
    def _():
        o_ref[...] += partial_sum()

    @pl.when(f == last)
    def _():
        h_next = _norm_mod(xn_ref[...], nw_ref[...], scn_ref[...], shn_ref[...])
        acc = o_ref[...] + partial_sum()
        o_ref[...] = x_ref[...] + g_ref[...] * acc
        h_scr[...] = h_next


def _ffn(x2d, mod4, row_of, l, W, tm):
    M, D = x2d.shape
    F = W["w_ff1"].shape[-1]
    tf = _pick(F, (1024, 512, 256))
    assert F // tf >= 2
    cur = lambda i: i
    nxt = lambda i: jnp.minimum(i + 1, M // tm - 1)
    modv = lambda k, blk: pl.BlockSpec((None, None, 1, D), lambda i, f: (l, row_of(blk(i) * tm), 0, k))
    return pl.pallas_call(
        _ffn_kernel,
        grid=(M // tm, F // tf),
        in_specs=[
            pl.BlockSpec((tm, D), lambda i, f: (i, 0)),
            pl.BlockSpec((tm, D), lambda i, f: (nxt(i), 0)),
            pl.BlockSpec((None, 1, D), lambda i, f: (l, 0, 0)),
            modv(3, cur), modv(4, cur), modv(3, nxt), modv(4, nxt), modv(5, cur),
            pl.BlockSpec((None, D, tf), lambda i, f: (l, 0, f)),
            pl.BlockSpec((None, tf, D), lambda i, f: (l, f, 0)),
        ],
        out_specs=pl.BlockSpec((tm, D), lambda i, f: (i, 0)),
        out_shape=jax.ShapeDtypeStruct((M, D), F32),
        scratch_shapes=[pltpu.VMEM((tm, D), BF16)],
        compiler_params=_params(("arbitrary", "arbitrary")),
        name="ffn",
    )(x2d, x2d, W["norm2"], mod4, mod4, mod4, mod4, mod4, W["w_ff1"], W["w_ff2"])


def _pad_lanes(x, n=LANE):
    return jnp.pad(x, [(0, 0)] * (x.ndim - 1) + [(0, n - x.shape[-1])])


def _rot_half(x):
    h = x.shape[-1] // 2
    return jnp.concatenate([-x[..., h:], x[..., :h]], axis=-1)


def _swap_half(x):
    h = x.shape[-1] // 2
    return jnp.concatenate([x[..., h:], x[..., :h]], axis=-1)


def _norm_gains(g):
    gr = g[:, QK_NOPE:]
    return (g[:, None, :QK_NOPE], _pad_lanes(gr)[:, None, :], _pad_lanes(_swap_half(gr))[:, None, :])


def _block_diag_tiles(w, per):
    *lead, nb, s, _ = w.shape
    w = w.reshape(*lead, nb // per, per, s, s)
    t = jnp.einsum("...kij,km->...kimj", w, jnp.eye(per, dtype=w.dtype))
    return t.reshape(*lead, nb // per, per * s, per * s)


def _pack_weights(w_in, q_a_norm, w_qb, kv_a_norm, w_kvb, q_norm, k_norm, w_mla_o, conv_w, conv_b,
                  lru_wa, lru_ba, lru_wx, lru_bx, lru_lambda, w_lru_o, pool_w, pool_scale, w_pool_o,
                  w_out, w_ff1, w_ff2, norm1_w, norm2_w):
    L, D, _ = w_in.shape
    o_kr = Q_LORA + KV_LORA
    w_in = w_in.astype(BF16)
    wkr = w_in[..., o_kr:o_kr + QK_ROPE]
    w_in_p = jnp.concatenate(
        [w_in[..., :o_kr], _pad_lanes(wkr), _pad_lanes(_rot_half(wkr)), w_in[..., o_kr + QK_ROPE:]], axis=-1)

    wq = w_qb.reshape(L, Q_LORA, N_HEADS, QK_DIM)
    wq_rope = wq[..., QK_NOPE:]
    w_q = jnp.concatenate([wq[..., :QK_NOPE].reshape(L, Q_LORA, -1),
                           _pad_lanes(wq_rope).reshape(L, Q_LORA, -1)], axis=-1)
    w_qrot = _pad_lanes(_rot_half(wq_rope)).reshape(L, Q_LORA, -1)

    wkv = w_kvb.reshape(L, KV_LORA, N_HEADS, QK_NOPE + V_DIM)
    w_kv = jnp.concatenate([wkv[..., :QK_NOPE].reshape(L, KV_LORA, -1),
                            wkv[..., QK_NOPE:].reshape(L, KV_LORA, -1)], axis=-1)

    qg = _norm_gains(q_norm)
    kg = _norm_gains(k_norm)

    per = MXU_TILE // lru_wa.shape[-1]
    ta = _block_diag_tiles(lru_wa, per)
    tx = _block_diag_tiles(lru_wx, per)
    lru_w = jnp.concatenate([ta[:, 0], tx[:, 0], ta[:, 1], tx[:, 1]], axis=-1)
    nct = lru_w.shape[1]
    ba = lru_ba.reshape(L, 2, nct, 1, MXU_TILE)
    bx = lru_bx.reshape(L, 2, nct, 1, MXU_TILE)
    lru_b = jnp.concatenate([ba[:, 0], bx[:, 0], ba[:, 1], bx[:, 1]], axis=-1)

    return dict(
        w_in=w_in_p.astype(BF16), norm1=norm1_w[:, None, :], norm2=norm2_w[:, None, :],
        q_a_norm=q_a_norm[:, None, :], kv_a_norm=kv_a_norm[:, None, :],
        w_q=w_q.astype(BF16), w_qrot=w_qrot.astype(BF16), w_kv=w_kv.astype(BF16),
        qg_n=qg[0], qg_r=qg[1], qg_s=qg[2], kg_n=kg[0], kg_r=kg[1], kg_s=kg[2],
        w_mla_o=w_mla_o.astype(BF16), w_lru_o=w_lru_o.astype(BF16), w_pool_o=w_pool_o.astype(BF16),
        conv_w=conv_w, conv_b=conv_b[:, None, :], lru_w=lru_w.astype(BF16), lru_b=lru_b,
        lru_lambda=lru_lambda, pool_w=pool_w.astype(BF16), pool_scale=pool_scale[:, None, :],
        w_out=w_out.astype(BF16), w_ff1=w_ff1.astype(BF16), w_ff2=w_ff2.astype(BF16),
    )


def _rope_tables(T):
    rows = T // GRID_W
    row = jnp.repeat(jnp.arange(rows), GRID_W).astype(F32)
    col = jnp.tile(jnp.arange(GRID_W), rows).astype(F32)
    n_freq = QK_ROPE // 4
    inv = ROPE_BASE ** (-(jnp.arange(n_freq, dtype=F32) / n_freq))
    ang = jnp.concatenate([row[:, None] * inv, col[:, None] * inv], axis=-1)
    cos, sin = jnp.cos(ang), jnp.sin(ang)
    return (_pad_lanes(jnp.concatenate([cos, cos], axis=-1)), _pad_lanes(jnp.concatenate([sin, sin], axis=-1)))


def _trunk_layer(x2d, B, T, mod4, row_of, shared_cond, l, W, rope, ctx, h0):
    M = B * T
    span = M if shared_cond else T
    tm = _pick(span, (1024, 512, 256, 128))
    tmp = _pick(T, (512, 256, 128))
    z, gates = _in_proj(x2d, mod4, row_of, W["norm1"], W["w_in"], l, tm)
    z3 = z.reshape(B, T, -1)

    q, k, v, ckv = _qk_prep(z, B, T, l, W, rope, tmp)
    if ctx is None:
        kc = vc = None
    else:
        ckv_c, kr_c = ctx
        P = ckv_c.shape[2]
        kc, vc = _cache_k_prep(ckv_c, kr_c, B, P, l, W, _pick(P, (512, 256, 128)))
    attn = _attention(q, k, v, kc, vc, _pick(T, (ATTN_STEP_ROWS, 1024, 512, 256, 128)))

    lru, state = _lru_mixer(z3, h0, l, W, _pick(T, (256, 128)))
    pool = _pool_mixer(z3, l, W)

    merged = _merge(attn.reshape(M, -1), lru.reshape(M, -1), pool.reshape(M, -1), gates, l, W, tm)
    x2d = _out_proj(merged, x2d, mod4, row_of, l, W, tm)
    x2d = _ffn(x2d, mod4, row_of, l, W, _pick(span, (512, 256, 128)))
    return x2d, z, ckv, state


def kernel(x_prompt, x_sample, c, cache_ckv, cache_krope, state_lru, c_ctx, w_mod, b_mod, norm1_w, norm2_w, w_in, q_a_norm, w_qb, kv_a_norm, w_kvb, q_norm, k_norm, w_mla_o, conv_w, conv_b, lru_wa, lru_ba, lru_wx, lru_bx, lru_lambda, w_lru_o, pool_w, pool_scale, w_pool_o, w_out, w_ff1, w_ff2):
    L = w_in.shape[0]
    Bc, Tc, D = x_prompt.shape
    Bs, Ts, _ = x_sample.shape

    W = _pack_weights(w_in, q_a_norm, w_qb, kv_a_norm, w_kvb, q_norm, k_norm, w_mla_o, conv_w, conv_b,
                      lru_wa, lru_ba, lru_wx, lru_bx, lru_lambda, w_lru_o, pool_w, pool_scale, w_pool_o,
                      w_out, w_ff1, w_ff2, norm1_w, norm2_w)

    rows = Bs + 1
    rpad = -rows % 8
    cond = jnp.concatenate([c, c_ctx[None, :], jnp.zeros((rpad, D), F32)], axis=0)
    mod = _modulation(cond, w_mod, b_mod)
    mod4 = mod.reshape(L, rows + rpad, 1, N_MOD * D)

    rope = _rope_tables(Ts)
    kr_cache = _pad_lanes(cache_krope)
    zero_state = jnp.zeros((Bc, 2, LRU_WIDTH), F32)

    row_smp = lambda t: t // Ts
    row_ctx = lambda t: Bs

    xp = x_prompt.reshape(Bc * Tc, D)
    xs = x_sample.reshape(Bs * Ts, D)
    ckv_list, kr_list, st_list = [], [], []
    for l in range(L):
        xp, zc, ckv, st = _trunk_layer(xp, Bc, Tc, mod4, row_ctx, True, l, W, None, None, zero_state)
        ckv_list.append(ckv.reshape(Bc, Tc, KV_LORA))
        kr_list.append(zc[:, OFF_KR:OFF_KR + QK_ROPE].reshape(Bc, Tc, QK_ROPE))
        st_list.append(st)
        xs, _, _, _ = _trunk_layer(xs, Bs, Ts, mod4, row_smp, False, l, W, rope,
                                   (cache_ckv, kr_cache), state_lru[:, l])
    return (xp.reshape(Bc, Tc, D), xs.reshape(Bs, Ts, D), jnp.stack(ckv_list, axis=1),
            jnp.stack(kr_list, axis=1), jnp.stack(st_list, axis=1))
```

```python
import functools

import jax
import jax.numpy as jnp
from jax import lax
from jax.experimental import pallas as pl
from jax.experimental.pallas import tpu as pltpu

F32 = jnp.float32
BF16 = jnp.bfloat16

EPS = 1e-6
GRID_W = 64
N_HEADS = 8
QK_NOPE = 128
QK_ROPE = 64
V_DIM = 128
QK_DIM = QK_NOPE + QK_ROPE
Q_LORA = 512
KV_LORA = 256
ROPE_BASE = 10000.0
LRU_WIDTH = 1024
LRU_C = 8.0
POOL_WIDTH = 1024
POOL_WINDOWS = (2, 4, 8, 16)
N_MOD = 6

LANE = 128
SUBLANE = 8
MXU_TILE = 256
VMEM_LIMIT = 56 * 2**20
MIXER_STEP_ROWS = 1024
FFN_ROW_TILE = 1024
FFN_HIDDEN_TILE = 512
ATTN_STEP_ROWS = 2048
ATTN_CHAIN_ROWS = 256
ATTN_LOOKAHEAD = 2
Q_PRESCALE =QK_DIM ** -0.5 * 1.4426950408889634

OFF_Q = 0
OFF_KV = Q_LORA
OFF_KR = OFF_KV + KV_LORA
OFF_KROT = OFF_KR + LANE
OFF_LRU = OFF_KROT + LANE
OFF_GATE = OFF_LRU + LRU_WIDTH
OFF_POOL = OFF_GATE + LRU_WIDTH
OFF_BR = OFF_POOL + POOL_WIDTH


def _params(sem):
    return pltpu.CompilerParams(dimension_semantics=sem, vmem_limit_bytes=VMEM_LIMIT)


def _pick(n, prefs):
    for p in prefs:
        if n % p == 0:
            return p
    return n


def _mod_kernel(c_ref, w_ref, b_ref, o_ref):
    c = c_ref[...]
    s = c * jax.nn.sigmoid(c)
    o_ref[...] = jnp.dot(s.astype(BF16), w_ref[...].astype(BF16), preferred_element_type=F32) + b_ref[...]


def _modulation(cond, w_mod, b_mod):
    L, D, N = w_mod.shape
    R = cond.shape[0]
    tn = _pick(N, (1024, 512, 256, 128))
    return pl.pallas_call(
        _mod_kernel,
        grid=(L, N // tn),
        in_specs=[
            pl.BlockSpec((R, D), lambda l, j: (0, 0)),
            pl.BlockSpec((None, D, tn), lambda l, j: (l, 0, j)),
            pl.BlockSpec((None, 1, tn), lambda l, j: (l, 0, j)),
        ],
        out_specs=pl.BlockSpec((None, R, tn), lambda l, j: (l, 0, j)),
        out_shape=jax.ShapeDtypeStruct((L, R, N), F32),
        compiler_params=_params(("arbitrary", "arbitrary")),
        name="modulation",
    )(cond, w_mod, b_mod.reshape(L, 1, N))


def _inproj_kernel(x_ref, nw_ref, sh_ref, sc_ref, w_ref, z_ref, g_ref, h_scr, *, n_lin):
    j = pl.program_id(1)

    @pl.when(j == 0)
    def _():
        h_scr[...] = _norm_mod(x_ref[...], nw_ref[...], sc_ref[...], sh_ref[...])

    @pl.when(j < n_lin)
    def _():
        z_ref[...] = jnp.dot(h_scr[...], w_ref[...], preferred_element_type=F32)

    @pl.when(j >= n_lin)
    def _():
        g_ref[...] = _sigmoid(jnp.dot(h_scr[...], w_ref[...], preferred_element_type=F32)).astype(BF16)


def _in_proj(x2d, mod4, row_of, norm_w, w_in_p, l, tm):
    M, D = x2d.shape
    N = w_in_p.shape[-1]
    tn = _pick(OFF_BR, (1024, 512, 256))
    n_lin = OFF_BR // tn
    return pl.pallas_call(
        functools.partial(_inproj_kernel, n_lin=n_lin),
        grid=(M // tm, N // tn),
        in_specs=[
            pl.BlockSpec((tm, D), lambda i, j: (i, 0)),
            pl.BlockSpec((None, 1, D), lambda i, j: (l, 0, 0)),
            pl.BlockSpec((None, None, 1, D), lambda i, j: (l, row_of(i * tm), 0, 0)),
            pl.BlockSpec((None, None, 1, D), lambda i, j: (l, row_of(i * tm), 0, 1)),
            pl.BlockSpec((None, D, tn), lambda i, j: (l, 0, j)),
        ],
        out_specs=[
            pl.BlockSpec((tm, tn), lambda i, j: (i, jnp.minimum(j, n_lin - 1))),
            pl.BlockSpec((tm, tn), lambda i, j: (i, jnp.maximum(j - n_lin, 0))),
        ],
        out_shape=[
            jax.ShapeDtypeStruct((M, OFF_BR), F32),
            jax.ShapeDtypeStruct((M, N - OFF_BR), BF16),
        ],
        scratch_shapes=[pltpu.VMEM((tm, D), BF16)],
        compiler_params=_params(("arbitrary", "arbitrary")),
        name="in_proj",
    )(x2d, norm_w, mod4, mod4, w_in_p)


def _head_scale(nope, rope_sq_sum):
    ss = jnp.sum(nope * nope, axis=-1, keepdims=True) + rope_sq_sum
    return lax.rsqrt(ss * (1.0 / QK_DIM) + EPS)


def _qprep_kernel(*refs, use_rope):
    if use_rope:
        ql_ref, an_ref, w_ref, wrot_ref, gn_ref, gr_ref, gs_ref, cos_ref, sin_ref, q_ref = refs
    else:
        ql_ref, an_ref, w_ref, gn_ref, gr_ref, q_ref = refs
    ql = ql_ref[...]
    ms = jnp.mean(ql * ql, axis=-1, keepdims=True)
    qn = (ql * lax.rsqrt(ms + EPS) * an_ref[...]).astype(BF16)
    qq = jnp.dot(qn, w_ref[...], preferred_element_type=F32)
    if use_rope:
        qrot = jnp.dot(qn, wrot_ref[...], preferred_element_type=F32)
        cos = cos_ref[...]
        sin = sin_ref[...]
        gs = gs_ref[...]
    gn = gn_ref[...]
    gr = gr_ref[...]
    hw = N_HEADS * LANE
    for h in range(N_HEADS):
        nope = qq[:, h * LANE:(h + 1) * LANE]
        rope = qq[:, hw + h * LANE:hw + (h + 1) * LANE]
        s = _head_scale(nope, jnp.sum(rope * rope, axis=-1, keepdims=True))
        if use_rope:
            r = gr * rope * cos + gs * qrot[:, h * LANE:(h + 1) * LANE] * sin
        else:
            r = gr * rope
        s = s * Q_PRESCALE
        q_ref[h, :, 0:LANE] = (nope * s * gn).astype(BF16)
        q_ref[h, :, LANE:2 * LANE] = (r * s).astype(BF16)


def _q_prep_io(z, B, T, l, W, rope, tm):
    nb = T // tm
    use_rope = rope is not None
    wspec = lambda n: pl.BlockSpec((None, Q_LORA, n), lambda i: (l, 0, 0))
    vec = pl.BlockSpec((None, 1, LANE), lambda i: (l, 0, 0))
    in_specs = [
        pl.BlockSpec((tm, Q_LORA), lambda i: (i, OFF_Q // Q_LORA)),
        pl.BlockSpec((None, 1, Q_LORA), lambda i: (l, 0, 0)),
        wspec(2 * N_HEADS * LANE),
    ]
    args = [z, W["q_a_norm"], W["w_q"]]
    if use_rope:
        tab = pl.BlockSpec((tm, LANE), lambda i: (i % nb, 0))
        in_specs += [wspec(N_HEADS * LANE), vec, vec, vec, tab, tab]
        args += [W["w_qrot"], W["qg_n"], W["qg_r"], W["qg_s"], rope[0], rope[1]]
    else:
        in_specs += [vec, vec]
        args += [W["qg_n"], W["qg_r"]]
    out_specs = [pl.BlockSpec((None, N_HEADS, tm, 2 * LANE), lambda i: (i // nb, 0, i % nb, 0))]
    out_shape = [jax.ShapeDtypeStruct((B, N_HEADS, T, 2 * LANE), BF16)]
    return in_specs, args, out_specs, out_shape


def _kprep_kernel(*refs, use_rope, normalize):
    refs = list(refs)
    kv_ref = refs.pop(0)
    kr_ref = refs.pop(0)
    krot_ref = refs.pop(0) if use_rope else None
    an_ref = refs.pop(0) if normalize else None
    w_ref = refs.pop(0)
    gn_ref = refs.pop(0)
    gr_ref = refs.pop(0)
    if use_rope:
        gs_ref, cos_ref, sin_ref = refs.pop(0), refs.pop(0), refs.pop(0)
    k_ref = refs.pop(0)
    v_ref = refs.pop(0)
    ckv_ref = refs.pop(0) if normalize else None

    kv = kv_ref[...]
    if normalize:
        ms = jnp.mean(kv * kv, axis=-1, keepdims=True)
        kv = kv * lax.rsqrt(ms + EPS) * an_ref[...]
        ckv_ref[...] = kv
    kk = jnp.dot(kv.astype(BF16), w_ref[...], preferred_element_type=F32)
    kr = kr_ref[...]
    ssr = jnp.sum(kr * kr, axis=-1, keepdims=True)
    if use_rope:
        base = gr_ref[...] * kr * cos_ref[...] + gs_ref[...] * krot_ref[...] * sin_ref[...]
    else:
        base = gr_ref[...] * kr
    gn = gn_ref[...]
    hw = N_HEADS * LANE
    for h in range(N_HEADS):
        nope = kk[:, h * LANE:(h + 1) * LANE]
        s = _head_scale(nope, ssr)
        k_ref[h, :, 0:LANE] = (nope * s * gn).astype(BF16)
        k_ref[h, :, LANE:2 * LANE] = (base * s).astype(BF16)
        v_ref[h] = kk[:, hw + h * LANE:hw + (h + 1) * LANE].astype(BF16)


def _k_prep_io(kv_src, kr_src, B, T, l, W, rope, tm, *, kv_spec, kr_spec, krot_spec, normalize):
    M = B * T
    nb = T // tm
    use_rope = rope is not None
    vec = pl.BlockSpec((None, 1, LANE), lambda i: (l, 0, 0))
    in_specs = [kv_spec, kr_spec]
    args = [kv_src, kr_src]
    if use_rope:
        in_specs.append(krot_spec)
        args.append(kr_src)
    if normalize:
        in_specs.append(pl.BlockSpec((None, 1, KV_LORA), lambda i: (l, 0, 0)))
        args.append(W["kv_a_norm"])
    in_specs += [pl.BlockSpec((None, KV_LORA, 2 * N_HEADS * LANE), lambda i: (l, 0, 0)), vec, vec]
    args += [W["w_kv"], W["kg_n"], W["kg_r"]]
    if use_rope:
        tab = pl.BlockSpec((tm, LANE), lambda i: (i % nb, 0))
        in_specs += [vec, tab, tab]
        args += [W["kg_s"], rope[0], rope[1]]
    out_specs = [
        pl.BlockSpec((None, N_HEADS, tm, 2 * LANE), lambda i: (i // nb, 0, i % nb, 0)),
        pl.BlockSpec((None, N_HEADS, tm, LANE), lambda i: (i // nb, 0, i % nb, 0)),
    ]
    out_shape = [
        jax.ShapeDtypeStruct((B, N_HEADS, T, 2 * LANE), BF16),
        jax.ShapeDtypeStruct((B, N_HEADS, T, LANE), BF16),
    ]
    if normalize:
        out_specs.append(pl.BlockSpec((tm, KV_LORA), lambda i: (i, 0)))
        out_shape.append(jax.ShapeDtypeStruct((M, KV_LORA), F32))
    return in_specs, args, out_specs, out_shape


def _cache_k_prep(ckv_c, kr_c, B, P, l, W, tp):
    npb = P // tp
    in_specs, args, out_specs, out_shape = _k_prep_io(
        ckv_c, kr_c, B, P, l, W, None, tp,
        kv_spec=pl.BlockSpec((None, None, tp, KV_LORA), lambda i: (i // npb, l, i % npb, 0)),
        kr_spec=pl.BlockSpec((None, None, tp, LANE), lambda i: (i // npb, l, i % npb, 0)),
        krot_spec=None, normalize=False)
    return pl.pallas_call(
        functools.partial(_kprep_kernel, use_rope=False, normalize=False),
        grid=(B * P // tp,),
        in_specs=in_specs,
        out_specs=out_specs,
        out_shape=out_shape,
        compiler_params=_params(("arbitrary",)),
        name="k_prep",
    )(*args)


def _qkprep_kernel(*refs, use_rope, n_q_in, n_k_in):
    q_in, k_in = refs[:n_q_in], refs[n_q_in:n_q_in + n_k_in]
    q_out, k_out = refs[n_q_in + n_k_in], refs[n_q_in + n_k_in + 1:]
    _qprep_kernel(*q_in, q_out, use_rope=use_rope)
    _kprep_kernel(*k_in, *k_out, use_rope=use_rope, normalize=True)


def _qk_prep(z, B, T, l, W, rope, tm):
    q_io = _q_prep_io(z, B, T, l, W, rope, tm)
    k_io = _k_prep_io(
        z, z, B, T, l, W, rope, tm,
        kv_spec=pl.BlockSpec((tm, KV_LORA), lambda i: (i, OFF_KV // KV_LORA)),
        kr_spec=pl.BlockSpec((tm, LANE), lambda i: (i, OFF_KR // LANE)),
        krot_spec=pl.BlockSpec((tm, LANE), lambda i: (i, OFF_KROT // LANE)),
        normalize=True)
    return pl.pallas_call(
        functools.partial(_qkprep_kernel, use_rope=rope is not None, n_q_in=len(q_io[1]), n_k_in=len(k_io[1])),
        grid=(B * T // tm,),
        in_specs=q_io[0] + k_io[0],
        out_specs=q_io[2] + k_io[2],
        out_shape=q_io[3] + k_io[3],
        compiler_params=_params(("arbitrary",)),
        name="qk_prep",
    )(*q_io[1], *k_io[1])


def _attn_kernel(*refs, has_ctx, nsub):
    if has_ctx:
        q_ref, k_ref, v_ref, kc_ref, vc_ref, o_ref = refs
    else:
        q_ref, k_ref, v_ref, o_ref = refs
    nt = (((1,), (1,)), ((), ()))
    hb, tq, _ = q_ref.shape
    ts = tq // nsub

    def key_tiles(h):
        out = []
        for kref, vref in ((k_ref, v_ref), (kc_ref, vc_ref)) if has_ctx else ((k_ref, v_ref),):
            kt = min(MXU_TILE, kref.shape[1])
            out += [(kref.at[h], vref.at[h], t * kt, kt) for t in range(kref.shape[1] // kt)]
        return out

    def fold_lanes(x, op):
        out = x[:, :LANE]
        for c in range(1, x.shape[1] // LANE):
            out = op(out, x[:, c * LANE:(c + 1) * LANE])
        return out

    def scores(h, r):
        q = q_ref[h, r * ts:(r + 1) * ts, :]
        s_tiles = []
        mrun = None
        for kref, _, off, kt in key_tiles(h):
            st = lax.dot_general(q, kref[off:off + kt, :], nt, preferred_element_type=F32)
            s_tiles.append(st)
            tmax = fold_lanes(st, jnp.maximum)
            mrun = tmax if mrun is None else jnp.maximum(mrun, tmax)
        return s_tiles, jnp.max(mrun, axis=-1, keepdims=True)

    def weights_and_values(h, r, s_tiles, m):
        lrun = None
        o = None
        for st, (_, vref, off, kt) in zip(s_tiles, key_tiles(h)):
            p = jnp.exp2(st - m)
            psum = fold_lanes(p, jnp.add)
            lrun = psum if lrun is None else lrun + psum
            ot = jnp.dot(p.astype(BF16), vref[off:off + kt, :], preferred_element_type=F32)
            o = ot if o is None else o + ot
        den = jnp.sum(lrun, axis=-1, keepdims=True)
        o_ref[r * ts:(r + 1) * ts, h * LANE:(h + 1) * LANE] = (o / den).astype(BF16)

    chains = [(h, r) for h in range(hb) for r in range(nsub)]
    ahead = min(ATTN_LOOKAHEAD, len(chains))
    pending = [scores(*c) for c in chains[:ahead]]
    for n, c in enumerate(chains):
        s_tiles, m = pending.pop(0)
        if n + ahead < len(chains):
            pending.append(scores(*chains[n + ahead]))
        weights_and_values(*c, s_tiles, m)


def _attention(q, k, v, kc, vc, tq):
    B, H, T, _ = q.shape
    S = k.shape[2]
    has_ctx = kc is not None
    hb = max(1, min(H, ATTN_STEP_ROWS // tq))
    in_specs = [
        pl.BlockSpec((None, hb, tq, 2 * LANE), lambda b, h, i: (b, h, i, 0)),
        pl.BlockSpec((None, hb, S, 2 * LANE), lambda b, h, i: (b, h, 0, 0)),
        pl.BlockSpec((None, hb, S, LANE), lambda b, h, i: (b, h, 0, 0)),
    ]
    args = [q, k, v]
    if has_ctx:
        P = kc.shape[2]
        in_specs += [
            pl.BlockSpec((None, hb, P, 2 * LANE), lambda b, h, i: (b, h, 0, 0)),
            pl.BlockSpec((None, hb, P, LANE), lambda b, h, i: (b, h, 0, 0)),
        ]
        args += [kc, vc]
    nsub = max(1, tq // ATTN_CHAIN_ROWS)
    return pl.pallas_call(
        functools.partial(_attn_kernel, has_ctx=has_ctx, nsub=nsub),
        grid=(B, H // hb, T // tq),
        in_specs=in_specs,
        out_specs=pl.BlockSpec((None, tq, hb * LANE), lambda b, h, i: (b, i, h)),
        out_shape=jax.ShapeDtypeStruct((B, T, H * LANE), BF16),
        compiler_params=_params(("arbitrary", "arbitrary", "arbitrary")),
        name="attention",
    )(*args)


def _shift_rows(x, k, row):
    n = x.shape[0]
    rolled = pltpu.roll(x, k % n, 0)
    valid = (row >= k) if k > 0 else (row < n + k)
    return jnp.where(valid, rolled, 0.0)


def _scan_chunk(a, b, carry, sub, reverse):
    n = a.shape[0]
    ng = n // SUBLANE
    hs = [None] * ng
    for j in (range(ng - 1, -1, -1) if reverse else range(ng)):
        aj = a[j * SUBLANE:(j + 1) * SUBLANE]
        bj = b[j * SUBLANE:(j + 1) * SUBLANE]
        d = 1
        while d < SUBLANE:
            shift = SUBLANE - d if reverse else d
            valid = (sub < SUBLANE - d) if reverse else (sub >= d)
            a_s = pltpu.roll(aj, shift, 0)
            b_s = pltpu.roll(bj, shift, 0)
            bj = jnp.where(valid, aj * b_s, 0.0) + bj
            aj = jnp.where(valid, aj * a_s, aj)
            d *= 2
        h = aj * carry + bj
        hs[j] = h
        carry = h[0:1] if reverse else h[SUBLANE - 1:SUBLANE]
    return jnp.concatenate(hs, axis=0), carry


def _sigmoid(x):
    return 0.5 * jnp.tanh(0.5 * x) + 0.5


def _softplus(x):
    return jnp.maximum(x, 0.0) + jnp.log(1.0 + jnp.exp(-jnp.abs(x)))


def _gelu_tanh(x):
    return 0.5 * x * (1.0 + jnp.tanh(0.7978845608028654 * (x + 0.044715 * x * x * x)))


def _lru_kernel(u_ref, ug_ref, cw_ref, cb_ref, w_ref, b_ref, lam_ref, h0_ref, y_ref, st_ref,
                xc_scr, hf_scr, hb_scr, *, tc):
    C = xc_scr.shape[1]
    for c in range(w_ref.shape[0]):
        cs = slice(c * C, (c + 1) * C)
        _lru_tile(u_ref.at[:, cs], ug_ref.at[:, cs], cw_ref.at[:, cs], cb_ref.at[:, cs], w_ref.at[c],
                  b_ref.at[c], lam_ref.at[:, cs], h0_ref.at[:, cs], y_ref.at[:, cs], st_ref.at[:, cs],
                  xc_scr, hf_scr, hb_scr, tc=tc)


def _lru_tile(u_ref, ug_ref, cw_ref, cb_ref, w_ref, b_ref, lam_ref, h0_ref, y_ref, st_ref,
              xc_scr, hf_scr, hb_scr, *, tc):
    T, C = xc_scr.shape
    nch = T // tc
    row_t = lax.broadcasted_iota(jnp.int32, (T, C), 0)
    u = u_ref[...]
    cw = cw_ref[...]
    xc = (cb_ref[...] + cw[0:1] * _shift_rows(u, 2, row_t) + cw[1:2] * _shift_rows(u, 1, row_t)
          + cw[2:3] * u + cw[3:4] * _shift_rows(u, -1, row_t))
    xc_scr[...] = xc

    sub = lax.broadcasted_iota(jnp.int32, (SUBLANE, C), 0)
    sp = _softplus(-lam_ref[...])
    h0 = h0_ref[...]

    def gates(xcj, d):
        g = jnp.dot(xcj.astype(BF16), w_ref[:, 2 * d * C:2 * (d + 1) * C], preferred_element_type=F32)
        g = g + b_ref[:, 2 * d * C:2 * (d + 1) * C]
        r = _sigmoid(g[:, :C])
        i = _sigmoid(g[:, C:])
        log_a = -LRU_C * r * sp[d:d + 1]
        a = jnp.exp(log_a)
        v = jnp.tanh(-log_a) * (a * a + 1.0)
        b = jnp.where(v > 0.0, v * lax.rsqrt(v), 0.0) * (i * xcj)
        return a, b

    def scan_dir(t0, carry, d):
        a, b = gates(xc_scr[pl.ds(t0, tc), :], d)
        return _scan_chunk(a, b, carry, sub, d == 1)

    def emit(t0, hf, hb):
        y = (hf + hb) * _gelu_tanh(ug_ref[pl.ds(t0, tc), :])
        y_ref[pl.ds(t0, tc), :] = y.astype(BF16)

    def meet(jj, carry):
        tf0 = pl.multiple_of(jj * tc, tc)
        tb0 = pl.multiple_of((nch - 1 - jj) * tc, tc)
        hf, cf = scan_dir(tf0, carry[0], 0)
        hb, cb = scan_dir(tb0, carry[1], 1)
        hf_scr[pl.ds(tf0, tc), :] = hf
        hb_scr[pl.ds(tb0, tc), :] = hb
        return cf, cb

    def part(jj, carry):
        tf0 = pl.multiple_of(jj * tc, tc)
        tb0 = pl.multiple_of((nch - 1 - jj) * tc, tc)
        hf, cf = scan_dir(tf0, carry[0], 0)
        hb, cb = scan_dir(tb0, carry[1], 1)
        emit(tf0, hf, hb_scr[pl.ds(tf0, tc), :])
        emit(tb0, hf_scr[pl.ds(tb0, tc), :], hb)
        return cf, cb

    carry = (h0[0:1], h0[1:2])
    if nch % 2 == 0:
        carry = lax.fori_loop(0, nch // 2, meet, carry)
        carry = lax.fori_loop(nch // 2, nch, part, carry)
    else:
        assert nch == 1
        hf, cf = scan_dir(0, carry[0], 0)
        hb, cb = scan_dir(0, carry[1], 1)
        emit(0, hf, hb)
        carry = (cf, cb)
    st_ref[0:1, :] = carry[0]
    st_ref[1:2, :] = carry[1]


def _lru_mixer(z3, h0, l, W, tc):
    B, T, _ = z3.shape
    C = MXU_TILE
    nct = LRU_WIDTH // C
    nc = max(1, min(nct, MIXER_STEP_ROWS // T))
    Cw = nc * C
    return pl.pallas_call(
        functools.partial(_lru_kernel, tc=tc),
        grid=(B, nct // nc),
        in_specs=[
            pl.BlockSpec((None, T, Cw), lambda b, c: (b, 0, OFF_LRU // Cw + c)),
            pl.BlockSpec((None, T, Cw), lambda b, c: (b, 0, OFF_GATE // Cw + c)),
            pl.BlockSpec((None, 4, Cw), lambda b, c: (l, 0, c)),
            pl.BlockSpec((None, 1, Cw), lambda b, c: (l, 0, c)),
            pl.BlockSpec((None, nc, C, 4 * C), lambda b, c: (l, c, 0, 0)),
            pl.BlockSpec((None, nc, 1, 4 * C), lambda b, c: (l, c, 0, 0)),
            pl.BlockSpec((None, 2, Cw), lambda b, c: (l, 0, c)),
            pl.BlockSpec((None, 2, Cw), lambda b, c: (b, 0, c)),
        ],
        out_specs=[
            pl.BlockSpec((None, T, Cw), lambda b, c: (b, 0, c)),
            pl.BlockSpec((None, 2, Cw), lambda b, c: (b, 0, c)),
        ],
        out_shape=[
            jax.ShapeDtypeStruct((B, T, LRU_WIDTH), BF16),
            jax.ShapeDtypeStruct((B, 2, LRU_WIDTH), F32),
        ],
        scratch_shapes=[pltpu.VMEM((T, C), F32)] * 3,
        compiler_params=_params(("arbitrary", "arbitrary")),
        name="lru_mixer",
    )(z3, z3, W["conv_w"], W["conv_b"], W["lru_w"], W["lru_b"], W["lru_lambda"], h0)


def _pool_group(u_ref, w_ref, sc_ref, o_ref, win):
    T, C = u_ref.shape
    row = lax.broadcasted_iota(jnp.int32, (T, C), 0)
    u = u_ref[...]
    s = u + _shift_rows(u, 1, row)
    w = 4
    while w <= win:
        q = w // 4
        s = _shift_rows(s, q, row) + _shift_rows(s, -q, row)
        w *= 2
    half = win // 2
    cnt = (jnp.minimum(row + half, T) - jnp.maximum(row - half, 0)).astype(F32)
    d = s / cnt - u
    y = jnp.dot(d.astype(BF16), w_ref[...], preferred_element_type=F32) * sc_ref[...]
    o_ref[...] = y.astype(BF16)


def _pool_kernel(u_ref, w_ref, sc_ref, o_ref):
    ng_step = w_ref.shape[0]
    C = w_ref.shape[1]
    if ng_step == len(POOL_WINDOWS):
        for gi, win in enumerate(POOL_WINDOWS):
            cs = slice(gi * C, (gi + 1) * C)
            _pool_group(u_ref.at[:, cs], w_ref.at[gi], sc_ref.at[:, cs], o_ref.at[:, cs], win)
    else:
        g = pl.program_id(1)
        for gi, win in enumerate(POOL_WINDOWS):
            @pl.when(g == gi)
            def _(win=win):
                _pool_group(u_ref, w_ref.at[0], sc_ref, o_ref, win)


def _pool_mixer(z3, l, W):
    B, T, _ = z3.shape
    C = MXU_TILE
    ng = POOL_WIDTH // C
    ngs = ng if ng * T <= MIXER_STEP_ROWS else 1
    Cw = ngs * C
    return pl.pallas_call(
        _pool_kernel,
        grid=(B, ng // ngs),
        in_specs=[
            pl.BlockSpec((None, T, Cw), lambda b, g: (b, 0, OFF_POOL // Cw + g)),
            pl.BlockSpec((None, ngs, C, C), lambda b, g: (l, g, 0, 0)),
            pl.BlockSpec((None, 1, Cw), lambda b, g: (l, 0, g)),
        ],
        out_specs=pl.BlockSpec((None, T, Cw), lambda b, g: (b, 0, g)),
        out_shape=jax.ShapeDtypeStruct((B, T, POOL_WIDTH), BF16),
        compiler_params=_params(("arbitrary", "arbitrary")),
        name="pool_mixer",
    )(z3, W["pool_w"], W["pool_scale"])


def _merge_kernel(a_ref, r_ref, p_ref, g0_ref, g1_ref, g2_ref, wa_ref, wr_ref, wp_ref, o_ref):
    ya = jnp.dot(a_ref[...], wa_ref[...], preferred_element_type=F32)
    yr = jnp.dot(r_ref[...], wr_ref[...], preferred_element_type=F32)
    yp = jnp.dot(p_ref[...], wp_ref[...], preferred_element_type=F32)
    m = g0_ref[...].astype(F32) * ya + g1_ref[...].astype(F32) * yr + g2_ref[...].astype(F32) * yp
    o_ref[...] = m.astype(BF16)


def _merge(attn, lru, pool, gates, l, W, tm):
    M, K = attn.shape
    D = W["w_mla_o"].shape[-1]
    tn = _pick(D, (1024, 512, 256))
    nbr = D // tn
    act = pl.BlockSpec((tm, K), lambda i, j: (i, 0))
    wsp = pl.BlockSpec((None, K, tn), lambda i, j: (l, 0, j))
    gsp = lambda k: pl.BlockSpec((tm, tn), lambda i, j: (i, k * nbr + j))
    return pl.pallas_call(
        _merge_kernel,
        grid=(M // tm, D // tn),
        in_specs=[act, act, act, gsp(0), gsp(1), gsp(2), wsp, wsp, wsp],
        out_specs=pl.BlockSpec((tm, tn), lambda i, j: (i, j)),
        out_shape=jax.ShapeDtypeStruct((M, D), BF16),
        compiler_params=_params(("arbitrary", "arbitrary")),
        name="merge",
    )(attn, lru, pool, gates, gates, gates, W["w_mla_o"], W["w_lru_o"], W["w_pool_o"])


def _outproj_kernel(m_ref, w_ref, x_ref, g_ref, o_ref):
    y = jnp.dot(m_ref[...], w_ref[...], preferred_element_type=F32)
    o_ref[...] = x_ref[...] + g_ref[...] * y


def _out_proj(merged, x2d, mod4, row_of, l, W, tm):
    M, D = x2d.shape
    tn = _pick(D, (1024, 512, 256))
    nbr = D // tn
    return pl.pallas_call(
        _outproj_kernel,
        grid=(M // tm, D // tn),
        in_specs=[
            pl.BlockSpec((tm, D), lambda i, j: (i, 0)),
            pl.BlockSpec((None, D, tn), lambda i, j: (l, 0, j)),
            pl.BlockSpec((tm, tn), lambda i, j: (i, j)),
            pl.BlockSpec((None, None, 1, tn), lambda i, j: (l, row_of(i * tm), 0, 2 * nbr + j)),
        ],
        out_specs=pl.BlockSpec((tm, tn), lambda i, j: (i, j)),
        out_shape=jax.ShapeDtypeStruct((M, D), F32),
        compiler_params=_params(("arbitrary", "arbitrary")),
        name="out_proj",
    )(merged, W["w_out"], x2d, mod4)


def _norm_mod(x, nw, sc, sh):
    ms = jnp.mean(x * x, axis=-1, keepdims=True)
    return (x * lax.rsqrt(ms + EPS) * (nw * (1.0 + sc)) + sh).astype(BF16)


def _ffn_kernel(x_ref, nw_ref, sh_ref, sc_ref, g_ref, w1_ref, w2_ref, o_ref, h_scr):
    f = pl.program_id(1)

    @pl.when(f == 0)
    def _():
        h_scr[...] = _norm_mod(x_ref[...], nw_ref[...], sc_ref[...], sh_ref[...])
        o_ref[...] = jnp.zeros_like(o_ref)

    a = jnp.dot(h_scr[...], w1_ref[...], preferred_element_type=F32)
    a = jnp.maximum(a, 0.0)
    o_ref[...] += jnp.dot((a * a).astype(BF16), w2_ref[...], preferred_element_type=F32)

    @pl.when(f == pl.num_programs(1) - 1)
    def _():
        o_ref[...] = x_ref[...] + g_ref[...] * o_ref[...]


def _ffn(x2d, mod4, row_of, l, W, tm):
    M, D = x2d.shape
    F = W["w_ff1"].shape[-1]
    tf = _pick(F, (FFN_HIDDEN_TILE, 256))
    modv = lambda k: pl.BlockSpec((None, None, 1, D), lambda i, f: (l, row_of(i * tm), 0, k))
    return pl.pallas_call(
        _ffn_kernel,
        grid=(M // tm, F // tf),
        in_specs=[
            pl.BlockSpec((tm, D), lambda i, f: (i, 0)),
            pl.BlockSpec((None, 1, D), lambda i, f: (l, 0, 0)),
            modv(3), modv(4), modv(5),
            pl.BlockSpec((None, D, tf), lambda i, f: (l, 0, f)),
            pl.BlockSpec((None, tf, D), lambda i, f: (l, f, 0)),
        ],
        out_specs=pl.BlockSpec((tm, D), lambda i, f: (i, 0)),
        out_shape=jax.ShapeDtypeStruct((M, D), F32),
        scratch_shapes=[pltpu.VMEM((tm, D), BF16)],
        compiler_params=_params(("arbitrary", "arbitrary")),
        name="ffn",
    )(x2d, W["norm2"], mod4, mod4, mod4, W["w_ff1"], W["w_ff2"])


def _pad_lanes(x, n=LANE):
    return jnp.pad(x, [(0, 0)] * (x.ndim - 1) + [(0, n - x.shape[-1])])


def _rot_half(x):
    h = x.shape[-1] // 2
    return jnp.concatenate([-x[..., h:], x[..., :h]], axis=-1)


def _swap_half(x):
    h = x.shape[-1] // 2
    return jnp.concatenate([x[..., h:], x[..., :h]], axis=-1)


def _norm_gains(g):
    gr = g[:, QK_NOPE:]
    return (g[:, None, :QK_NOPE], _pad_lanes(gr)[:, None, :], _pad_lanes(_swap_half(gr))[:, None, :])


def _block_diag_tiles(w, per):
    *lead, nb, s, _ = w.shape
    w = w.reshape(*lead, nb // per, per, s, s)
    t = jnp.einsum("...kij,km->...kimj", w, jnp.eye(per, dtype=w.dtype))
    return t.reshape(*lead, nb // per, per * s, per * s)


def _pack_weights(w_in, q_a_norm, w_qb, kv_a_norm, w_kvb, q_norm, k_norm, w_mla_o, conv_w, conv_b,
                  lru_wa, lru_ba, lru_wx, lru_bx, lru_lambda, w_lru_o, pool_w, pool_scale, w_pool_o,
                  w_out, w_ff1, w_ff2, norm1_w, norm2_w):
    L, D, _ = w_in.shape
    o_kr = Q_LORA + KV_LORA
    w_in = w_in.astype(BF16)
    wkr = w_in[..., o_kr:o_kr + QK_ROPE]
    w_in_p = jnp.concatenate(
        [w_in[..., :o_kr], _pad_lanes(wkr), _pad_lanes(_rot_half(wkr)), w_in[..., o_kr + QK_ROPE:]], axis=-1)

    wq = w_qb.reshape(L, Q_LORA, N_HEADS, QK_DIM)
    wq_rope = wq[..., QK_NOPE:]
    w_q = jnp.concatenate([wq[..., :QK_NOPE].reshape(L, Q_LORA, -1),
                           _pad_lanes(wq_rope).reshape(L, Q_LORA, -1)], axis=-1)
    w_qrot = _pad_lanes(_rot_half(wq_rope)).reshape(L, Q_LORA, -1)

    wkv = w_kvb.reshape(L, KV_LORA, N_HEADS, QK_NOPE + V_DIM)
    w_kv = jnp.concatenate([wkv[..., :QK_NOPE].reshape(L, KV_LORA, -1),
                            wkv[..., QK_NOPE:].reshape(L, KV_LORA, -1)], axis=-1)

    qg = _norm_gains(q_norm)
    kg = _norm_gains(k_norm)

    per = MXU_TILE // lru_wa.shape[-1]
    ta = _block_diag_tiles(lru_wa, per)
    tx = _block_diag_tiles(lru_wx, per)
    lru_w = jnp.concatenate([ta[:, 0], tx[:, 0], ta[:, 1], tx[:, 1]], axis=-1)
    nct = lru_w.shape[1]
    ba = lru_ba.reshape(L, 2, nct, 1, MXU_TILE)
    bx = lru_bx.reshape(L, 2, nct, 1, MXU_TILE)
    lru_b = jnp.concatenate([ba[:, 0], bx[:, 0], ba[:, 1], bx[:, 1]], axis=-1)

    return dict(
        w_in=w_in_p.astype(BF16), norm1=norm1_w[:, None, :], norm2=norm2_w[:, None, :],
        q_a_norm=q_a_norm[:, None, :], kv_a_norm=kv_a_norm[:, None, :],
        w_q=w_q.astype(BF16), w_qrot=w_qrot.astype(BF16), w_kv=w_kv.astype(BF16),
        qg_n=qg[0], qg_r=qg[1], qg_s=qg[2], kg_n=kg[0], kg_r=kg[1], kg_s=kg[2],
        w_mla_o=w_mla_o.astype(BF16), w_lru_o=w_lru_o.astype(BF16), w_pool_o=w_pool_o.astype(BF16),
        conv_w=conv_w, conv_b=conv_b[:, None, :], lru_w=lru_w.astype(BF16), lru_b=lru_b,
        lru_lambda=lru_lambda, pool_w=pool_w.astype(BF16), pool_scale=pool_scale[:, None, :],
        w_out=w_out.astype(BF16), w_ff1=w_ff1.astype(BF16), w_ff2=w_ff2.astype(BF16),
    )


def _rope_tables(T):
    rows = T // GRID_W
    row = jnp.repeat(jnp.arange(rows), GRID_W).astype(F32)
    col = jnp.tile(jnp.arange(GRID_W), rows).astype(F32)
    n_freq = QK_ROPE // 4
    inv = ROPE_BASE ** (-(jnp.arange(n_freq, dtype=F32) / n_freq))
    ang = jnp.concatenate([row[:, None] * inv, col[:, None] * inv], axis=-1)
    cos, sin = jnp.cos(ang), jnp.sin(ang)
    return (_pad_lanes(jnp.concatenate([cos, cos], axis=-1)), _pad_lanes(jnp.concatenate([sin, sin], axis=-1)))


def _trunk_layer(x2d, B, T, mod4, row_of, shared_cond, l, W, rope, ctx, h0):
    M = B * T
    span = M if shared_cond else T
    tm = _pick(span, (1024, 512, 256, 128))
    tmp = _pick(T, (512, 256, 128))
    z, gates = _in_proj(x2d, mod4, row_of, W["norm1"], W["w_in"], l, tm)
    z3 = z.reshape(B, T, -1)

    q, k, v, ckv = _qk_prep(z, B, T, l, W, rope, tmp)
    if ctx is None:
        kc = vc = None
    else:
        ckv_c, kr_c = ctx
        P = ckv_c.shape[2]
        kc, vc = _cache_k_prep(ckv_c, kr_c, B, P, l, W, _pick(P, (512, 256, 128)))
    attn = _attention(q, k, v, kc, vc, _pick(T, (ATTN_STEP_ROWS, 1024, 512, 256, 128)))

    lru, state = _lru_mixer(z3, h0, l, W, _pick(T, (256, 128)))
    pool = _pool_mixer(z3, l, W)

    merged = _merge(attn.reshape(M, -1), lru.reshape(M, -1), pool.reshape(M, -1), gates, l, W, tm)
    x2d = _out_proj(merged, x2d, mod4, row_of, l, W, tm)
    x2d = _ffn(x2d, mod4, row_of, l, W, _pick(span, (FFN_ROW_TILE, 512, 256, 128)))
    return x2d, z, ckv, state


def kernel(x_prompt, x_sample, c, cache_ckv, cache_krope, state_lru, c_ctx, w_mod, b_mod, norm1_w, norm2_w, w_in, q_a_norm, w_qb, kv_a_norm, w_kvb, q_norm, k_norm, w_mla_o, conv_w, conv_b, lru_wa, lru_ba, lru_wx, lru_bx, lru_lambda, w_lru_o, pool_w, pool_scale, w_pool_o, w_out, w_ff1, w_ff2):
    L = w_in.shape[0]
    Bc, Tc, D = x_prompt.shape
    Bs, Ts, _ = x_sample.shape

    W = _pack_weights(w_in, q_a_norm, w_qb, kv_a_norm, w_kvb, q_norm, k_norm, w_mla_o, conv_w, conv_b,
                      lru_wa, lru_ba, lru_wx, lru_bx, lru_lambda, w_lru_o, pool_w, pool_scale, w_pool_o,
                      w_out, w_ff1, w_ff2, norm1_w, norm2_w)

    rows = Bs + 1
    rpad = -rows % 8
    cond = jnp.concatenate([c, c_ctx[None, :], jnp.zeros((rpad, D), F32)], axis=0)
    mod = _modulation(cond, w_mod, b_mod)
    mod4 = mod.reshape(L, rows + rpad, 1, N_MOD * D)

    rope = _rope_tables(Ts)
    kr_cache = _pad_lanes(cache_krope)
    zero_state = jnp.zeros((Bc, 2, LRU_WIDTH), F32)

    row_smp = lambda t: t // Ts
    row_ctx = lambda t: Bs

    xp = x_prompt.reshape(Bc * Tc, D)
    xs = x_sample.reshape(Bs * Ts, D)
    ckv_list, kr_list, st_list = [], [], []
    for l in range(L):
        xp, zc, ckv, st = _trunk_layer(xp, Bc, Tc, mod4, row_ctx, True, l, W, None, None, zero_state)
        ckv_list.append(ckv.reshape(Bc, Tc, KV_LORA))
        kr_list.append(zc[:, OFF_KR:OFF_KR + QK_ROPE].reshape(Bc, Tc, QK_ROPE))
        st_list.append(st)
        xs, _, _, _ = _trunk_layer(xs, Bs, Ts, mod4, row_smp, False, l, W, rope,
                                   (cache_ckv, kr_cache), state_lru[:, l])
    return (xp.reshape(Bc, Tc, D), xs.reshape(Bs, Ts, D), jnp.stack(ckv_list, axis=1),
            jnp.stack(kr_list, axis=1), jnp.stack(st_list, axis=1))
```

```python
import functools

import jax
import jax.numpy as jnp
from jax import lax
from jax.experimental import pallas as pl
from jax.experimental.pallas import tpu as pltpu

F32 = jnp.float32
BF16 = jnp.bfloat16

EPS = 1e-6
GRID_W = 64
N_HEADS = 8
QK_NOPE = 128
QK_ROPE = 64
V_DIM = 128
QK_DIM = QK_NOPE + QK_ROPE
Q_LORA = 512
KV_LORA = 256
ROPE_BASE = 10000.0
LRU_WIDTH = 1024
LRU_C = 8.0
POOL_WIDTH = 1024
POOL_WINDOWS = (2, 4, 8, 16)
N_MOD = 6

LANE = 128
SUBLANE = 8
MXU_TILE = 256
VMEM_LIMIT = 56 * 2**20
MIXER_STEP_ROWS = 1024
FFN_ROW_TILE = 512
FFN_HIDDEN_TILE = 1024
ATTN_STEP_ROWS = 4096
ATTN_CHAIN_ROWS = 256
ATTN_LOOKAHEAD = 2
Q_PRESCALE =QK_DIM ** -0.5 * 1.4426950408889634

OFF_Q = 0
OFF_KV = Q_LORA
OFF_KR = OFF_KV + KV_LORA
OFF_KROT = OFF_KR + LANE
OFF_LRU = OFF_KROT + LANE
OFF_GATE = OFF_LRU + LRU_WIDTH
OFF_POOL = OFF_GATE + LRU_WIDTH
OFF_BR = OFF_POOL + POOL_WIDTH


def _params(sem):
    return pltpu.CompilerParams(dimension_semantics=sem, vmem_limit_bytes=VMEM_LIMIT)


def _pick(n, prefs):
    for p in prefs:
        if n % p == 0:
            return p
    return n


def _mod_kernel(c_ref, w_ref, b_ref, o_ref):
    c = c_ref[...]
    s = c * jax.nn.sigmoid(c)
    o_ref[...] = jnp.dot(s.astype(BF16), w_ref[...].astype(BF16), preferred_element_type=F32) + b_ref[...]


def _modulation(cond, w_mod, b_mod):
    L, D, N = w_mod.shape
    R = cond.shape[0]
    tn = _pick(N, (1024, 512, 256, 128))
    return pl.pallas_call(
        _mod_kernel,
        grid=(L, N // tn),
        in_specs=[
            pl.BlockSpec((R, D), lambda l, j: (0, 0)),
            pl.BlockSpec((None, D, tn), lambda l, j: (l, 0, j)),
            pl.BlockSpec((None, 1, tn), lambda l, j: (l, 0, j)),
        ],
        out_specs=pl.BlockSpec((None, R, tn), lambda l, j: (l, 0, j)),
        out_shape=jax.ShapeDtypeStruct((L, R, N), F32),
        compiler_params=_params(("arbitrary", "arbitrary")),
        name="modulation",
    )(cond, w_mod, b_mod.reshape(L, 1, N))


def _inproj_kernel(x_ref, nw_ref, sh_ref, sc_ref, w_ref, z_ref, g_ref, h_scr, *, n_lin):
    j = pl.program_id(1)

    @pl.when(j == 0)
    def _():
        h_scr[...] = _norm_mod(x_ref[...], nw_ref[...], sc_ref[...], sh_ref[...])

    @pl.when(j < n_lin)
    def _():
        z_ref[...] = jnp.dot(h_scr[...], w_ref[...], preferred_element_type=F32)

    @pl.when(j >= n_lin)
    def _():
        g_ref[...] = _sigmoid(jnp.dot(h_scr[...], w_ref[...], preferred_element_type=F32)).astype(BF16)


def _in_proj(x2d, mod4, row_of, norm_w, w_in_p, l, tm):
    M, D = x2d.shape
    N = w_in_p.shape[-1]
    tn = _pick(OFF_BR, (1024, 512, 256))
    n_lin = OFF_BR // tn
    return pl.pallas_call(
        functools.partial(_inproj_kernel, n_lin=n_lin),
        grid=(M // tm, N // tn),
        in_specs=[
            pl.BlockSpec((tm, D), lambda i, j: (i, 0)),
            pl.BlockSpec((None, 1, D), lambda i, j: (l, 0, 0)),
            pl.BlockSpec((None, None, 1, D), lambda i, j: (l, row_of(i * tm), 0, 0)),
            pl.BlockSpec((None, None, 1, D), lambda i, j: (l, row_of(i * tm), 0, 1)),
            pl.BlockSpec((None, D, tn), lambda i, j: (l, 0, j)),
        ],
        out_specs=[
            pl.BlockSpec((tm, tn), lambda i, j: (i, jnp.minimum(j, n_lin - 1))),
            pl.BlockSpec((tm, tn), lambda i, j: (i, jnp.maximum(j - n_lin, 0))),
        ],
        out_shape=[
            jax.ShapeDtypeStruct((M, OFF_BR), F32),
            jax.ShapeDtypeStruct((M, N - OFF_BR), BF16),
        ],
        scratch_shapes=[pltpu.VMEM((tm, D), BF16)],
        compiler_params=_params(("arbitrary", "arbitrary")),
        name="in_proj",
    )(x2d, norm_w, mod4, mod4, w_in_p)


def _head_scale(nope, rope_sq_sum):
    ss = jnp.sum(nope * nope, axis=-1, keepdims=True) + rope_sq_sum
    return lax.rsqrt(ss * (1.0 / QK_DIM) + EPS)


def _qprep_kernel(*refs, use_rope):
    if use_rope:
        ql_ref, an_ref, w_ref, wrot_ref, gn_ref, gr_ref, gs_ref, cos_ref, sin_ref, q_ref = refs
    else:
        ql_ref, an_ref, w_ref, gn_ref, gr_ref, q_ref = refs
    ql = ql_ref[...]
    ms = jnp.mean(ql * ql, axis=-1, keepdims=True)
    qn = (ql * lax.rsqrt(ms + EPS) * an_ref[...]).astype(BF16)
    qq = jnp.dot(qn, w_ref[...], preferred_element_type=F32)
    if use_rope:
        qrot = jnp.dot(qn, wrot_ref[...], preferred_element_type=F32)
        cos = cos_ref[...]
        sin = sin_ref[...]
        gs = gs_ref[...]
    gn = gn_ref[...]
    gr = gr_ref[...]
    hw = N_HEADS * LANE
    for h in range(N_HEADS):
        nope = qq[:, h * LANE:(h + 1) * LANE]
        rope = qq[:, hw + h * LANE:hw + (h + 1) * LANE]
        s = _head_scale(nope, jnp.sum(rope * rope, axis=-1, keepdims=True))
        if use_rope:
            r = gr * rope * cos + gs * qrot[:, h * LANE:(h + 1) * LANE] * sin
        else:
            r = gr * rope
        s = s * Q_PRESCALE
        q_ref[h, :, 0:LANE] = (nope * s * gn).astype(BF16)
        q_ref[h, :, LANE:2 * LANE] = (r * s).astype(BF16)


def _q_prep_io(z, B, T, l, W, rope, tm):
    nb = T // tm
    use_rope = rope is not None
    wspec = lambda n: pl.BlockSpec((None, Q_LORA, n), lambda i: (l, 0, 0))
    vec = pl.BlockSpec((None, 1, LANE), lambda i: (l, 0, 0))
    in_specs = [
        pl.BlockSpec((tm, Q_LORA), lambda i: (i, OFF_Q // Q_LORA)),
        pl.BlockSpec((None, 1, Q_LORA), lambda i: (l, 0, 0)),
        wspec(2 * N_HEADS * LANE),
    ]
    args = [z, W["q_a_norm"], W["w_q"]]
    if use_rope:
        tab = pl.BlockSpec((tm, LANE), lambda i: (i % nb, 0))
        in_specs += [wspec(N_HEADS * LANE), vec, vec, vec, tab, tab]
        args += [W["w_qrot"], W["qg_n"], W["qg_r"], W["qg_s"], rope[0], rope[1]]
    else:
        in_specs += [vec, vec]
        args += [W["qg_n"], W["qg_r"]]
    out_specs = [pl.BlockSpec((None, N_HEADS, tm, 2 * LANE), lambda i: (i // nb, 0, i % nb, 0))]
    out_shape = [jax.ShapeDtypeStruct((B, N_HEADS, T, 2 * LANE), BF16)]
    return in_specs, args, out_specs, out_shape


def _kprep_kernel(*refs, use_rope, normalize):
    refs = list(refs)
    kv_ref = refs.pop(0)
    kr_ref = refs.pop(0)
    krot_ref = refs.pop(0) if use_rope else None
    an_ref = refs.pop(0) if normalize else None
    w_ref = refs.pop(0)
    gn_ref = refs.pop(0)
    gr_ref = refs.pop(0)
    if use_rope:
        gs_ref, cos_ref, sin_ref = refs.pop(0), refs.pop(0), refs.pop(0)
    k_ref = refs.pop(0)
    v_ref = refs.pop(0)
    ckv_ref = refs.pop(0) if normalize else None

    kv = kv_ref[...]
    if normalize:
        ms = jnp.mean(kv * kv, axis=-1, keepdims=True)
        kv = kv * lax.rsqrt(ms + EPS) * an_ref[...]
        ckv_ref[...] = kv
    kk = jnp.dot(kv.astype(BF16), w_ref[...], preferred_element_type=F32)
    kr = kr_ref[...]
    ssr = jnp.sum(kr * kr, axis=-1, keepdims=True)
    if use_rope:
        base = gr_ref[...] * kr * cos_ref[...] + gs_ref[...] * krot_ref[...] * sin_ref[...]
    else:
        base = gr_ref[...] * kr
    gn = gn_ref[...]
    hw = N_HEADS * LANE
    for h in range(N_HEADS):
        nope = kk[:, h * LANE:(h + 1) * LANE]
        s = _head_scale(nope, ssr)
        k_ref[h, :, 0:LANE] = (nope * s * gn).astype(BF16)
        k_ref[h, :, LANE:2 * LANE] = (base * s).astype(BF16)
        v_ref[h] = kk[:, hw + h * LANE:hw + (h + 1) * LANE].astype(BF16)


def _k_prep_io(kv_src, kr_src, B, T, l, W, rope, tm, *, kv_spec, kr_spec, krot_spec, normalize):
    M = B * T
    nb = T // tm
    use_rope = rope is not None
    vec = pl.BlockSpec((None, 1, LANE), lambda i: (l, 0, 0))
    in_specs = [kv_spec, kr_spec]
    args = [kv_src, kr_src]
    if use_rope:
        in_specs.append(krot_spec)
        args.append(kr_src)
    if normalize:
        in_specs.append(pl.BlockSpec((None, 1, KV_LORA), lambda i: (l, 0, 0)))
        args.append(W["kv_a_norm"])
    in_specs += [pl.BlockSpec((None, KV_LORA, 2 * N_HEADS * LANE), lambda i: (l, 0, 0)), vec, vec]
    args += [W["w_kv"], W["kg_n"], W["kg_r"]]
    if use_rope:
        tab = pl.BlockSpec((tm, LANE), lambda i: (i % nb, 0))
        in_specs += [vec, tab, tab]
        args += [W["kg_s"], rope[0], rope[1]]
    out_specs = [
        pl.BlockSpec((None, N_HEADS, tm, 2 * LANE), lambda i: (i // nb, 0, i % nb, 0)),
        pl.BlockSpec((None, N_HEADS, tm, LANE), lambda i: (i // nb, 0, i % nb, 0)),
    ]
    out_shape = [
        jax.ShapeDtypeStruct((B, N_HEADS, T, 2 * LANE), BF16),
        jax.ShapeDtypeStruct((B, N_HEADS, T, LANE), BF16),
    ]
    if normalize:
        out_specs.append(pl.BlockSpec((tm, KV_LORA), lambda i: (i, 0)))
        out_shape.append(jax.ShapeDtypeStruct((M, KV_LORA), F32))
    return in_specs, args, out_specs, out_shape


def _cache_k_prep(ckv_c, kr_c, B, P, l, W, tp):
    npb = P // tp
    in_specs, args, out_specs, out_shape = _k_prep_io(
        ckv_c, kr_c, B, P, l, W, None, tp,
        kv_spec=pl.BlockSpec((None, None, tp, KV_LORA), lambda i: (i // npb, l, i % npb, 0)),
        kr_spec=pl.BlockSpec((None, None, tp, LANE), lambda i: (i // npb, l, i % npb, 0)),
        krot_spec=None, normalize=False)
    return pl.pallas_call(
        functools.partial(_kprep_kernel, use_rope=False, normalize=False),
        grid=(B * P // tp,),
        in_specs=in_specs,
        out_specs=out_specs,
        out_shape=out_shape,
        compiler_params=_params(("arbitrary",)),
        name="k_prep",
    )(*args)


def _qkprep_kernel(*refs, use_rope, n_q_in, n_k_in):
    q_in, k_in = refs[:n_q_in], refs[n_q_in:n_q_in + n_k_in]
    q_out, k_out = refs[n_q_in + n_k_in], refs[n_q_in + n_k_in + 1:]
    _qprep_kernel(*q_in, q_out, use_rope=use_rope)
    _kprep_kernel(*k_in, *k_out, use_rope=use_rope, normalize=True)


def _qk_prep(z, B, T, l, W, rope, tm):
    q_io = _q_prep_io(z, B, T, l, W, rope, tm)
    k_io = _k_prep_io(
        z, z, B, T, l, W, rope, tm,
        kv_spec=pl.BlockSpec((tm, KV_LORA), lambda i: (i, OFF_KV // KV_LORA)),
        kr_spec=pl.BlockSpec((tm, LANE), lambda i: (i, OFF_KR // LANE)),
        krot_spec=pl.BlockSpec((tm, LANE), lambda i: (i, OFF_KROT // LANE)),
        normalize=True)
    return pl.pallas_call(
        functools.partial(_qkprep_kernel, use_rope=rope is not None, n_q_in=len(q_io[1]), n_k_in=len(k_io[1])),
        grid=(B * T // tm,),
        in_specs=q_io[0] + k_io[0],
        out_specs=q_io[2] + k_io[2],
        out_shape=q_io[3] + k_io[3],
        compiler_params=_params(("arbitrary",)),
        name="qk_prep",
    )(*q_io[1], *k_io[1])


def _attn_kernel(*refs, has_ctx, nsub):
    if has_ctx:
        q_ref, k_ref, v_ref, kc_ref, vc_ref, o_ref = refs
    else:
        q_ref, k_ref, v_ref, o_ref = refs
    nt = (((1,), (1,)), ((), ()))
    hb, tq, _ = q_ref.shape
    ts = tq // nsub

    def key_tiles(h):
        out = []
        for kref, vref in ((k_ref, v_ref), (kc_ref, vc_ref)) if has_ctx else ((k_ref, v_ref),):
            kt = min(MXU_TILE, kref.shape[1])
            out += [(kref.at[h], vref.at[h], t * kt, kt) for t in range(kref.shape[1] // kt)]
        return out

    def fold_lanes(x, op):
        out = x[:, :LANE]
        for c in range(1, x.shape[1] // LANE):
            out = op(out, x[:, c * LANE:(c + 1) * LANE])
        return out

    def scores(h, r):
        q = q_ref[h, r * ts:(r + 1) * ts, :]
        s_tiles = []
        mrun = None
        for kref, _, off, kt in key_tiles(h):
            st = lax.dot_general(q, kref[off:off + kt, :], nt, preferred_element_type=F32)
            s_tiles.append(st)
            tmax = fold_lanes(st, jnp.maximum)
            mrun = tmax if mrun is None else jnp.maximum(mrun, tmax)
        return s_tiles, jnp.max(mrun, axis=-1, keepdims=True)

    def weights_and_values(h, r, s_tiles, m):
        lrun = None
        o = None
        for st, (_, vref, off, kt) in zip(s_tiles, key_tiles(h)):
            p = jnp.exp2(st - m)
            psum = fold_lanes(p, jnp.add)
            lrun = psum if lrun is None else lrun + psum
            ot = jnp.dot(p.astype(BF16), vref[off:off + kt, :], preferred_element_type=F32)
            o = ot if o is None else o + ot
        den = jnp.sum(lrun, axis=-1, keepdims=True)
        o_ref[r * ts:(r + 1) * ts, h * LANE:(h + 1) * LANE] = (o / den).astype(BF16)

    chains = [(h, r) for h in range(hb) for r in range(nsub)]
    ahead = min(ATTN_LOOKAHEAD, len(chains))
    pending = [scores(*c) for c in chains[:ahead]]
    for n, c in enumerate(chains):
        s_tiles, m = pending.pop(0)
        if n + ahead < len(chains):
            pending.append(scores(*chains[n + ahead]))
        weights_and_values(*c, s_tiles, m)


def _attention(q, k, v, kc, vc, tq):
    B, H, T, _ = q.shape
    S = k.shape[2]
    has_ctx = kc is not None
    hb = max(1, min(H, ATTN_STEP_ROWS // tq))
    in_specs = [
        pl.BlockSpec((None, hb, tq, 2 * LANE), lambda b, h, i: (b, h, i, 0)),
        pl.BlockSpec((None, hb, S, 2 * LANE), lambda b, h, i: (b, h, 0, 0)),
        pl.BlockSpec((None, hb, S, LANE), lambda b, h, i: (b, h, 0, 0)),
    ]
    args = [q, k, v]
    if has_ctx:
        P = kc.shape[2]
        in_specs += [
            pl.BlockSpec((None, hb, P, 2 * LANE), lambda b, h, i: (b, h, 0, 0)),
            pl.BlockSpec((None, hb, P, LANE), lambda b, h, i: (b, h, 0, 0)),
        ]
        args += [kc, vc]
    nsub = max(1, tq // ATTN_CHAIN_ROWS)
    return pl.pallas_call(
        functools.partial(_attn_kernel, has_ctx=has_ctx, nsub=nsub),
        grid=(B, H // hb, T // tq),
        in_specs=in_specs,
        out_specs=pl.BlockSpec((None, tq, hb * LANE), lambda b, h, i: (b, i, h)),
        out_shape=jax.ShapeDtypeStruct((B, T, H * LANE), BF16),
        compiler_params=_params(("arbitrary", "arbitrary", "arbitrary")),
        name="attention",
    )(*args)


def _shift_rows(x, k, row):
    n = x.shape[0]
    rolled = pltpu.roll(x, k % n, 0)
    valid = (row >= k) if k > 0 else (row < n + k)
    return jnp.where(valid, rolled, 0.0)


def _scan_chunk(a, b, carry, sub, reverse):
    n = a.shape[0]
    ng = n // SUBLANE
    hs = [None] * ng
    for j in (range(ng - 1, -1, -1) if reverse else range(ng)):
        aj = a[j * SUBLANE:(j + 1) * SUBLANE]
        bj = b[j * SUBLANE:(j + 1) * SUBLANE]
        d = 1
        while d < SUBLANE:
            shift = SUBLANE - d if reverse else d
            valid = (sub < SUBLANE - d) if reverse else (sub >= d)
            a_s = pltpu.roll(aj, shift, 0)
            b_s = pltpu.roll(bj, shift, 0)
            bj = jnp.where(valid, aj * b_s, 0.0) + bj
            aj = jnp.where(valid, aj * a_s, aj)
            d *= 2
        h = aj * carry + bj
        hs[j] = h
        carry = h[0:1] if reverse else h[SUBLANE - 1:SUBLANE]
    return jnp.concatenate(hs, axis=0), carry


def _sigmoid(x):
    return 0.5 * jnp.tanh(0.5 * x) + 0.5


def _softplus(x):
    return jnp.maximum(x, 0.0) + jnp.log(1.0 + jnp.exp(-jnp.abs(x)))


def _gelu_tanh(x):
    return 0.5 * x * (1.0 + jnp.tanh(0.7978845608028654 * (x + 0.044715 * x * x * x)))


def _lru_kernel(u_ref, ug_ref, cw_ref, cb_ref, w_ref, b_ref, lam_ref, h0_ref, y_ref, st_ref,
                xc_scr, hf_scr, hb_scr, *, tc):
    C = xc_scr.shape[1]
    for c in range(w_ref.shape[0]):
        cs = slice(c * C, (c + 1) * C)
        _lru_tile(u_ref.at[:, cs], ug_ref.at[:, cs], cw_ref.at[:, cs], cb_ref.at[:, cs], w_ref.at[c],
                  b_ref.at[c], lam_ref.at[:, cs], h0_ref.at[:, cs], y_ref.at[:, cs], st_ref.at[:, cs],
                  xc_scr, hf_scr, hb_scr, tc=tc)


def _lru_tile(u_ref, ug_ref, cw_ref, cb_ref, w_ref, b_ref, lam_ref, h0_ref, y_ref, st_ref,
              xc_scr, hf_scr, hb_scr, *, tc):
    T, C = xc_scr.shape
    nch = T // tc
    row_t = lax.broadcasted_iota(jnp.int32, (T, C), 0)
    u = u_ref[...]
    cw = cw_ref[...]
    xc = (cb_ref[...] + cw[0:1] * _shift_rows(u, 2, row_t) + cw[1:2] * _shift_rows(u, 1, row_t)
          + cw[2:3] * u + cw[3:4] * _shift_rows(u, -1, row_t))
    xc_scr[...] = xc

    sub = lax.broadcasted_iota(jnp.int32, (SUBLANE, C), 0)
    sp = _softplus(-lam_ref[...])
    h0 = h0_ref[...]

    def gates(xcj, d):
        g = jnp.dot(xcj.astype(BF16), w_ref[:, 2 * d * C:2 * (d + 1) * C], preferred_element_type=F32)
        g = g + b_ref[:, 2 * d * C:2 * (d + 1) * C]
        r = _sigmoid(g[:, :C])
        i = _sigmoid(g[:, C:])
        log_a = -LRU_C * r * sp[d:d + 1]
        a = jnp.exp(log_a)
        v = jnp.tanh(-log_a) * (a * a + 1.0)
        b = jnp.where(v > 0.0, v * lax.rsqrt(v), 0.0) * (i * xcj)
        return a, b

    def scan_dir(t0, carry, d):
        a, b = gates(xc_scr[pl.ds(t0, tc), :], d)
        return _scan_chunk(a, b, carry, sub, d == 1)

    def emit(t0, hf, hb):
        y = (hf + hb) * _gelu_tanh(ug_ref[pl.ds(t0, tc), :])
        y_ref[pl.ds(t0, tc), :] = y.astype(BF16)

    def meet(jj, carry):
        tf0 = pl.multiple_of(jj * tc, tc)
        tb0 = pl.multiple_of((nch - 1 - jj) * tc, tc)
        hf, cf = scan_dir(tf0, carry[0], 0)
        hb, cb = scan_dir(tb0, carry[1], 1)
        hf_scr[pl.ds(tf0, tc), :] = hf
        hb_scr[pl.ds(tb0, tc), :] = hb
        return cf, cb

    def part(jj, carry):
        tf0 = pl.multiple_of(jj * tc, tc)
        tb0 = pl.multiple_of((nch - 1 - jj) * tc, tc)
        hf, cf = scan_dir(tf0, carry[0], 0)
        hb, cb = scan_dir(tb0, carry[1], 1)
        emit(tf0, hf, hb_scr[pl.ds(tf0, tc), :])
        emit(tb0, hf_scr[pl.ds(tb0, tc), :], hb)
        return cf, cb

    carry = (h0[0:1], h0[1:2])
    if nch % 2 == 0:
        carry = lax.fori_loop(0, nch // 2, meet, carry)
        carry = lax.fori_loop(nch // 2, nch, part, carry)
    else:
        assert nch == 1
        hf, cf = scan_dir(0, carry[0], 0)
        hb, cb = scan_dir(0, carry[1], 1)
        emit(0, hf, hb)
        carry = (cf, cb)
    st_ref[0:1, :] = carry[0]
    st_ref[1:2, :] = carry[1]


def _lru_mixer(z3, h0, l, W, tc):
    B, T, _ = z3.shape
    C = MXU_TILE
    nct = LRU_WIDTH // C
    nc = max(1, min(nct, MIXER_STEP_ROWS // T))
    Cw = nc * C
    return pl.pallas_call(
        functools.partial(_lru_kernel, tc=tc),
        grid=(B, nct // nc),
        in_specs=[
            pl.BlockSpec((None, T, Cw), lambda b, c: (b, 0, OFF_LRU // Cw + c)),
            pl.BlockSpec((None, T, Cw), lambda b, c: (b, 0, OFF_GATE // Cw + c)),
            pl.BlockSpec((None, 4, Cw), lambda b, c: (l, 0, c)),
            pl.BlockSpec((None, 1, Cw), lambda b, c: (l, 0, c)),
            pl.BlockSpec((None, nc, C, 4 * C), lambda b, c: (l, c, 0, 0)),
            pl.BlockSpec((None, nc, 1, 4 * C), lambda b, c: (l, c, 0, 0)),
            pl.BlockSpec((None, 2, Cw), lambda b, c: (l, 0, c)),
            pl.BlockSpec((None, 2, Cw), lambda b, c: (b, 0, c)),
        ],
        out_specs=[
            pl.BlockSpec((None, T, Cw), lambda b, c: (b, 0, c)),
            pl.BlockSpec((None, 2, Cw), lambda b, c: (b, 0, c)),
        ],
        out_shape=[
            jax.ShapeDtypeStruct((B, T, LRU_WIDTH), BF16),
            jax.ShapeDtypeStruct((B, 2, LRU_WIDTH), F32),
        ],
        scratch_shapes=[pltpu.VMEM((T, C), F32)] * 3,
        compiler_params=_params(("arbitrary", "arbitrary")),
        name="lru_mixer",
    )(z3, z3, W["conv_w"], W["conv_b"], W["lru_w"], W["lru_b"], W["lru_lambda"], h0)


def _pool_group(u_ref, w_ref, sc_ref, o_ref, win):
    T, C = u_ref.shape
    row = lax.broadcasted_iota(jnp.int32, (T, C), 0)
    u = u_ref[...]
    s = u + _shift_rows(u, 1, row)
    w = 4
    while w <= win:
        q = w // 4
        s = _shift_rows(s, q, row) + _shift_rows(s, -q, row)
        w *= 2
    half = win // 2
    cnt = (jnp.minimum(row + half, T) - jnp.maximum(row - half, 0)).astype(F32)
    d = s / cnt - u
    y = jnp.dot(d.astype(BF16), w_ref[...], preferred_element_type=F32) * sc_ref[...]
    o_ref[...] = y.astype(BF16)


def _pool_kernel(u_ref, w_ref, sc_ref, o_ref):
    ng_step = w_ref.shape[0]
    C = w_ref.shape[1]
    if ng_step == len(POOL_WINDOWS):
        for gi, win in enumerate(POOL_WINDOWS):
            cs = slice(gi * C, (gi + 1) * C)
            _pool_group(u_ref.at[:, cs], w_ref.at[gi], sc_ref.at[:, cs], o_ref.at[:, cs], win)
    else:
        g = pl.program_id(1)
        for gi, win in enumerate(POOL_WINDOWS):
            @pl.when(g == gi)
            def _(win=win):
                _pool_group(u_ref, w_ref.at[0], sc_ref, o_ref, win)


def _pool_mixer(z3, l, W):
    B, T, _ = z3.shape
    C = MXU_TILE
    ng = POOL_WIDTH // C
    ngs = ng if ng * T <= MIXER_STEP_ROWS else 1
    Cw = ngs * C
    return pl.pallas_call(
        _pool_kernel,
        grid=(B, ng // ngs),
        in_specs=[
            pl.BlockSpec((None, T, Cw), lambda b, g: (b, 0, OFF_POOL // Cw + g)),
            pl.BlockSpec((None, ngs, C, C), lambda b, g: (l, g, 0, 0)),
            pl.BlockSpec((None, 1, Cw), lambda b, g: (l, 0, g)),
        ],
        out_specs=pl.BlockSpec((None, T, Cw), lambda b, g: (b, 0, g)),
        out_shape=jax.ShapeDtypeStruct((B, T, POOL_WIDTH), BF16),
        compiler_params=_params(("arbitrary", "arbitrary")),
        name="pool_mixer",
    )(z3, W["pool_w"], W["pool_scale"])


def _merge_kernel(a_ref, r_ref, p_ref, g0_ref, g1_ref, g2_ref, wa_ref, wr_ref, wp_ref, o_ref):
    ya = jnp.dot(a_ref[...], wa_ref[...], preferred_element_type=F32)
    yr = jnp.dot(r_ref[...], wr_ref[...], preferred_element_type=F32)
    yp = jnp.dot(p_ref[...], wp_ref[...], preferred_element_type=F32)
    m = g0_ref[...].astype(F32) * ya + g1_ref[...].astype(F32) * yr + g2_ref[...].astype(F32) * yp
    o_ref[...] = m.astype(BF16)


def _merge(attn, lru, pool, gates, l, W, tm):
    M, K = attn.shape
    D = W["w_mla_o"].shape[-1]
    tn = _pick(D, (1024, 512, 256))
    nbr = D // tn
    act = pl.BlockSpec((tm, K), lambda i, j: (i, 0))
    wsp = pl.BlockSpec((None, K, tn), lambda i, j: (l, 0, j))
    gsp = lambda k: pl.BlockSpec((tm, tn), lambda i, j: (i, k * nbr + j))
    return pl.pallas_call(
        _merge_kernel,
        grid=(M // tm, D // tn),
        in_specs=[act, act, act, gsp(0), gsp(1), gsp(2), wsp, wsp, wsp],
        out_specs=pl.BlockSpec((tm, tn), lambda i, j: (i, j)),
        out_shape=jax.ShapeDtypeStruct((M, D), BF16),
        compiler_params=_params(("arbitrary", "arbitrary")),
        name="merge",
    )(attn, lru, pool, gates, gates, gates, W["w_mla_o"], W["w_lru_o"], W["w_pool_o"])


def _outproj_kernel(m_ref, w_ref, x_ref, g_ref, o_ref):
    y = jnp.dot(m_ref[...], w_ref[...], preferred_element_type=F32)
    o_ref[...] = x_ref[...] + g_ref[...] * y


def _out_proj(merged, x2d, mod4, row_of, l, W, tm):
    M, D = x2d.shape
    tn = _pick(D, (1024, 512, 256))
    nbr = D // tn
    return pl.pallas_call(
        _outproj_kernel,
        grid=(M // tm, D // tn),
        in_specs=[
            pl.BlockSpec((tm, D), lambda i, j: (i, 0)),
            pl.BlockSpec((None, D, tn), lambda i, j: (l, 0, j)),
            pl.BlockSpec((tm, tn), lambda i, j: (i, j)),
            pl.BlockSpec((None, None, 1, tn), lambda i, j: (l, row_of(i * tm), 0, 2 * nbr + j)),
        ],
        out_specs=pl.BlockSpec((tm, tn), lambda i, j: (i, j)),
        out_shape=jax.ShapeDtypeStruct((M, D), F32),
        compiler_params=_params(("arbitrary", "arbitrary")),
        name="out_proj",
    )(merged, W["w_out"], x2d, mod4)


def _norm_mod(x, nw, sc, sh):
    ms = jnp.mean(x * x, axis=-1, keepdims=True)
    return (x * lax.rsqrt(ms + EPS) * (nw * (1.0 + sc)) + sh).astype(BF16)


def _ffn_kernel(x_ref, nw_ref, sh_ref, sc_ref, g_ref, w1_ref, w2_ref, o_ref, h_scr):
    f = pl.program_id(1)

    @pl.when(f == 0)
    def _():
        h_scr[...] = _norm_mod(x_ref[...], nw_ref[...], sc_ref[...], sh_ref[...])
        o_ref[...] = jnp.zeros_like(o_ref)

    a = jnp.dot(h_scr[...], w1_ref[...], preferred_element_type=F32)
    a = jnp.maximum(a, 0.0)
    o_ref[...] += jnp.dot((a * a).astype(BF16), w2_ref[...], preferred_element_type=F32)

    @pl.when(f == pl.num_programs(1) - 1)
    def _():
        o_ref[...] = x_ref[...] + g_ref[...] * o_ref[...]


def _ffn(x2d, mod4, row_of, l, W, tm):
    M, D = x2d.shape
    F = W["w_ff1"].shape[-1]
    tf = _pick(F, (FFN_HIDDEN_TILE, 256))
    modv = lambda k: pl.BlockSpec((None, None, 1, D), lambda i, f: (l, row_of(i * tm), 0, k))
    return pl.pallas_call(
        _ffn_kernel,
        grid=(M // tm, F // tf),
        in_specs=[
            pl.BlockSpec((tm, D), lambda i, f: (i, 0)),
            pl.BlockSpec((None, 1, D), lambda i, f: (l, 0, 0)),
            modv(3), modv(4), modv(5),
            pl.BlockSpec((None, D, tf), lambda i, f: (l, 0, f)),
            pl.BlockSpec((None, tf, D), lambda i, f: (l, f, 0)),
        ],
        out_specs=pl.BlockSpec((tm, D), lambda i, f: (i, 0)),
        out_shape=jax.ShapeDtypeStruct((M, D), F32),
        scratch_shapes=[pltpu.VMEM((tm, D), BF16)],
        compiler_params=_params(("arbitrary", "arbitrary")),
        name="ffn",
    )(x2d, W["norm2"], mod4, mod4, mod4, W["w_ff1"], W["w_ff2"])


def _pad_lanes(x, n=LANE):
    return jnp.pad(x, [(0, 0)] * (x.ndim - 1) + [(0, n - x.shape[-1])])


def _rot_half(x):
    h = x.shape[-1] // 2
    return jnp.concatenate([-x[..., h:], x[..., :h]], axis=-1)


def _swap_half(x):
    h = x.shape[-1] // 2
    return jnp.concatenate([x[..., h:], x[..., :h]], axis=-1)


def _norm_gains(g):
    gr = g[:, QK_NOPE:]
    return (g[:, None, :QK_NOPE], _pad_lanes(gr)[:, None, :], _pad_lanes(_swap_half(gr))[:, None, :])


def _block_diag_tiles(w, per):
    *lead, nb, s, _ = w.shape
    w = w.reshape(*lead, nb // per, per, s, s)
    t = jnp.einsum("...kij,km->...kimj", w, jnp.eye(per, dtype=w.dtype))
    return t.reshape(*lead, nb // per, per * s, per * s)


def _pack_weights(w_in, q_a_norm, w_qb, kv_a_norm, w_kvb, q_norm, k_norm, w_mla_o, conv_w, conv_b,
                  lru_wa, lru_ba, lru_wx, lru_bx, lru_lambda, w_lru_o, pool_w, pool_scale, w_pool_o,
                  w_out, w_ff1, w_ff2, norm1_w, norm2_w):
    L, D, _ = w_in.shape
    o_kr = Q_LORA + KV_LORA
    w_in = w_in.astype(BF16)
    wkr = w_in[..., o_kr:o_kr + QK_ROPE]
    w_in_p = jnp.concatenate(
        [w_in[..., :o_kr], _pad_lanes(wkr), _pad_lanes(_rot_half(wkr)), w_in[..., o_kr + QK_ROPE:]], axis=-1)

    wq = w_qb.reshape(L, Q_LORA, N_HEADS, QK_DIM)
    wq_rope = wq[..., QK_NOPE:]
    w_q = jnp.concatenate([wq[..., :QK_NOPE].reshape(L, Q_LORA, -1),
                           _pad_lanes(wq_rope).reshape(L, Q_LORA, -1)], axis=-1)
    w_qrot = _pad_lanes(_rot_half(wq_rope)).reshape(L, Q_LORA, -1)

    wkv = w_kvb.reshape(L, KV_LORA, N_HEADS, QK_NOPE + V_DIM)
    w_kv = jnp.concatenate([wkv[..., :QK_NOPE].reshape(L, KV_LORA, -1),
                            wkv[..., QK_NOPE:].reshape(L, KV_LORA, -1)], axis=-1)

    qg = _norm_gains(q_norm)
    kg = _norm_gains(k_norm)

    per = MXU_TILE // lru_wa.shape[-1]
    ta = _block_diag_tiles(lru_wa, per)
    tx = _block_diag_tiles(lru_wx, per)
    lru_w = jnp.concatenate([ta[:, 0], tx[:, 0], ta[:, 1], tx[:, 1]], axis=-1)
    nct = lru_w.shape[1]
    ba = lru_ba.reshape(L, 2, nct, 1, MXU_TILE)
    bx = lru_bx.reshape(L, 2, nct, 1, MXU_TILE)
    lru_b = jnp.concatenate([ba[:, 0], bx[:, 0], ba[:, 1], bx[:, 1]], axis=-1)

    return dict(
        w_in=w_in_p.astype(BF16), norm1=norm1_w[:, None, :], norm2=norm2_w[:, None, :],
        q_a_norm=q_a_norm[:, None, :], kv_a_norm=kv_a_norm[:, None, :],
        w_q=w_q.astype(BF16), w_qrot=w_qrot.astype(BF16), w_kv=w_kv.astype(BF16),
        qg_n=qg[0], qg_r=qg[1], qg_s=qg[2], kg_n=kg[0], kg_r=kg[1], kg_s=kg[2],
        w_mla_o=w_mla_o.astype(BF16), w_lru_o=w_lru_o.astype(BF16), w_pool_o=w_pool_o.astype(BF16),
        conv_w=conv_w, conv_b=conv_b[:, None, :], lru_w=lru_w.astype(BF16), lru_b=lru_b,
        lru_lambda=lru_lambda, pool_w=pool_w.astype(BF16), pool_scale=pool_scale[:, None, :],
        w_out=w_out.astype(BF16), w_ff1=w_ff1.astype(BF16), w_ff2=w_ff2.astype(BF16),
    )


def _rope_tables(T):
    rows = T // GRID_W
    row = jnp.repeat(jnp.arange(rows), GRID_W).astype(F32)
    col = jnp.tile(jnp.arange(GRID_W), rows).astype(F32)
    n_freq = QK_ROPE // 4
    inv = ROPE_BASE ** (-(jnp.arange(n_freq, dtype=F32) / n_freq))
    ang = jnp.concatenate([row[:, None] * inv, col[:, None] * inv], axis=-1)
    cos, sin = jnp.cos(ang), jnp.sin(ang)
    return (_pad_lanes(jnp.concatenate([cos, cos], axis=-1)), _pad_lanes(jnp.concatenate([sin, sin], axis=-1)))


def _trunk_layer(x2d, B, T, mod4, row_of, shared_cond, l, W, rope, ctx, h0):
    M = B * T
    span = M if shared_cond else T
    tm = _pick(span, (1024, 512, 256, 128))
    tmp = _pick(T, (512, 256, 128))
    z, gates = _in_proj(x2d, mod4, row_of, W["norm1"], W["w_in"], l, tm)
    z3 = z.reshape(B, T, -1)

    q, k, v, ckv = _qk_prep(z, B, T, l, W, rope, tmp)
    if ctx is None:
        kc = vc = None
    else:
        ckv_c, kr_c = ctx
        P = ckv_c.shape[2]
        kc, vc = _cache_k_prep(ckv_c, kr_c, B, P, l, W, _pick(P, (512, 256, 128)))
    attn = _attention(q, k, v, kc, vc, _pick(T, (ATTN_STEP_ROWS, 1024, 512, 256, 128)))

    lru, state = _lru_mixer(z3, h0, l, W, _pick(T, (256, 128)))
    pool = _pool_mixer(z3, l, W)

    merged = _merge(attn.reshape(M, -1), lru.reshape(M, -1), pool.reshape(M, -1), gates, l, W, tm)
    x2d = _out_proj(merged, x2d, mod4, row_of, l, W, tm)
    x2d = _ffn(x2d, mod4, row_of, l, W, _pick(span, (FFN_ROW_TILE, 512, 256, 128)))
    return x2d, z, ckv, state


def kernel(x_prompt, x_sample, c, cache_ckv, cache_krope, state_lru, c_ctx, w_mod, b_mod, norm1_w, norm2_w, w_in, q_a_norm, w_qb, kv_a_norm, w_kvb, q_norm, k_norm, w_mla_o, conv_w, conv_b, lru_wa, lru_ba, lru_wx, lru_bx, lru_lambda, w_lru_o, pool_w, pool_scale, w_pool_o, w_out, w_ff1, w_ff2):
    L = w_in.shape[0]
    Bc, Tc, D = x_prompt.shape
    Bs, Ts, _ = x_sample.shape

    W = _pack_weights(w_in, q_a_norm, w_qb, kv_a_norm, w_kvb, q_norm, k_norm, w_mla_o, conv_w, conv_b,
                      lru_wa, lru_ba, lru_wx, lru_bx, lru_lambda, w_lru_o, pool_w, pool_scale, w_pool_o,
                      w_out, w_ff1, w_ff2, norm1_w, norm2_w)

    rows = Bs + 1
    rpad = -rows % 8
    cond = jnp.concatenate([c, c_ctx[None, :], jnp.zeros((rpad, D), F32)], axis=0)
    mod = _modulation(cond, w_mod, b_mod)
    mod4 = mod.reshape(L, rows + rpad, 1, N_MOD * D)

    rope = _rope_tables(Ts)
    kr_cache = _pad_lanes(cache_krope)
    zero_state = jnp.zeros((Bc, 2, LRU_WIDTH), F32)

    row_smp = lambda t: t // Ts
    row_ctx = lambda t: Bs

    xp = x_prompt.reshape(Bc * Tc, D)
    xs = x_sample.reshape(Bs * Ts, D)
    ckv_list, kr_list, st_list = [], [], []
    for l in range(L):
        xp, zc, ckv, st = _trunk_layer(xp, Bc, Tc, mod4, row_ctx, True, l, W, None, None, zero_state)
        ckv_list.append(ckv.reshape(Bc, Tc, KV_LORA))
        kr_list.append(zc[:, OFF_KR:OFF_KR + QK_ROPE].reshape(Bc, Tc, QK_ROPE))
        st_list.append(st)
        xs, _, _, _ = _trunk_layer(xs, Bs, Ts, mod4, row_smp, False, l, W, rope,
                                   (cache_ckv, kr_cache), state_lru[:, l])
    return (xp.reshape(Bc, Tc, D), xs.reshape(Bs, Ts, D), jnp.stack(ckv_list, axis=1),
            jnp.stack(kr_list, axis=1), jnp.stack(st_list, axis=1))
```

```python
import functools

import jax
import jax.numpy as jnp
from jax import lax
from jax.experimental import pallas as pl
from jax.experimental.pallas import tpu as pltpu

F32 = jnp.float32
BF16 = jnp.bfloat16

EPS = 1e-6
GRID_W = 64
N_HEADS = 8
QK_NOPE = 128
QK_ROPE = 64
V_DIM = 128
QK_DIM = QK_NOPE + QK_ROPE
Q_LORA = 512
KV_LORA = 256
ROPE_BASE = 10000.0
LRU_WIDTH = 1024
LRU_C = 8.0
POOL_WIDTH = 1024
POOL_WINDOWS = (2, 4, 8, 16)
N_MOD = 6

LANE = 128
SUBLANE = 8
MXU_TILE = 256
VMEM_LIMIT = 56 * 2**20
MIXER_STEP_ROWS = 1024
FFN_ROW_TILE = 512
FFN_HIDDEN_TILE = 1024
ATTN_STEP_ROWS = 4096
ATTN_CHAIN_ROWS = 256
ATTN_LOOKAHEAD = 2
LOG2E = 1.4426950408889634
Q_PRESCALE = QK_DIM ** -0.5 * LOG2E

OFF_Q = 0
OFF_KV = Q_LORA
OFF_KR = OFF_KV + KV_LORA
OFF_KROT = OFF_KR + LANE
OFF_LRU = OFF_KROT + LANE
OFF_GATE = OFF_LRU + LRU_WIDTH
OFF_POOL = OFF_GATE + LRU_WIDTH
OFF_BR = OFF_POOL + POOL_WIDTH


def _params(sem):
    return pltpu.CompilerParams(dimension_semantics=sem, vmem_limit_bytes=VMEM_LIMIT)


def _pick(n, prefs):
    for p in prefs:
        if n % p == 0:
            return p
    return n


def _mod_kernel(c_ref, w_ref, b_ref, o_ref):
    c = c_ref[...]
    s = c * jax.nn.sigmoid(c)
    o_ref[...] = jnp.dot(s.astype(BF16), w_ref[...].astype(BF16), preferred_element_type=F32) + b_ref[...]


def _modulation(cond, w_mod, b_mod):
    L, D, N = w_mod.shape
    R = cond.shape[0]
    tn = _pick(N, (1024, 512, 256, 128))
    return pl.pallas_call(
        _mod_kernel,
        grid=(L, N // tn),
        in_specs=[
            pl.BlockSpec((R, D), lambda l, j: (0, 0)),
            pl.BlockSpec((None, D, tn), lambda l, j: (l, 0, j)),
            pl.BlockSpec((None, 1, tn), lambda l, j: (l, 0, j)),
        ],
        out_specs=pl.BlockSpec((None, R, tn), lambda l, j: (l, 0, j)),
        out_shape=jax.ShapeDtypeStruct((L, R, N), F32),
        compiler_params=_params(("arbitrary", "arbitrary")),
        name="modulation",
    )(cond, w_mod, b_mod.reshape(L, 1, N))


def _inproj_kernel(x_ref, nw_ref, sh_ref, sc_ref, w_ref, z_ref, g_ref, h_scr, *, n_lin):
    j = pl.program_id(1)

    @pl.when(j == 0)
    def _():
        h_scr[...] = _norm_mod(x_ref[...], nw_ref[...], sc_ref[...], sh_ref[...])

    @pl.when(j < n_lin)
    def _():
        z_ref[...] = jnp.dot(h_scr[...], w_ref[...], preferred_element_type=F32)

    @pl.when(j >= n_lin)
    def _():
        g_ref[...] = _sigmoid(jnp.dot(h_scr[...], w_ref[...], preferred_element_type=F32)).astype(BF16)


def _in_proj(x2d, mod4, row_of, norm_w, w_in_p, l, tm):
    M, D = x2d.shape
    N = w_in_p.shape[-1]
    tn = _pick(OFF_BR, (1024, 512, 256))
    n_lin = OFF_BR // tn
    return pl.pallas_call(
        functools.partial(_inproj_kernel, n_lin=n_lin),
        grid=(M // tm, N // tn),
        in_specs=[
            pl.BlockSpec((tm, D), lambda i, j: (i, 0)),
            pl.BlockSpec((None, 1, D), lambda i, j: (l, 0, 0)),
            pl.BlockSpec((None, None, 1, D), lambda i, j: (l, row_of(i * tm), 0, 0)),
            pl.BlockSpec((None, None, 1, D), lambda i, j: (l, row_of(i * tm), 0, 1)),
            pl.BlockSpec((None, D, tn), lambda i, j: (l, 0, j)),
        ],
        out_specs=[
            pl.BlockSpec((tm, tn), lambda i, j: (i, jnp.minimum(j, n_lin - 1))),
            pl.BlockSpec((tm, tn), lambda i, j: (i, jnp.maximum(j - n_lin, 0))),
        ],
        out_shape=[
            jax.ShapeDtypeStruct((M, OFF_BR), F32),
            jax.ShapeDtypeStruct((M, N - OFF_BR), BF16),
        ],
        scratch_shapes=[pltpu.VMEM((tm, D), BF16)],
        compiler_params=_params(("arbitrary", "arbitrary")),
        name="in_proj",
    )(x2d, norm_w, mod4, mod4, w_in_p)


def _head_scale(nope, rope_sq_sum):
    ss = jnp.sum(nope * nope, axis=-1, keepdims=True) + rope_sq_sum
    return lax.rsqrt(ss * (1.0 / QK_DIM) + EPS)


def _qprep_kernel(*refs, use_rope):
    if use_rope:
        ql_ref, an_ref, w_ref, wrot_ref, gn_ref, gr_ref, gs_ref, cos_ref, sin_ref, q_ref = refs
    else:
        ql_ref, an_ref, w_ref, gn_ref, gr_ref, q_ref = refs
    ql = ql_ref[...]
    ms = jnp.mean(ql * ql, axis=-1, keepdims=True)
    qn = (ql * lax.rsqrt(ms + EPS) * an_ref[...]).astype(BF16)
    qq = jnp.dot(qn, w_ref[...], preferred_element_type=F32)
    if use_rope:
        qrot = jnp.dot(qn, wrot_ref[...], preferred_element_type=F32)
        cos = cos_ref[...]
        sin = sin_ref[...]
        gs = gs_ref[...]
    gn = gn_ref[...]
    gr = gr_ref[...]
    hw = N_HEADS * LANE
    for h in range(N_HEADS):
        nope = qq[:, h * LANE:(h + 1) * LANE]
        rope = qq[:, hw + h * LANE:hw + (h + 1) * LANE]
        s = _head_scale(nope, jnp.sum(rope * rope, axis=-1, keepdims=True))
        if use_rope:
            r = gr * rope * cos + gs * qrot[:, h * LANE:(h + 1) * LANE] * sin
        else:
            r = gr * rope
        s = s * Q_PRESCALE
        q_ref[h, :, 0:LANE] = (nope * s * gn).astype(BF16)
        q_ref[h, :, LANE:2 * LANE] = (r * s).astype(BF16)


def _q_prep_io(z, B, T, l, W, rope, tm):
    nb = T // tm
    use_rope = rope is not None
    wspec = lambda n: pl.BlockSpec((None, Q_LORA, n), lambda i: (l, 0, 0))
    vec = pl.BlockSpec((None, 1, LANE), lambda i: (l, 0, 0))
    in_specs = [
        pl.BlockSpec((tm, Q_LORA), lambda i: (i, OFF_Q // Q_LORA)),
        pl.BlockSpec((None, 1, Q_LORA), lambda i: (l, 0, 0)),
        wspec(2 * N_HEADS * LANE),
    ]
    args = [z, W["q_a_norm"], W["w_q"]]
    if use_rope:
        tab = pl.BlockSpec((tm, LANE), lambda i: (i % nb, 0))
        in_specs += [wspec(N_HEADS * LANE), vec, vec, vec, tab, tab]
        args += [W["w_qrot"], W["qg_n"], W["qg_r"], W["qg_s"], rope[0], rope[1]]
    else:
        in_specs += [vec, vec]
        args += [W["qg_n"], W["qg_r"]]
    out_specs = [pl.BlockSpec((None, N_HEADS, tm, 2 * LANE), lambda i: (i // nb, 0, i % nb, 0))]
    out_shape = [jax.ShapeDtypeStruct((B, N_HEADS, T, 2 * LANE), BF16)]
    return in_specs, args, out_specs, out_shape


def _kprep_kernel(*refs, use_rope, normalize):
    refs = list(refs)
    kv_ref = refs.pop(0)
    kr_ref = refs.pop(0)
    krot_ref = refs.pop(0) if use_rope else None
    an_ref = refs.pop(0) if normalize else None
    w_ref = refs.pop(0)
    gn_ref = refs.pop(0)
    gr_ref = refs.pop(0)
    if use_rope:
        gs_ref, cos_ref, sin_ref = refs.pop(0), refs.pop(0), refs.pop(0)
    k_ref = refs.pop(0)
    v_ref = refs.pop(0)
    ckv_ref = refs.pop(0) if normalize else None

    kv = kv_ref[...]
    if normalize:
        ms = jnp.mean(kv * kv, axis=-1, keepdims=True)
        kv = kv * lax.rsqrt(ms + EPS) * an_ref[...]
        ckv_ref[...] = kv
    kk = jnp.dot(kv.astype(BF16), w_ref[...], preferred_element_type=F32)
    kr = kr_ref[...]
    ssr = jnp.sum(kr * kr, axis=-1, keepdims=True)
    if use_rope:
        base = gr_ref[...] * kr * cos_ref[...] + gs_ref[...] * krot_ref[...] * sin_ref[...]
    else:
        base = gr_ref[...] * kr
    gn = gn_ref[...]
    hw = N_HEADS * LANE
    for h in range(N_HEADS):
        nope = kk[:, h * LANE:(h + 1) * LANE]
        s = _head_scale(nope, ssr)
        k_ref[h, :, 0:LANE] = (nope * s * gn).astype(BF16)
        k_ref[h, :, LANE:2 * LANE] = (base * s).astype(BF16)
        v_ref[h] = kk[:, hw + h * LANE:hw + (h + 1) * LANE].astype(BF16)


def _k_prep_io(kv_src, kr_src, B, T, l, W, rope, tm, *, kv_spec, kr_spec, krot_spec, normalize):
    M = B * T
    nb = T // tm
    use_rope = rope is not None
    vec = pl.BlockSpec((None, 1, LANE), lambda i: (l, 0, 0))
    in_specs = [kv_spec, kr_spec]
    args = [kv_src, kr_src]
    if use_rope:
        in_specs.append(krot_spec)
        args.append(kr_src)
    if normalize:
        in_specs.append(pl.BlockSpec((None, 1, KV_LORA), lambda i: (l, 0, 0)))
        args.append(W["kv_a_norm"])
    in_specs += [pl.BlockSpec((None, KV_LORA, 2 * N_HEADS * LANE), lambda i: (l, 0, 0)), vec, vec]
    args += [W["w_kv"], W["kg_n"], W["kg_r"]]
    if use_rope:
        tab = pl.BlockSpec((tm, LANE), lambda i: (i % nb, 0))
        in_specs += [vec, tab, tab]
        args += [W["kg_s"], rope[0], rope[1]]
    out_specs = [
        pl.BlockSpec((None, N_HEADS, tm, 2 * LANE), lambda i: (i // nb, 0, i % nb, 0)),
        pl.BlockSpec((None, N_HEADS, tm, LANE), lambda i: (i // nb, 0, i % nb, 0)),
    ]
    out_shape = [
        jax.ShapeDtypeStruct((B, N_HEADS, T, 2 * LANE), BF16),
        jax.ShapeDtypeStruct((B, N_HEADS, T, LANE), BF16),
    ]
    if normalize:
        out_specs.append(pl.BlockSpec((tm, KV_LORA), lambda i: (i, 0)))
        out_shape.append(jax.ShapeDtypeStruct((M, KV_LORA), F32))
    return in_specs, args, out_specs, out_shape


def _cache_k_prep(ckv_c, kr_c, B, P, l, W, tp):
    npb = P // tp
    in_specs, args, out_specs, out_shape = _k_prep_io(
        ckv_c, kr_c, B, P, l, W, None, tp,
        kv_spec=pl.BlockSpec((None, None, tp, KV_LORA), lambda i: (i // npb, l, i % npb, 0)),
        kr_spec=pl.BlockSpec((None, None, tp, LANE), lambda i: (i // npb, l, i % npb, 0)),
        krot_spec=None, normalize=False)
    return pl.pallas_call(
        functools.partial(_kprep_kernel, use_rope=False, normalize=False),
        grid=(B * P // tp,),
        in_specs=in_specs,
        out_specs=out_specs,
        out_shape=out_shape,
        compiler_params=_params(("arbitrary",)),
        name="k_prep",
    )(*args)


def _qkprep_kernel(*refs, use_rope, n_q_in, n_k_in):
    q_in, k_in = refs[:n_q_in], refs[n_q_in:n_q_in + n_k_in]
    q_out, k_out = refs[n_q_in + n_k_in], refs[n_q_in + n_k_in + 1:]
    _qprep_kernel(*q_in, q_out, use_rope=use_rope)
    _kprep_kernel(*k_in, *k_out, use_rope=use_rope, normalize=True)


def _qk_prep(z, B, T, l, W, rope, tm):
    q_io = _q_prep_io(z, B, T, l, W, rope, tm)
    k_io = _k_prep_io(
        z, z, B, T, l, W, rope, tm,
        kv_spec=pl.BlockSpec((tm, KV_LORA), lambda i: (i, OFF_KV // KV_LORA)),
        kr_spec=pl.BlockSpec((tm, LANE), lambda i: (i, OFF_KR // LANE)),
        krot_spec=pl.BlockSpec((tm, LANE), lambda i: (i, OFF_KROT // LANE)),
        normalize=True)
    return pl.pallas_call(
        functools.partial(_qkprep_kernel, use_rope=rope is not None, n_q_in=len(q_io[1]), n_k_in=len(k_io[1])),
        grid=(B * T // tm,),
        in_specs=q_io[0] + k_io[0],
        out_specs=q_io[2] + k_io[2],
        out_shape=q_io[3] + k_io[3],
        compiler_params=_params(("arbitrary",)),
        name="qk_prep",
    )(*q_io[1], *k_io[1])


def _attn_kernel(*refs, has_ctx, nsub):
    if has_ctx:
        q_ref, k_ref, v_ref, kc_ref, vc_ref, o_ref = refs
    else:
        q_ref, k_ref, v_ref, o_ref = refs
    nt = (((1,), (1,)), ((), ()))
    hb, tq, _ = q_ref.shape
    ts = tq // nsub

    def key_tiles(h):
        out = []
        for kref, vref in ((k_ref, v_ref), (kc_ref, vc_ref)) if has_ctx else ((k_ref, v_ref),):
            kt = min(MXU_TILE, kref.shape[1])
            out += [(kref.at[h], vref.at[h], t * kt, kt) for t in range(kref.shape[1] // kt)]
        return out

    def fold_lanes(x, op):
        out = x[:, :LANE]
        for c in range(1, x.shape[1] // LANE):
            out = op(out, x[:, c * LANE:(c + 1) * LANE])
        return out

    def scores(h, r):
        q = q_ref[h, r * ts:(r + 1) * ts, :]
        s_tiles = []
        mrun = None
        for kref, _, off, kt in key_tiles(h):
            st = lax.dot_general(q, kref[off:off + kt, :], nt, preferred_element_type=F32)
            s_tiles.append(st)
            tmax = fold_lanes(st, jnp.maximum)
            mrun = tmax if mrun is None else jnp.maximum(mrun, tmax)
        return s_tiles, jnp.max(mrun, axis=-1, keepdims=True)

    def weights_and_values(h, r, s_tiles, m):
        lrun = None
        o = None
        for st, (_, vref, off, kt) in zip(s_tiles, key_tiles(h)):
            p = jnp.exp2(st - m)
            psum = fold_lanes(p, jnp.add)
            lrun = psum if lrun is None else lrun + psum
            ot = jnp.dot(p.astype(BF16), vref[off:off + kt, :], preferred_element_type=F32)
            o = ot if o is None else o + ot
        den = jnp.sum(lrun, axis=-1, keepdims=True)
        o_ref[r * ts:(r + 1) * ts, h * LANE:(h + 1) * LANE] = (o / den).astype(BF16)

    chains = [(h, r) for h in range(hb) for r in range(nsub)]
    ahead = min(ATTN_LOOKAHEAD, len(chains))
    pending = [scores(*c) for c in chains[:ahead]]
    for n, c in enumerate(chains):
        s_tiles, m = pending.pop(0)
        if n + ahead < len(chains):
            pending.append(scores(*chains[n + ahead]))
        weights_and_values(*c, s_tiles, m)


def _attention(q, k, v, kc, vc, tq):
    B, H, T, _ = q.shape
    S = k.shape[2]
    has_ctx = kc is not None
    hb = max(1, min(H, ATTN_STEP_ROWS // tq))
    in_specs = [
        pl.BlockSpec((None, hb, tq, 2 * LANE), lambda b, h, i: (b, h, i, 0)),
        pl.BlockSpec((None, hb, S, 2 * LANE), lambda b, h, i: (b, h, 0, 0)),
        pl.BlockSpec((None, hb, S, LANE), lambda b, h, i: (b, h, 0, 0)),
    ]
    args = [q, k, v]
    if has_ctx:
        P = kc.shape[2]
        in_specs += [
            pl.BlockSpec((None, hb, P, 2 * LANE), lambda b, h, i: (b, h, 0, 0)),
            pl.BlockSpec((None, hb, P, LANE), lambda b, h, i: (b, h, 0, 0)),
        ]
        args += [kc, vc]
    nsub = max(1, tq // ATTN_CHAIN_ROWS)
    return pl.pallas_call(
        functools.partial(_attn_kernel, has_ctx=has_ctx, nsub=nsub),
        grid=(B, H // hb, T // tq),
        in_specs=in_specs,
        out_specs=pl.BlockSpec((None, tq, hb * LANE), lambda b, h, i: (b, i, h)),
        out_shape=jax.ShapeDtypeStruct((B, T, H * LANE), BF16),
        compiler_params=_params(("arbitrary", "arbitrary", "arbitrary")),
        name="attention",
    )(*args)


def _shift_rows(x, k, sub):
    n = x.shape[0]
    rolled = pltpu.roll(x, k % n, 0)
    if k > 0:
        head = jnp.where(sub >= k, rolled[:SUBLANE], 0.0)
        return jnp.concatenate([head, rolled[SUBLANE:]], axis=0)
    tail = jnp.where(sub < SUBLANE + k, rolled[n - SUBLANE:], 0.0)
    return jnp.concatenate([rolled[:n - SUBLANE], tail], axis=0)


def _scan_chunk(a, b, carry, sub, reverse):
    n = a.shape[0]
    ng = n // SUBLANE
    hs = [None] * ng
    for j in (range(ng - 1, -1, -1) if reverse else range(ng)):
        aj = a[j * SUBLANE:(j + 1) * SUBLANE]
        bj = b[j * SUBLANE:(j + 1) * SUBLANE]
        d = 1
        while d < SUBLANE:
            shift = SUBLANE - d if reverse else d
            valid = (sub < SUBLANE - d) if reverse else (sub >= d)
            a_s = pltpu.roll(aj, shift, 0)
            b_s = pltpu.roll(bj, shift, 0)
            bj = jnp.where(valid, aj * b_s, 0.0) + bj
            aj = jnp.where(valid, aj * a_s, aj)
            d *= 2
        h = aj * carry + bj
        hs[j] = h
        carry = h[0:1] if reverse else h[SUBLANE - 1:SUBLANE]
    return jnp.concatenate(hs, axis=0), carry


def _sigmoid(x):
    return 0.5 * jnp.tanh(0.5 * x) + 0.5


def _softplus(x):
    return jnp.maximum(x, 0.0) + jnp.log(1.0 + jnp.exp(-jnp.abs(x)))


def _gelu_tanh(x):
    return 0.5 * x * (1.0 + jnp.tanh(0.7978845608028654 * (x + 0.044715 * x * x * x)))


def _lru_kernel(u_ref, ug_ref, cw_ref, cb_ref, w_ref, b_ref, lam_ref, h0_ref, y_ref, st_ref,
                xc_scr, hf_scr, hb_scr, *, tc):
    C = xc_scr.shape[1]
    for c in range(w_ref.shape[0]):
        cs = slice(c * C, (c + 1) * C)
        _lru_tile(u_ref.at[:, cs], ug_ref.at[:, cs], cw_ref.at[:, cs], cb_ref.at[:, cs], w_ref.at[c],
                  b_ref.at[c], lam_ref.at[:, cs], h0_ref.at[:, cs], y_ref.at[:, cs], st_ref.at[:, cs],
                  xc_scr, hf_scr, hb_scr, tc=tc)


def _lru_tile(u_ref, ug_ref, cw_ref, cb_ref, w_ref, b_ref, lam_ref, h0_ref, y_ref, st_ref,
              xc_scr, hf_scr, hb_scr, *, tc):
    T, C = xc_scr.shape
    nch = T // tc
    sub = lax.broadcasted_iota(jnp.int32, (SUBLANE, C), 0)
    u = u_ref[...]
    cw = cw_ref[...]
    xc = (cb_ref[...] + cw[0:1] * _shift_rows(u, 2, sub) + cw[1:2] * _shift_rows(u, 1, sub)
          + cw[2:3] * u + cw[3:4] * _shift_rows(u, -1, sub))
    xc_scr[...] = xc

    decay = LRU_C * _softplus(-lam_ref[...])
    h0 = h0_ref[...]

    def gates(xcj, d):
        g = jnp.dot(xcj.astype(BF16), w_ref[:, 2 * d * C:2 * (d + 1) * C], preferred_element_type=F32)
        g = g + b_ref[:, 2 * d * C:2 * (d + 1) * C]
        r = _sigmoid(g[:, :C])
        i = _sigmoid(g[:, C:])
        nla = r * decay[d:d + 1]
        a = jnp.exp2(nla * (-LOG2E))
        v = jnp.tanh(nla) * (a * a + 1.0)
        b = jnp.where(v > 0.0, v * lax.rsqrt(v), 0.0) * (i * xcj)
        return a, b

    def scan_dir(t0, carry, d):
        a, b = gates(xc_scr[pl.ds(t0, tc), :], d)
        return _scan_chunk(a, b, carry, sub, d == 1)

    def emit(t0, hf, hb):
        y = (hf + hb) * _gelu_tanh(ug_ref[pl.ds(t0, tc), :])
        y_ref[pl.ds(t0, tc), :] = y.astype(BF16)

    def meet(jj, carry):
        tf0 = pl.multiple_of(jj * tc, tc)
        tb0 = pl.multiple_of((nch - 1 - jj) * tc, tc)
        hf, cf = scan_dir(tf0, carry[0], 0)
        hb, cb = scan_dir(tb0, carry[1], 1)
        hf_scr[pl.ds(tf0, tc), :] = hf
        hb_scr[pl.ds(tb0, tc), :] = hb
        return cf, cb

    def part(jj, carry):
        tf0 = pl.multiple_of(jj * tc, tc)
        tb0 = pl.multiple_of((nch - 1 - jj) * tc, tc)
        hf, cf = scan_dir(tf0, carry[0], 0)
        hb, cb = scan_dir(tb0, carry[1], 1)
        emit(tf0, hf, hb_scr[pl.ds(tf0, tc), :])
        emit(tb0, hf_scr[pl.ds(tb0, tc), :], hb)
        return cf, cb

    carry = (h0[0:1], h0[1:2])
    if nch % 2 == 0:
        carry = lax.fori_loop(0, nch // 2, meet, carry)
        carry = lax.fori_loop(nch // 2, nch, part, carry)
    else:
        assert nch == 1
        hf, cf = scan_dir(0, carry[0], 0)
        hb, cb = scan_dir(0, carry[1], 1)
        emit(0, hf, hb)
        carry = (cf, cb)
    st_ref[0:1, :] = carry[0]
    st_ref[1:2, :] = carry[1]


def _lru_mixer(z3, h0, l, W, tc):
    B, T, _ = z3.shape
    C = MXU_TILE
    nct = LRU_WIDTH // C
    nc = max(1, min(nct, MIXER_STEP_ROWS // T))
    Cw = nc * C
    return pl.pallas_call(
        functools.partial(_lru_kernel, tc=tc),
        grid=(B, nct // nc),
        in_specs=[
            pl.BlockSpec((None, T, Cw), lambda b, c: (b, 0, OFF_LRU // Cw + c)),
            pl.BlockSpec((None, T, Cw), lambda b, c: (b, 0, OFF_GATE // Cw + c)),
            pl.BlockSpec((None, 4, Cw), lambda b, c: (l, 0, c)),
            pl.BlockSpec((None, 1, Cw), lambda b, c: (l, 0, c)),
            pl.BlockSpec((None, nc, C, 4 * C), lambda b, c: (l, c, 0, 0)),
            pl.BlockSpec((None, nc, 1, 4 * C), lambda b, c: (l, c, 0, 0)),
            pl.BlockSpec((None, 2, Cw), lambda b, c: (l, 0, c)),
            pl.BlockSpec((None, 2, Cw), lambda b, c: (b, 0, c)),
        ],
        out_specs=[
            pl.BlockSpec((None, T, Cw), lambda b, c: (b, 0, c)),
            pl.BlockSpec((None, 2, Cw), lambda b, c: (b, 0, c)),
        ],
        out_shape=[
            jax.ShapeDtypeStruct((B, T, LRU_WIDTH), BF16),
            jax.ShapeDtypeStruct((B, 2, LRU_WIDTH), F32),
        ],
        scratch_shapes=[pltpu.VMEM((T, C), F32)] * 3,
        compiler_params=_params(("arbitrary", "arbitrary")),
        name="lru_mixer",
    )(z3, z3, W["conv_w"], W["conv_b"], W["lru_w"], W["lru_b"], W["lru_lambda"], h0)


def _pool_group(u_ref, w_ref, sc_ref, o_ref, win):
    T, C = u_ref.shape
    sub = lax.broadcasted_iota(jnp.int32, (SUBLANE, C), 0)
    u = u_ref[...]
    s = u + _shift_rows(u, 1, sub)
    w = 4
    while w <= win:
        q = w // 4
        s = _shift_rows(s, q, sub) + _shift_rows(s, -q, sub)
        w *= 2
    half = win // 2
    assert half <= SUBLANE and T >= 2 * SUBLANE
    head_cnt = (jnp.minimum(sub + half, T) - jnp.maximum(sub - half, 0)).astype(F32)
    tail_row = sub + (T - SUBLANE)
    tail_cnt = (jnp.minimum(tail_row + half, T) - jnp.maximum(tail_row - half, 0)).astype(F32)
    mean = jnp.concatenate([s[:SUBLANE] / head_cnt, s[SUBLANE:T - SUBLANE] * (1.0 / win),
                            s[T - SUBLANE:] / tail_cnt], axis=0)
    d = mean - u
    y = jnp.dot(d.astype(BF16), w_ref[...], preferred_element_type=F32) * sc_ref[...]
    o_ref[...] = y.astype(BF16)


def _pool_kernel(u_ref, w_ref, sc_ref, o_ref):
    ng_step = w_ref.shape[0]
    C = w_ref.shape[1]
    if ng_step == len(POOL_WINDOWS):
        for gi, win in enumerate(POOL_WINDOWS):
            cs = slice(gi * C, (gi + 1) * C)
            _pool_group(u_ref.at[:, cs], w_ref.at[gi], sc_ref.at[:, cs], o_ref.at[:, cs], win)
    else:
        g = pl.program_id(1)
        for gi, win in enumerate(POOL_WINDOWS):
            @pl.when(g == gi)
            def _(win=win):
                _pool_group(u_ref, w_ref.at[0], sc_ref, o_ref, win)


def _pool_mixer(z3, l, W):
    B, T, _ = z3.shape
    C = MXU_TILE
    ng = POOL_WIDTH // C
    ngs = ng if ng * T <= MIXER_STEP_ROWS else 1
    Cw = ngs * C
    return pl.pallas_call(
        _pool_kernel,
        grid=(B, ng // ngs),
        in_specs=[
            pl.BlockSpec((None, T, Cw), lambda b, g: (b, 0, OFF_POOL // Cw + g)),
            pl.BlockSpec((None, ngs, C, C), lambda b, g: (l, g, 0, 0)),
            pl.BlockSpec((None, 1, Cw), lambda b, g: (l, 0, g)),
        ],
        out_specs=pl.BlockSpec((None, T, Cw), lambda b, g: (b, 0, g)),
        out_shape=jax.ShapeDtypeStruct((B, T, POOL_WIDTH), BF16),
        compiler_params=_params(("arbitrary", "arbitrary")),
        name="pool_mixer",
    )(z3, W["pool_w"], W["pool_scale"])


def _merge_kernel(a_ref, r_ref, p_ref, g0_ref, g1_ref, g2_ref, wa_ref, wr_ref, wp_ref, o_ref):
    ya = jnp.dot(a_ref[...], wa_ref[...], preferred_element_type=F32)
    yr = jnp.dot(r_ref[...], wr_ref[...], preferred_element_type=F32)
    yp = jnp.dot(p_ref[...], wp_ref[...], preferred_element_type=F32)
    m = g0_ref[...].astype(F32) * ya + g1_ref[...].astype(F32) * yr + g2_ref[...].astype(F32) * yp
    o_ref[...] = m.astype(BF16)


def _merge(attn, lru, pool, gates, l, W, tm):
    M, K = attn.shape
    D = W["w_mla_o"].shape[-1]
    tn = _pick(D, (1024, 512, 256))
    nbr = D // tn
    act = pl.BlockSpec((tm, K), lambda i, j: (i, 0))
    wsp = pl.BlockSpec((None, K, tn), lambda i, j: (l, 0, j))
    gsp = lambda k: pl.BlockSpec((tm, tn), lambda i, j: (i, k * nbr + j))
    return pl.pallas_call(
        _merge_kernel,
        grid=(M // tm, D // tn),
        in_specs=[act, act, act, gsp(0), gsp(1), gsp(2), wsp, wsp, wsp],
        out_specs=pl.BlockSpec((tm, tn), lambda i, j: (i, j)),
        out_shape=jax.ShapeDtypeStruct((M, D), BF16),
        compiler_params=_params(("arbitrary", "arbitrary")),
        name="merge",
    )(attn, lru, pool, gates, gates, gates, W["w_mla_o"], W["w_lru_o"], W["w_pool_o"])


def _outproj_kernel(m_ref, w_ref, x_ref, g_ref, o_ref):
    y = jnp.dot(m_ref[...], w_ref[...], preferred_element_type=F32)
    o_ref[...] = x_ref[...] + g_ref[...] * y


def _out_proj(merged, x2d, mod4, row_of, l, W, tm):
    M, D = x2d.shape
    tn = _pick(D, (1024, 512, 256))
    nbr = D // tn
    return pl.pallas_call(
        _outproj_kernel,
        grid=(M // tm, D // tn),
        in_specs=[
            pl.BlockSpec((tm, D), lambda i, j: (i, 0)),
            pl.BlockSpec((None, D, tn), lambda i, j: (l, 0, j)),
            pl.BlockSpec((tm, tn), lambda i, j: (i, j)),
            pl.BlockSpec((None, None, 1, tn), lambda i, j: (l, row_of(i * tm), 0, 2 * nbr + j)),
        ],
        out_specs=pl.BlockSpec((tm, tn), lambda i, j: (i, j)),
        out_shape=jax.ShapeDtypeStruct((M, D), F32),
        compiler_params=_params(("arbitrary", "arbitrary")),
        name="out_proj",
    )(merged, W["w_out"], x2d, mod4)


def _norm_mod(x, nw, sc, sh):
    ms = jnp.mean(x * x, axis=-1, keepdims=True)
    return (x * lax.rsqrt(ms + EPS) * (nw * (1.0 + sc)) + sh).astype(BF16)


def _ffn_kernel(x_ref, nw_ref, sh_ref, sc_ref, g_ref, w1_ref, w2_ref, o_ref, h_scr):
    f = pl.program_id(1)

    @pl.when(f == 0)
    def _():
        h_scr[...] = _norm_mod(x_ref[...], nw_ref[...], sc_ref[...], sh_ref[...])
        o_ref[...] = jnp.zeros_like(o_ref)

    a = jnp.dot(h_scr[...], w1_ref[...], preferred_element_type=F32)
    a = jnp.maximum(a, 0.0)
    o_ref[...] += jnp.dot((a * a).astype(BF16), w2_ref[...], preferred_element_type=F32)

    @pl.when(f == pl.num_programs(1) - 1)
    def _():
        o_ref[...] = x_ref[...] + g_ref[...] * o_ref[...]


def _ffn(x2d, mod4, row_of, l, W, tm):
    M, D = x2d.shape
    F = W["w_ff1"].shape[-1]
    tf = _pick(F, (FFN_HIDDEN_TILE, 256))
    modv = lambda k: pl.BlockSpec((None, None, 1, D), lambda i, f: (l, row_of(i * tm), 0, k))
    return pl.pallas_call(
        _ffn_kernel,
        grid=(M // tm, F // tf),
        in_specs=[
            pl.BlockSpec((tm, D), lambda i, f: (i, 0)),
            pl.BlockSpec((None, 1, D), lambda i, f: (l, 0, 0)),
            modv(3), modv(4), modv(5),
            pl.BlockSpec((None, D, tf), lambda i, f: (l, 0, f)),
            pl.BlockSpec((None, tf, D), lambda i, f: (l, f, 0)),
        ],
        out_specs=pl.BlockSpec((tm, D), lambda i, f: (i, 0)),
        out_shape=jax.ShapeDtypeStruct((M, D), F32),
        scratch_shapes=[pltpu.VMEM((tm, D), BF16)],
        compiler_params=_params(("arbitrary", "arbitrary")),
        name="ffn",
    )(x2d, W["norm2"], mod4, mod4, mod4, W["w_ff1"], W["w_ff2"])


def _pad_lanes(x, n=LANE):
    return jnp.pad(x, [(0, 0)] * (x.ndim - 1) + [(0, n - x.shape[-1])])


def _rot_half(x):
    h = x.shape[-1] // 2
    return jnp.concatenate([-x[..., h:], x[..., :h]], axis=-1)


def _swap_half(x):
    h = x.shape[-1] // 2
    return jnp.concatenate([x[..., h:], x[..., :h]], axis=-1)


def _norm_gains(g):
    gr = g[:, QK_NOPE:]
    return (g[:, None, :QK_NOPE], _pad_lanes(gr)[:, None, :], _pad_lanes(_swap_half(gr))[:, None, :])


def _block_diag_tiles(w, per):
    *lead, nb, s, _ = w.shape
    w = w.reshape(*lead, nb // per, per, s, s)
    t = jnp.einsum("...kij,km->...kimj", w, jnp.eye(per, dtype=w.dtype))
    return t.reshape(*lead, nb // per, per * s, per * s)


def _pack_weights(w_in, q_a_norm, w_qb, kv_a_norm, w_kvb, q_norm, k_norm, w_mla_o, conv_w, conv_b,
                  lru_wa, lru_ba, lru_wx, lru_bx, lru_lambda, w_lru_o, pool_w, pool_scale, w_pool_o,
                  w_out, w_ff1, w_ff2, norm1_w, norm2_w):
    L, D, _ = w_in.shape
    o_kr = Q_LORA + KV_LORA
    w_in = w_in.astype(BF16)
    wkr = w_in[..., o_kr:o_kr + QK_ROPE]
    w_in_p = jnp.concatenate(
        [w_in[..., :o_kr], _pad_lanes(wkr), _pad_lanes(_rot_half(wkr)), w_in[..., o_kr + QK_ROPE:]], axis=-1)

    wq = w_qb.reshape(L, Q_LORA, N_HEADS, QK_DIM)
    wq_rope = wq[..., QK_NOPE:]
    w_q = jnp.concatenate([wq[..., :QK_NOPE].reshape(L, Q_LORA, -1),
                           _pad_lanes(wq_rope).reshape(L, Q_LORA, -1)], axis=-1)
    w_qrot = _pad_lanes(_rot_half(wq_rope)).reshape(L, Q_LORA, -1)

    wkv = w_kvb.reshape(L, KV_LORA, N_HEADS, QK_NOPE + V_DIM)
    w_kv = jnp.concatenate([wkv[..., :QK_NOPE].reshape(L, KV_LORA, -1),
                            wkv[..., QK_NOPE:].reshape(L, KV_LORA, -1)], axis=-1)

    qg = _norm_gains(q_norm)
    kg = _norm_gains(k_norm)

    per = MXU_TILE // lru_wa.shape[-1]
    ta = _block_diag_tiles(lru_wa, per)
    tx = _block_diag_tiles(lru_wx, per)
    lru_w = jnp.concatenate([ta[:, 0], tx[:, 0], ta[:, 1], tx[:, 1]], axis=-1)
    nct = lru_w.shape[1]
    ba = lru_ba.reshape(L, 2, nct, 1, MXU_TILE)
    bx = lru_bx.reshape(L, 2, nct, 1, MXU_TILE)
    lru_b = jnp.concatenate([ba[:, 0], bx[:, 0], ba[:, 1], bx[:, 1]], axis=-1)

    return dict(
        w_in=w_in_p.astype(BF16), norm1=norm1_w[:, None, :], norm2=norm2_w[:, None, :],
        q_a_norm=q_a_norm[:, None, :], kv_a_norm=kv_a_norm[:, None, :],
        w_q=w_q.astype(BF16), w_qrot=w_qrot.astype(BF16), w_kv=w_kv.astype(BF16),
        qg_n=qg[0], qg_r=qg[1], qg_s=qg[2], kg_n=kg[0], kg_r=kg[1], kg_s=kg[2],
        w_mla_o=w_mla_o.astype(BF16), w_lru_o=w_lru_o.astype(BF16), w_pool_o=w_pool_o.astype(BF16),
        conv_w=conv_w, conv_b=conv_b[:, None, :], lru_w=lru_w.astype(BF16), lru_b=lru_b,
        lru_lambda=lru_lambda, pool_w=pool_w.astype(BF16), pool_scale=pool_scale[:, None, :],
        w_out=w_out.astype(BF16), w_ff1=w_ff1.astype(BF16), w_ff2=w_ff2.astype(BF16),
    )


def _rope_tables(T):
    rows = T // GRID_W
    row = jnp.repeat(jnp.arange(rows), GRID_W).astype(F32)
    col = jnp.tile(jnp.arange(GRID_W), rows).astype(F32)
    n_freq = QK_ROPE // 4
    inv = ROPE_BASE ** (-(jnp.arange(n_freq, dtype=F32) / n_freq))
    ang = jnp.concatenate([row[:, None] * inv, col[:, None] * inv], axis=-1)
    cos, sin = jnp.cos(ang), jnp.sin(ang)
    return (_pad_lanes(jnp.concatenate([cos, cos], axis=-1)), _pad_lanes(jnp.concatenate([sin, sin], axis=-1)))


def _trunk_layer(x2d, B, T, mod4, row_of, shared_cond, l, W, rope, ctx, h0):
    M = B * T
    span = M if shared_cond else T
    tm = _pick(span, (1024, 512, 256, 128))
    tmp = _pick(T, (512, 256, 128))
    z, gates = _in_proj(x2d, mod4, row_of, W["norm1"], W["w_in"], l, tm)
    z3 = z.reshape(B, T, -1)

    q, k, v, ckv = _qk_prep(z, B, T, l, W, rope, tmp)
    if ctx is None:
        kc = vc = None
    else:
        ckv_c, kr_c = ctx
        P = ckv_c.shape[2]
        kc, vc = _cache_k_prep(ckv_c, kr_c, B, P, l, W, _pick(P, (512, 256, 128)))
    attn = _attention(q, k, v, kc, vc, _pick(T, (ATTN_STEP_ROWS, 1024, 512, 256, 128)))

    lru, state = _lru_mixer(z3, h0, l, W, _pick(T, (256, 128)))
    pool = _pool_mixer(z3, l, W)

    merged = _merge(attn.reshape(M, -1), lru.reshape(M, -1), pool.reshape(M, -1), gates, l, W, tm)
    x2d = _out_proj(merged, x2d, mod4, row_of, l, W, tm)
    x2d = _ffn(x2d, mod4, row_of, l, W, _pick(span, (FFN_ROW_TILE, 512, 256, 128)))
    return x2d, z, ckv, state


def kernel(x_prompt, x_sample, c, cache_ckv, cache_krope, state_lru, c_ctx, w_mod, b_mod, norm1_w, norm2_w, w_in, q_a_norm, w_qb, kv_a_norm, w_kvb, q_norm, k_norm, w_mla_o, conv_w, conv_b, lru_wa, lru_ba, lru_wx, lru_bx, lru_lambda, w_lru_o, pool_w, pool_scale, w_pool_o, w_out, w_ff1, w_ff2):
    L = w_in.shape[0]
    Bc, Tc, D = x_prompt.shape
    Bs, Ts, _ = x_sample.shape

    W = _pack_weights(w_in, q_a_norm, w_qb, kv_a_norm, w_kvb, q_norm, k_norm, w_mla_o, conv_w, conv_b,
                      lru_wa, lru_ba, lru_wx, lru_bx, lru_lambda, w_lru_o, pool_w, pool_scale, w_pool_o,
                      w_out, w_ff1, w_ff2, norm1_w, norm2_w)

    rows = Bs + 1
    rpad = -rows % 8
    cond = jnp.concatenate([c, c_ctx[None, :], jnp.zeros((rpad, D), F32)], axis=0)
    mod = _modulation(cond, w_mod, b_mod)
    mod4 = mod.reshape(L, rows + rpad, 1, N_MOD * D)

    rope = _rope_tables(Ts)
    kr_cache = _pad_lanes(cache_krope)
    zero_state = jnp.zeros((Bc, 2, LRU_WIDTH), F32)

    row_smp = lambda t: t // Ts
    row_ctx = lambda t: Bs

    xp = x_prompt.reshape(Bc * Tc, D)
    xs = x_sample.reshape(Bs * Ts, D)
    ckv_list, kr_list, st_list = [], [], []
    for l in range(L):
        xp, zc, ckv, st = _trunk_layer(xp, Bc, Tc, mod4, row_ctx, True, l, W, None, None, zero_state)
        ckv_list.append(ckv.reshape(Bc, Tc, KV_LORA))
        kr_list.append(zc[:, OFF_KR:OFF_KR + QK_ROPE].reshape(Bc, Tc, QK_ROPE))
        st_list.append(st)
        xs, _, _, _ = _trunk_layer(xs, Bs, Ts, mod4, row_smp, False, l, W, rope,
                                   (cache_ckv, kr_cache), state_lru[:, l])
    return (xp.reshape(Bc, Tc, D), xs.reshape(Bs, Ts, D), jnp.stack(ckv_list, axis=1),
            jnp.stack(kr_list, axis=1), jnp.stack(st_list, axis=1))
```

```python
import functools

import jax
import jax.numpy as jnp
from jax import lax
from jax.experimental import pallas as pl
from jax.experimental.pallas import tpu as pltpu

F32 = jnp.float32
BF16 = jnp.bfloat16

EPS = 1e-6
GRID_W = 64
N_HEADS = 8
QK_NOPE = 128
QK_ROPE = 64
V_DIM = 128
QK_DIM = QK_NOPE + QK_ROPE
Q_LORA = 512
KV_LORA = 256
ROPE_BASE = 10000.0
LRU_WIDTH = 1024
LRU_C = 8.0
POOL_WIDTH = 1024
POOL_WINDOWS = (2, 4, 8, 16)
N_MOD = 6

LANE = 128
SUBLANE = 8
MXU_TILE = 256
VMEM_LIMIT = 56 * 2**20
MIXER_STEP_ROWS = 1024
FFN_ROW_TILE = 512
FFN_HIDDEN_TILE = 1024
ATTN_STEP_ROWS = 4096
ATTN_CHAIN_ROWS = 256
ATTN_LOOKAHEAD = 2
LOG2E = 1.4426950408889634
Q_PRESCALE = QK_DIM ** -0.5 * LOG2E

OFF_Q = 0
OFF_KV = Q_LORA
OFF_KR = OFF_KV + KV_LORA
OFF_KROT = OFF_KR + LANE
OFF_LRU = OFF_KROT + LANE
OFF_GATE = OFF_LRU + LRU_WIDTH
OFF_POOL = OFF_GATE + LRU_WIDTH
OFF_BR = OFF_POOL + POOL_WIDTH


def _params(sem):
    return pltpu.CompilerParams(dimension_semantics=sem, vmem_limit_bytes=VMEM_LIMIT)


def _pick(n, prefs):
    for p in prefs:
        if n % p == 0:
            return p
    return n


def _mod_kernel(c_ref, w_ref, b_ref, o_ref):
    c = c_ref[...]
    s = c * jax.nn.sigmoid(c)
    o_ref[...] = jnp.dot(s.astype(BF16), w_ref[...].astype(BF16), preferred_element_type=F32) + b_ref[...]


def _modulation(cond, w_mod, b_mod):
    L, D, N = w_mod.shape
    R = cond.shape[0]
    tn = _pick(N, (1024, 512, 256, 128))
    return pl.pallas_call(
        _mod_kernel,
        grid=(L, N // tn),
        in_specs=[
            pl.BlockSpec((R, D), lambda l, j: (0, 0)),
            pl.BlockSpec((None, D, tn), lambda l, j: (l, 0, j)),
            pl.BlockSpec((None, 1, tn), lambda l, j: (l, 0, j)),
        ],
        out_specs=pl.BlockSpec((None, R, tn), lambda l, j: (l, 0, j)),
        out_shape=jax.ShapeDtypeStruct((L, R, N), F32),
        compiler_params=_params(("arbitrary", "arbitrary")),
        name="modulation",
    )(cond, w_mod, b_mod.reshape(L, 1, N))


def _inproj_kernel(x_ref, nw_ref, sh_ref, sc_ref, w_ref, z_ref, g_ref, h_scr, *, n_lin):
    j = pl.program_id(1)

    @pl.when(j == 0)
    def _():
        h_scr[...] = _norm_mod(x_ref[...], nw_ref[...], sc_ref[...], sh_ref[...])

    @pl.when(j < n_lin)
    def _():
        z_ref[...] = jnp.dot(h_scr[...], w_ref[...], preferred_element_type=F32)

    @pl.when(j >= n_lin)
    def _():
        g_ref[...] = _sigmoid(jnp.dot(h_scr[...], w_ref[...], preferred_element_type=F32)).astype(BF16)


def _in_proj(x2d, mod4, row_of, norm_w, w_in_p, l, tm):
    M, D = x2d.shape
    N = w_in_p.shape[-1]
    tn = _pick(OFF_BR, (1024, 512, 256))
    n_lin = OFF_BR // tn
    return pl.pallas_call(
        functools.partial(_inproj_kernel, n_lin=n_lin),
        grid=(M // tm, N // tn),
        in_specs=[
            pl.BlockSpec((tm, D), lambda i, j: (i, 0)),
            pl.BlockSpec((None, 1, D), lambda i, j: (l, 0, 0)),
            pl.BlockSpec((None, None, 1, D), lambda i, j: (l, row_of(i * tm), 0, 0)),
            pl.BlockSpec((None, None, 1, D), lambda i, j: (l, row_of(i * tm), 0, 1)),
            pl.BlockSpec((None, D, tn), lambda i, j: (l, 0, j)),
        ],
        out_specs=[
            pl.BlockSpec((tm, tn), lambda i, j: (i, jnp.minimum(j, n_lin - 1))),
            pl.BlockSpec((tm, tn), lambda i, j: (i, jnp.maximum(j - n_lin, 0))),
        ],
        out_shape=[
            jax.ShapeDtypeStruct((M, OFF_BR), F32),
            jax.ShapeDtypeStruct((M, N - OFF_BR), BF16),
        ],
        scratch_shapes=[pltpu.VMEM((tm, D), BF16)],
        compiler_params=_params(("arbitrary", "arbitrary")),
        name="in_proj",
    )(x2d, norm_w, mod4, mod4, w_in_p)


def _head_scale(nope, rope_sq_sum):
    ss = jnp.sum(nope * nope, axis=-1, keepdims=True) + rope_sq_sum
    return lax.rsqrt(ss * (1.0 / QK_DIM) + EPS)


def _qprep_kernel(*refs, use_rope):
    if use_rope:
        ql_ref, an_ref, w_ref, wrot_ref, gn_ref, gr_ref, gs_ref, cos_ref, sin_ref, q_ref = refs
    else:
        ql_ref, an_ref, w_ref, gn_ref, gr_ref, q_ref = refs
    ql = ql_ref[...]
    ms = jnp.mean(ql * ql, axis=-1, keepdims=True)
    qn = (ql * lax.rsqrt(ms + EPS) * an_ref[...]).astype(BF16)
    qq = jnp.dot(qn, w_ref[...], preferred_element_type=F32)
    if use_rope:
        qrot = jnp.dot(qn, wrot_ref[...], preferred_element_type=F32)
        cos = cos_ref[...]
        sin = sin_ref[...]
        gs = gs_ref[...]
    gn = gn_ref[...]
    gr = gr_ref[...]
    hw = N_HEADS * LANE
    for h in range(N_HEADS):
        nope = qq[:, h * LANE:(h + 1) * LANE]
        rope = qq[:, hw + h * LANE:hw + (h + 1) * LANE]
        s = _head_scale(nope, jnp.sum(rope * rope, axis=-1, keepdims=True))
        if use_rope:
            r = gr * rope * cos + gs * qrot[:, h * LANE:(h + 1) * LANE] * sin
        else:
            r = gr * rope
        s = s * Q_PRESCALE
        q_ref[h, :, 0:LANE] = (nope * s * gn).astype(BF16)
        q_ref[h, :, LANE:2 * LANE] = (r * s).astype(BF16)


def _q_prep_io(z, B, T, l, W, rope, tm):
    nb = T // tm
    use_rope = rope is not None
    wspec = lambda n: pl.BlockSpec((None, Q_LORA, n), lambda i: (l, 0, 0))
    vec = pl.BlockSpec((None, 1, LANE), lambda i: (l, 0, 0))
    in_specs = [
        pl.BlockSpec((tm, Q_LORA), lambda i: (i, OFF_Q // Q_LORA)),
        pl.BlockSpec((None, 1, Q_LORA), lambda i: (l, 0, 0)),
        wspec(2 * N_HEADS * LANE),
    ]
    args = [z, W["q_a_norm"], W["w_q"]]
    if use_rope:
        tab = pl.BlockSpec((tm, LANE), lambda i: (i % nb, 0))
        in_specs += [wspec(N_HEADS * LANE), vec, vec, vec, tab, tab]
        args += [W["w_qrot"], W["qg_n"], W["qg_r"], W["qg_s"], rope[0], rope[1]]
    else:
        in_specs += [vec, vec]
        args += [W["qg_n"], W["qg_r"]]
    out_specs = [pl.BlockSpec((None, N_HEADS, tm, 2 * LANE), lambda i: (i // nb, 0, i % nb, 0))]
    out_shape = [jax.ShapeDtypeStruct((B, N_HEADS, T, 2 * LANE), BF16)]
    return in_specs, args, out_specs, out_shape


def _kprep_kernel(*refs, use_rope, normalize):
    refs = list(refs)
    kv_ref = refs.pop(0)
    kr_ref = refs.pop(0)
    krot_ref = refs.pop(0) if use_rope else None
    an_ref = refs.pop(0) if normalize else None
    w_ref = refs.pop(0)
    gn_ref = refs.pop(0)
    gr_ref = refs.pop(0)
    if use_rope:
        gs_ref, cos_ref, sin_ref = refs.pop(0), refs.pop(0), refs.pop(0)
    k_ref = refs.pop(0)
    v_ref = refs.pop(0)
    ckv_ref = refs.pop(0) if normalize else None

    kv = kv_ref[...]
    if normalize:
        ms = jnp.mean(kv * kv, axis=-1, keepdims=True)
        kv = kv * lax.rsqrt(ms + EPS) * an_ref[...]
        ckv_ref[...] = kv
    kk = jnp.dot(kv.astype(BF16), w_ref[...], preferred_element_type=F32)
    kr = kr_ref[...]
    ssr = jnp.sum(kr * kr, axis=-1, keepdims=True)
    if use_rope:
        base = gr_ref[...] * kr * cos_ref[...] + gs_ref[...] * krot_ref[...] * sin_ref[...]
    else:
        base = gr_ref[...] * kr
    gn = gn_ref[...]
    hw = N_HEADS * LANE
    for h in range(N_HEADS):
        nope = kk[:, h * LANE:(h + 1) * LANE]
        s = _head_scale(nope, ssr)
        k_ref[h, :, 0:LANE] = (nope * s * gn).astype(BF16)
        k_ref[h, :, LANE:2 * LANE] = (base * s).astype(BF16)
        v_ref[h] = kk[:, hw + h * LANE:hw + (h + 1) * LANE].astype(BF16)


def _k_prep_io(kv_src, kr_src, B, T, l, W, rope, tm, *, kv_spec, kr_spec, krot_spec, normalize):
    M = B * T
    nb = T // tm
    use_rope = rope is not None
    vec = pl.BlockSpec((None, 1, LANE), lambda i: (l, 0, 0))
    in_specs = [kv_spec, kr_spec]
    args = [kv_src, kr_src]
    if use_rope:
        in_specs.append(krot_spec)
        args.append(kr_src)
    if normalize:
        in_specs.append(pl.BlockSpec((None, 1, KV_LORA), lambda i: (l, 0, 0)))
        args.append(W["kv_a_norm"])
    in_specs += [pl.BlockSpec((None, KV_LORA, 2 * N_HEADS * LANE), lambda i: (l, 0, 0)), vec, vec]
    args += [W["w_kv"], W["kg_n"], W["kg_r"]]
    if use_rope:
        tab = pl.BlockSpec((tm, LANE), lambda i: (i % nb, 0))
        in_specs += [vec, tab, tab]
        args += [W["kg_s"], rope[0], rope[1]]
    out_specs = [
        pl.BlockSpec((None, N_HEADS, tm, 2 * LANE), lambda i: (i // nb, 0, i % nb, 0)),
        pl.BlockSpec((None, N_HEADS, tm, LANE), lambda i: (i // nb, 0, i % nb, 0)),
    ]
    out_shape = [
        jax.ShapeDtypeStruct((B, N_HEADS, T, 2 * LANE), BF16),
        jax.ShapeDtypeStruct((B, N_HEADS, T, LANE), BF16),
    ]
    if normalize:
        out_specs.append(pl.BlockSpec((tm, KV_LORA), lambda i: (i, 0)))
        out_shape.append(jax.ShapeDtypeStruct((M, KV_LORA), F32))
    return in_specs, args, out_specs, out_shape


def _cache_k_prep(ckv_c, kr_c, B, P, l, W, tp):
    npb = P // tp
    in_specs, args, out_specs, out_shape = _k_prep_io(
        ckv_c, kr_c, B, P, l, W, None, tp,
        kv_spec=pl.BlockSpec((None, None, tp, KV_LORA), lambda i: (i // npb, l, i % npb, 0)),
        kr_spec=pl.BlockSpec((None, None, tp, LANE), lambda i: (i // npb, l, i % npb, 0)),
        krot_spec=None, normalize=False)
    return pl.pallas_call(
        functools.partial(_kprep_kernel, use_rope=False, normalize=False),
        grid=(B * P // tp,),
        in_specs=in_specs,
        out_specs=out_specs,
        out_shape=out_shape,
        compiler_params=_params(("arbitrary",)),
        name="k_prep",
    )(*args)


def _qkprep_kernel(*refs, use_rope, n_q_in, n_k_in):
    q_in, k_in = refs[:n_q_in], refs[n_q_in:n_q_in + n_k_in]
    q_out, k_out = refs[n_q_in + n_k_in], refs[n_q_in + n_k_in + 1:]
    _qprep_kernel(*q_in, q_out, use_rope=use_rope)
    _kprep_kernel(*k_in, *k_out, use_rope=use_rope, normalize=True)


def _qk_prep(z, B, T, l, W, rope, tm):
    q_io = _q_prep_io(z, B, T, l, W, rope, tm)
    k_io = _k_prep_io(
        z, z, B, T, l, W, rope, tm,
        kv_spec=pl.BlockSpec((tm, KV_LORA), lambda i: (i, OFF_KV // KV_LORA)),
        kr_spec=pl.BlockSpec((tm, LANE), lambda i: (i, OFF_KR // LANE)),
        krot_spec=pl.BlockSpec((tm, LANE), lambda i: (i, OFF_KROT // LANE)),
        normalize=True)
    return pl.pallas_call(
        functools.partial(_qkprep_kernel, use_rope=rope is not None, n_q_in=len(q_io[1]), n_k_in=len(k_io[1])),
        grid=(B * T // tm,),
        in_specs=q_io[0] + k_io[0],
        out_specs=q_io[2] + k_io[2],
        out_shape=q_io[3] + k_io[3],
        compiler_params=_params(("arbitrary",)),
        name="qk_prep",
    )(*q_io[1], *k_io[1])


def _attn_kernel(*refs, has_ctx, nsub):
    if has_ctx:
        q_ref, k_ref, v_ref, kc_ref, vc_ref, o_ref = refs
    else:
        q_ref, k_ref, v_ref, o_ref = refs
    nt = (((1,), (1,)), ((), ()))
    hb, tq, _ = q_ref.shape
    ts = tq // nsub

    def key_tiles(h):
        out = []
        for kref, vref in ((k_ref, v_ref), (kc_ref, vc_ref)) if has_ctx else ((k_ref, v_ref),):
            kt = min(MXU_TILE, kref.shape[1])
            out += [(kref.at[h], vref.at[h], t * kt, kt) for t in range(kref.shape[1] // kt)]
        return out

    def fold_lanes(x, op):
        out = x[:, :LANE]
        for c in range(1, x.shape[1] // LANE):
            out = op(out, x[:, c * LANE:(c + 1) * LANE])
        return out

    def scores(h, r):
        q = q_ref[h, r * ts:(r + 1) * ts, :]
        s_tiles = []
        mrun = None
        for kref, _, off, kt in key_tiles(h):
            st = lax.dot_general(q, kref[off:off + kt, :], nt, preferred_element_type=F32)
            s_tiles.append(st)
            tmax = fold_lanes(st, jnp.maximum)
            mrun = tmax if mrun is None else jnp.maximum(mrun, tmax)
        return s_tiles, jnp.max(mrun, axis=-1, keepdims=True)

    def weights_and_values(h, r, s_tiles, m):
        lrun = None
        o = None
        for st, (_, vref, off, kt) in zip(s_tiles, key_tiles(h)):
            p = jnp.exp2(st - m)
            psum = fold_lanes(p, jnp.add)
            lrun = psum if lrun is None else lrun + psum
            ot = jnp.dot(p.astype(BF16), vref[off:off + kt, :], preferred_element_type=F32)
            o = ot if o is None else o + ot
        den = jnp.sum(lrun, axis=-1, keepdims=True)
        o_ref[r * ts:(r + 1) * ts, h * LANE:(h + 1) * LANE] = (o / den).astype(BF16)

    chains = [(h, r) for h in range(hb) for r in range(nsub)]
    ahead = min(ATTN_LOOKAHEAD, len(chains))
    pending = [scores(*c) for c in chains[:ahead]]
    for n, c in enumerate(chains):
        s_tiles, m = pending.pop(0)
        if n + ahead < len(chains):
            pending.append(scores(*chains[n + ahead]))
        weights_and_values(*c, s_tiles, m)


def _attention(q, k, v, kc, vc, tq):
    B, H, T, _ = q.shape
    S = k.shape[2]
    has_ctx = kc is not None
    hb = max(1, min(H, ATTN_STEP_ROWS // tq))
    in_specs = [
        pl.BlockSpec((None, hb, tq, 2 * LANE), lambda b, h, i: (b, h, i, 0)),
        pl.BlockSpec((None, hb, S, 2 * LANE), lambda b, h, i: (b, h, 0, 0)),
        pl.BlockSpec((None, hb, S, LANE), lambda b, h, i: (b, h, 0, 0)),
    ]
    args = [q, k, v]
    if has_ctx:
        P = kc.shape[2]
        in_specs += [
            pl.BlockSpec((None, hb, P, 2 * LANE), lambda b, h, i: (b, h, 0, 0)),
            pl.BlockSpec((None, hb, P, LANE), lambda b, h, i: (b, h, 0, 0)),
        ]
        args += [kc, vc]
    nsub = max(1, tq // ATTN_CHAIN_ROWS)
    return pl.pallas_call(
        functools.partial(_attn_kernel, has_ctx=has_ctx, nsub=nsub),
        grid=(B, H // hb, T // tq),
        in_specs=in_specs,
        out_specs=pl.BlockSpec((None, tq, hb * LANE), lambda b, h, i: (b, i, h)),
        out_shape=jax.ShapeDtypeStruct((B, T, H * LANE), BF16),
        compiler_params=_params(("arbitrary", "arbitrary", "arbitrary")),
        name="attention",
    )(*args)


def _shift_rows(x, k, sub):
    n = x.shape[0]
    rolled = pltpu.roll(x, k % n, 0)
    if k > 0:
        head = jnp.where(sub >= k, rolled[:SUBLANE], 0.0)
        return jnp.concatenate([head, rolled[SUBLANE:]], axis=0)
    tail = jnp.where(sub < SUBLANE + k, rolled[n - SUBLANE:], 0.0)
    return jnp.concatenate([rolled[:n - SUBLANE], tail], axis=0)


def _scan_chunk(a, b, carry, sub, reverse):
    n = a.shape[0]
    ng = n // SUBLANE
    hs = [None] * ng
    for j in (range(ng - 1, -1, -1) if reverse else range(ng)):
        aj = a[j * SUBLANE:(j + 1) * SUBLANE]
        bj = b[j * SUBLANE:(j + 1) * SUBLANE]
        d = 1
        while d < SUBLANE:
            shift = SUBLANE - d if reverse else d
            valid = (sub < SUBLANE - d) if reverse else (sub >= d)
            a_s = pltpu.roll(aj, shift, 0)
            b_s = pltpu.roll(bj, shift, 0)
            bj = jnp.where(valid, aj * b_s, 0.0) + bj
            aj = jnp.where(valid, aj * a_s, aj)
            d *= 2
        h = aj * carry + bj
        hs[j] = h
        carry = h[0:1] if reverse else h[SUBLANE - 1:SUBLANE]
    return jnp.concatenate(hs, axis=0), carry


def _sigmoid(x):
    return 0.5 * jnp.tanh(0.5 * x) + 0.5


def _softplus(x):
    return jnp.maximum(x, 0.0) + jnp.log(1.0 + jnp.exp(-jnp.abs(x)))


def _gelu_tanh(x):
    return 0.5 * x * (1.0 + jnp.tanh(0.7978845608028654 * (x + 0.044715 * x * x * x)))


def _lru_kernel(u_ref, ug_ref, cw_ref, cb_ref, w_ref, b_ref, lam_ref, h0_ref, y_ref, st_ref,
                xc_scr, hf_scr, hb_scr, *, tc):
    C = xc_scr.shape[1]
    for c in range(w_ref.shape[0]):
        cs = slice(c * C, (c + 1) * C)
        _lru_tile(u_ref.at[:, cs], ug_ref.at[:, cs], cw_ref.at[:, cs], cb_ref.at[:, cs], w_ref.at[c],
                  b_ref.at[c], lam_ref.at[:, cs], h0_ref.at[:, cs], y_ref.at[:, cs], st_ref.at[:, cs],
                  xc_scr, hf_scr, hb_scr, tc=tc)


def _lru_tile(u_ref, ug_ref, cw_ref, cb_ref, w_ref, b_ref, lam_ref, h0_ref, y_ref, st_ref,
              xc_scr, hf_scr, hb_scr, *, tc):
    T, C = xc_scr.shape
    nch = T // tc
    sub = lax.broadcasted_iota(jnp.int32, (SUBLANE, C), 0)
    u = u_ref[...]
    cw = cw_ref[...]
    xc = (cb_ref[...] + cw[0:1] * _shift_rows(u, 2, sub) + cw[1:2] * _shift_rows(u, 1, sub)
          + cw[2:3] * u + cw[3:4] * _shift_rows(u, -1, sub))
    xc_scr[...] = xc

    decay = LRU_C * _softplus(-lam_ref[...])
    h0 = h0_ref[...]

    def gates(xcj, d):
        g = jnp.dot(xcj.astype(BF16), w_ref[:, 2 * d * C:2 * (d + 1) * C], preferred_element_type=F32)
        g = g + b_ref[:, 2 * d * C:2 * (d + 1) * C]
        r = _sigmoid(g[:, :C])
        i = _sigmoid(g[:, C:])
        nla = r * decay[d:d + 1]
        a = jnp.exp2(nla * (-LOG2E))
        v = jnp.tanh(nla) * (a * a + 1.0)
        b = jnp.where(v > 0.0, v * lax.rsqrt(v), 0.0) * (i * xcj)
        return a, b

    def scan_dir(t0, carry, d):
        a, b = gates(xc_scr[pl.ds(t0, tc), :], d)
        return _scan_chunk(a, b, carry, sub, d == 1)

    def emit(t0, hf, hb):
        y = (hf + hb) * _gelu_tanh(ug_ref[pl.ds(t0, tc), :])
        y_ref[pl.ds(t0, tc), :] = y.astype(BF16)

    def meet(jj, carry):
        tf0 = pl.multiple_of(jj * tc, tc)
        tb0 = pl.multiple_of((nch - 1 - jj) * tc, tc)
        hf, cf = scan_dir(tf0, carry[0], 0)
        hb, cb = scan_dir(tb0, carry[1], 1)
        hf_scr[pl.ds(tf0, tc), :] = hf
        hb_scr[pl.ds(tb0, tc), :] = hb
        return cf, cb

    def part(jj, carry):
        tf0 = pl.multiple_of(jj * tc, tc)
        tb0 = pl.multiple_of((nch - 1 - jj) * tc, tc)
        hf, cf = scan_dir(tf0, carry[0], 0)
        hb, cb = scan_dir(tb0, carry[1], 1)
        emit(tf0, hf, hb_scr[pl.ds(tf0, tc), :])
        emit(tb0, hf_scr[pl.ds(tb0, tc), :], hb)
        return cf, cb

    carry = (h0[0:1], h0[1:2])
    if nch % 2 == 0:
        carry = lax.fori_loop(0, nch // 2, meet, carry)
        carry = lax.fori_loop(nch // 2, nch, part, carry)
    else:
        assert nch == 1
        hf, cf = scan_dir(0, carry[0], 0)
        hb, cb = scan_dir(0, carry[1], 1)
        emit(0, hf, hb)
        carry = (cf, cb)
    st_ref[0:1, :] = carry[0]
    st_ref[1:2, :] = carry[1]


def _lru_mixer(z3, h0, l, W, tc):
    B, T, _ = z3.shape
    C = MXU_TILE
    nct = LRU_WIDTH // C
    nc = max(1, min(nct, MIXER_STEP_ROWS // T))
    Cw = nc * C
    return pl.pallas_call(
        functools.partial(_lru_kernel, tc=tc),
        grid=(B, nct // nc),
        in_specs=[
            pl.BlockSpec((None, T, Cw), lambda b, c: (b, 0, OFF_LRU // Cw + c)),
            pl.BlockSpec((None, T, Cw), lambda b, c: (b, 0, OFF_GATE // Cw + c)),
            pl.BlockSpec((None, 4, Cw), lambda b, c: (l, 0, c)),
            pl.BlockSpec((None, 1, Cw), lambda b, c: (l, 0, c)),
            pl.BlockSpec((None, nc, C, 4 * C), lambda b, c: (l, c, 0, 0)),
            pl.BlockSpec((None, nc, 1, 4 * C), lambda b, c: (l, c, 0, 0)),
            pl.BlockSpec((None, 2, Cw), lambda b, c: (l, 0, c)),
            pl.BlockSpec((None, 2, Cw), lambda b, c: (b, 0, c)),
        ],
        out_specs=[
            pl.BlockSpec((None, T, Cw), lambda b, c: (b, 0, c)),
            pl.BlockSpec((None, 2, Cw), lambda b, c: (b, 0, c)),
        ],
        out_shape=[
            jax.ShapeDtypeStruct((B, T, LRU_WIDTH), BF16),
            jax.ShapeDtypeStruct((B, 2, LRU_WIDTH), F32),
        ],
        scratch_shapes=[pltpu.VMEM((T, C), F32)] * 3,
        compiler_params=_params(("arbitrary", "arbitrary")),
        name="lru_mixer",
    )(z3, z3, W["conv_w"], W["conv_b"], W["lru_w"], W["lru_b"], W["lru_lambda"], h0)


def _pool_group(u_ref, w_ref, sc_ref, o_ref, win):
    T, C = u_ref.shape
    sub = lax.broadcasted_iota(jnp.int32, (SUBLANE, C), 0)
    u = u_ref[...]
    s = u + _shift_rows(u, 1, sub)
    w = 4
    while w <= win:
        q = w // 4
        s = _shift_rows(s, q, sub) + _shift_rows(s, -q, sub)
        w *= 2
    half = win // 2
    assert half <= SUBLANE and T >= 2 * SUBLANE
    head_cnt = (jnp.minimum(sub + half, T) - jnp.maximum(sub - half, 0)).astype(F32)
    tail_row = sub + (T - SUBLANE)
    tail_cnt = (jnp.minimum(tail_row + half, T) - jnp.maximum(tail_row - half, 0)).astype(F32)
    mean = jnp.concatenate([s[:SUBLANE] / head_cnt, s[SUBLANE:T - SUBLANE] * (1.0 / win),
                            s[T - SUBLANE:] / tail_cnt], axis=0)
    d = mean - u
    y = jnp.dot(d.astype(BF16), w_ref[...], preferred_element_type=F32) * sc_ref[...]
    o_ref[...] = y.astype(BF16)


def _pool_kernel(u_ref, w_ref, sc_ref, o_ref):
    ng_step = w_ref.shape[0]
    C = w_ref.shape[1]
    if ng_step == len(POOL_WINDOWS):
        for gi, win in enumerate(POOL_WINDOWS):
            cs = slice(gi * C, (gi + 1) * C)
            _pool_group(u_ref.at[:, cs], w_ref.at[gi], sc_ref.at[:, cs], o_ref.at[:, cs], win)
    else:
        g = pl.program_id(1)
        for gi, win in enumerate(POOL_WINDOWS):
            @pl.when(g == gi)
            def _(win=win):
                _pool_group(u_ref, w_ref.at[0], sc_ref, o_ref, win)


def _pool_mixer(z3, l, W):
    B, T, _ = z3.shape
    C = MXU_TILE
    ng = POOL_WIDTH // C
    ngs = ng if ng * T <= MIXER_STEP_ROWS else 1
    Cw = ngs * C
    return pl.pallas_call(
        _pool_kernel,
        grid=(B, ng // ngs),
        in_specs=[
            pl.BlockSpec((None, T, Cw), lambda b, g: (b, 0, OFF_POOL // Cw + g)),
            pl.BlockSpec((None, ngs, C, C), lambda b, g: (l, g, 0, 0)),
            pl.BlockSpec((None, 1, Cw), lambda b, g: (l, 0, g)),
        ],
        out_specs=pl.BlockSpec((None, T, Cw), lambda b, g: (b, 0, g)),
        out_shape=jax.ShapeDtypeStruct((B, T, POOL_WIDTH), BF16),
        compiler_params=_params(("arbitrary", "arbitrary")),
        name="pool_mixer",
    )(z3, W["pool_w"], W["pool_scale"])


def _merge_kernel(a_ref, r_ref, p_ref, g0_ref, g1_ref, g2_ref, wa_ref, wr_ref, wp_ref, o_ref):
    ya = jnp.dot(a_ref[...], wa_ref[...], preferred_element_type=F32)
    yr = jnp.dot(r_ref[...], wr_ref[...], preferred_element_type=F32)
    yp = jnp.dot(p_ref[...], wp_ref[...], preferred_element_type=F32)
    m = g0_ref[...].astype(F32) * ya + g1_ref[...].astype(F32) * yr + g2_ref[...].astype(F32) * yp
    o_ref[...] = m.astype(BF16)


def _merge(attn, lru, pool, gates, l, W, tm):
    M, K = attn.shape
    D = W["w_mla_o"].shape[-1]
    tn = _pick(D, (1024, 512, 256))
    nbr = D // tn
    act = pl.BlockSpec((tm, K), lambda i, j: (i, 0))
    wsp = pl.BlockSpec((None, K, tn), lambda i, j: (l, 0, j))
    gsp = lambda k: pl.BlockSpec((tm, tn), lambda i, j: (i, k * nbr + j))
    return pl.pallas_call(
        _merge_kernel,
        grid=(M // tm, D // tn),
        in_specs=[act, act, act, gsp(0), gsp(1), gsp(2), wsp, wsp, wsp],
        out_specs=pl.BlockSpec((tm, tn), lambda i, j: (i, j)),
        out_shape=jax.ShapeDtypeStruct((M, D), BF16),
        compiler_params=_params(("arbitrary", "arbitrary")),
        name="merge",
    )(attn, lru, pool, gates, gates, gates, W["w_mla_o"], W["w_lru_o"], W["w_pool_o"])


def _outproj_kernel(m_ref, w_ref, x_ref, g_ref, o_ref):
    y = jnp.dot(m_ref[...], w_ref[...], preferred_element_type=F32)
    o_ref[...] = x_ref[...] + g_ref[...] * y


def _out_proj(merged, x2d, mod4, row_of, l, W, tm):
    M, D = x2d.shape
    tn = _pick(D, (1024, 512, 256))
    nbr = D // tn
    return pl.pallas_call(
        _outproj_kernel,
        grid=(M // tm, D // tn),
        in_specs=[
            pl.BlockSpec((tm, D), lambda i, j: (i, 0)),
            pl.BlockSpec((None, D, tn), lambda i, j: (l, 0, j)),
            pl.BlockSpec((tm, tn), lambda i, j: (i, j)),
            pl.BlockSpec((None, None, 1, tn), lambda i, j: (l, row_of(i * tm), 0, 2 * nbr + j)),
        ],
        out_specs=pl.BlockSpec((tm, tn), lambda i, j: (i, j)),
        out_shape=jax.ShapeDtypeStruct((M, D), F32),
        compiler_params=_params(("arbitrary", "arbitrary")),
        name="out_proj",
    )(merged, W["w_out"], x2d, mod4)


def _norm_mod(x, nw, sc, sh):
    ms = jnp.mean(x * x, axis=-1, keepdims=True)
    return (x * lax.rsqrt(ms + EPS) * (nw * (1.0 + sc)) + sh).astype(BF16)


def _ffn_kernel(x_ref, nw_ref, sh_ref, sc_ref, g_ref, w1_ref, w2_ref, o_ref, h_scr):
    f = pl.program_id(1)

    @pl.when(f == 0)
    def _():
        h_scr[...] = _norm_mod(x_ref[...], nw_ref[...], sc_ref[...], sh_ref[...])
        o_ref[...] = jnp.zeros_like(o_ref)

    a = jnp.dot(h_scr[...], w1_ref[...], preferred_element_type=F32)
    a = jnp.maximum(a, 0.0)
    o_ref[...] += jnp.dot((a * a).astype(BF16), w2_ref[...], preferred_element_type=F32)

    @pl.when(f == pl.num_programs(1) - 1)
    def _():
        o_ref[...] = x_ref[...] + g_ref[...] * o_ref[...]


def _ffn(x2d, mod4, row_of, l, W, tm):
    M, D = x2d.shape
    F = W["w_ff1"].shape[-1]
    tf = _pick(F, (FFN_HIDDEN_TILE, 256))
    modv = lambda k: pl.BlockSpec((None, None, 1, D), lambda i, f: (l, row_of(i * tm), 0, k))
    return pl.pallas_call(
        _ffn_kernel,
        grid=(M // tm, F // tf),
        in_specs=[
            pl.BlockSpec((tm, D), lambda i, f: (i, 0)),
            pl.BlockSpec((None, 1, D), lambda i, f: (l, 0, 0)),
            modv(3), modv(4), modv(5),
            pl.BlockSpec((None, D, tf), lambda i, f: (l, 0, f)),
            pl.BlockSpec((None, tf, D), lambda i, f: (l, f, 0)),
        ],
        out_specs=pl.BlockSpec((tm, D), lambda i, f: (i, 0)),
        out_shape=jax.ShapeDtypeStruct((M, D), F32),
        scratch_shapes=[pltpu.VMEM((tm, D), BF16)],
        compiler_params=_params(("arbitrary", "arbitrary")),
        name="ffn",
    )(x2d, W["norm2"], mod4, mod4, mod4, W["w_ff1"], W["w_ff2"])


def _pad_lanes(x, n=LANE):
    return jnp.pad(x, [(0, 0)] * (x.ndim - 1) + [(0, n - x.shape[-1])])


def _rot_half(x):
    h = x.shape[-1] // 2
    return jnp.concatenate([-x[..., h:], x[..., :h]], axis=-1)


def _swap_half(x):
    h = x.shape[-1] // 2
    return jnp.concatenate([x[..., h:], x[..., :h]], axis=-1)


def _norm_gains(g):
    gr = g[:, QK_NOPE:]
    return (g[:, None, :QK_NOPE], _pad_lanes(gr)[:, None, :], _pad_lanes(_swap_half(gr))[:, None, :])


def _block_diag_tiles(w, per):
    *lead, nb, s, _ = w.shape
    w = w.reshape(*lead, nb // per, per, s, s)
    t = jnp.einsum("...kij,km->...kimj", w, jnp.eye(per, dtype=w.dtype))
    return t.reshape(*lead, nb // per, per * s, per * s)


def _pack_weights(w_in, q_a_norm, w_qb, kv_a_norm, w_kvb, q_norm, k_norm, w_mla_o, conv_w, conv_b,
                  lru_wa, lru_ba, lru_wx, lru_bx, lru_lambda, w_lru_o, pool_w, pool_scale, w_pool_o,
                  w_out, w_ff1, w_ff2, norm1_w, norm2_w):
    L, D, _ = w_in.shape
    o_kr = Q_LORA + KV_LORA
    wkr = w_in[..., o_kr:o_kr + QK_ROPE]
    head = jnp.concatenate([w_in[..., :o_kr], _pad_lanes(wkr), _pad_lanes(_rot_half(wkr))], axis=-1).astype(BF16)
    tail = jnp.pad(w_in[..., o_kr + QK_ROPE:].astype(BF16), ((0, 0), (0, 0), (OFF_LRU, 0)))
    w_in_p = lax.dynamic_update_slice(tail, head, (0, 0, 0))

    wq = w_qb.reshape(L, Q_LORA, N_HEADS, QK_DIM)
    wq_rope = wq[..., QK_NOPE:]
    w_q = jnp.concatenate([wq[..., :QK_NOPE].reshape(L, Q_LORA, -1),
                           _pad_lanes(wq_rope).reshape(L, Q_LORA, -1)], axis=-1)
    w_qrot = _pad_lanes(_rot_half(wq_rope)).reshape(L, Q_LORA, -1)

    wkv = w_kvb.reshape(L, KV_LORA, N_HEADS, QK_NOPE + V_DIM)
    w_kv = jnp.concatenate([wkv[..., :QK_NOPE].reshape(L, KV_LORA, -1),
                            wkv[..., QK_NOPE:].reshape(L, KV_LORA, -1)], axis=-1)

    qg = _norm_gains(q_norm)
    kg = _norm_gains(k_norm)

    per = MXU_TILE // lru_wa.shape[-1]
    ta = _block_diag_tiles(lru_wa, per)
    tx = _block_diag_tiles(lru_wx, per)
    lru_w = jnp.concatenate([ta[:, 0], tx[:, 0], ta[:, 1], tx[:, 1]], axis=-1)
    nct = lru_w.shape[1]
    ba = lru_ba.reshape(L, 2, nct, 1, MXU_TILE)
    bx = lru_bx.reshape(L, 2, nct, 1, MXU_TILE)
    lru_b = jnp.concatenate([ba[:, 0], bx[:, 0], ba[:, 1], bx[:, 1]], axis=-1)

    return dict(
        w_in=w_in_p.astype(BF16), norm1=norm1_w[:, None, :], norm2=norm2_w[:, None, :],
        q_a_norm=q_a_norm[:, None, :], kv_a_norm=kv_a_norm[:, None, :],
        w_q=w_q.astype(BF16), w_qrot=w_qrot.astype(BF16), w_kv=w_kv.astype(BF16),
        qg_n=qg[0], qg_r=qg[1], qg_s=qg[2], kg_n=kg[0], kg_r=kg[1], kg_s=kg[2],
        w_mla_o=w_mla_o.astype(BF16), w_lru_o=w_lru_o.astype(BF16), w_pool_o=w_pool_o.astype(BF16),
        conv_w=conv_w, conv_b=conv_b[:, None, :], lru_w=lru_w.astype(BF16), lru_b=lru_b,
        lru_lambda=lru_lambda, pool_w=pool_w.astype(BF16), pool_scale=pool_scale[:, None, :],
        w_out=w_out.astype(BF16), w_ff1=w_ff1.astype(BF16), w_ff2=w_ff2.astype(BF16),
    )


def _rope_tables(T):
    rows = T // GRID_W
    row = jnp.repeat(jnp.arange(rows), GRID_W).astype(F32)
    col = jnp.tile(jnp.arange(GRID_W), rows).astype(F32)
    n_freq = QK_ROPE // 4
    inv = ROPE_BASE ** (-(jnp.arange(n_freq, dtype=F32) / n_freq))
    ang = jnp.concatenate([row[:, None] * inv, col[:, None] * inv], axis=-1)
    cos, sin = jnp.cos(ang), jnp.sin(ang)
    return (_pad_lanes(jnp.concatenate([cos, cos], axis=-1)), _pad_lanes(jnp.concatenate([sin, sin], axis=-1)))


def _trunk_layer(x2d, B, T, mod4, row_of, shared_cond, l, W, rope, ctx, h0):
    M = B * T
    span = M if shared_cond else T
    tm = _pick(span, (1024, 512, 256, 128))
    tmp = _pick(T, (512, 256, 128))
    z, gates = _in_proj(x2d, mod4, row_of, W["norm1"], W["w_in"], l, tm)
    z3 = z.reshape(B, T, -1)

    q, k, v, ckv = _qk_prep(z, B, T, l, W, rope, tmp)
    if ctx is None:
        kc = vc = None
    else:
        ckv_c, kr_c = ctx
        P = ckv_c.shape[2]
        kc, vc = _cache_k_prep(ckv_c, kr_c, B, P, l, W, _pick(P, (512, 256, 128)))
    attn = _attention(q, k, v, kc, vc, _pick(T, (ATTN_STEP_ROWS, 1024, 512, 256, 128)))

    lru, state = _lru_mixer(z3, h0, l, W, _pick(T, (256, 128)))
    pool = _pool_mixer(z3, l, W)

    merged = _merge(attn.reshape(M, -1), lru.reshape(M, -1), pool.reshape(M, -1), gates, l, W, tm)
    x2d = _out_proj(merged, x2d, mod4, row_of, l, W, tm)
    x2d = _ffn(x2d, mod4, row_of, l, W, _pick(span, (FFN_ROW_TILE, 512, 256, 128)))
    return x2d, z, ckv, state


def kernel(x_prompt, x_sample, c, cache_ckv, cache_krope, state_lru, c_ctx, w_mod, b_mod, norm1_w, norm2_w, w_in, q_a_norm, w_qb, kv_a_norm, w_kvb, q_norm, k_norm, w_mla_o, conv_w, conv_b, lru_wa, lru_ba, lru_wx, lru_bx, lru_lambda, w_lru_o, pool_w, pool_scale, w_pool_o, w_out, w_ff1, w_ff2):
    L = w_in.shape[0]
    Bc, Tc, D = x_prompt.shape
    Bs, Ts, _ = x_sample.shape

    W = _pack_weights(w_in, q_a_norm, w_qb, kv_a_norm, w_kvb, q_norm, k_norm, w_mla_o, conv_w, conv_b,
                      lru_wa, lru_ba, lru_wx, lru_bx, lru_lambda, w_lru_o, pool_w, pool_scale, w_pool_o,
                      w_out, w_ff1, w_ff2, norm1_w, norm2_w)

    rows = Bs + 1
    rpad = -rows % 8
    cond = jnp.concatenate([c, c_ctx[None, :], jnp.zeros((rpad, D), F32)], axis=0)
    mod = _modulation(cond, w_mod, b_mod)
    mod4 = mod.reshape(L, rows + rpad, 1, N_MOD * D)

    rope = _rope_tables(Ts)
    kr_cache = _pad_lanes(cache_krope)
    zero_state = jnp.zeros((Bc, 2, LRU_WIDTH), F32)

    row_smp = lambda t: t // Ts
    row_ctx = lambda t: Bs

    xp = x_prompt.reshape(Bc * Tc, D)
    xs = x_sample.reshape(Bs * Ts, D)
    ckv_list, kr_list, st_list = [], [], []
    for l in range(L):
        xp, zc, ckv, st = _trunk_layer(xp, Bc, Tc, mod4, row_ctx, True, l, W, None, None, zero_state)
        ckv_list.append(ckv.reshape(Bc, Tc, KV_LORA))
        kr_list.append(zc[:, OFF_KR:OFF_KR + QK_ROPE].reshape(Bc, Tc, QK_ROPE))
        st_list.append(st)
        xs, _, _, _ = _trunk_layer(xs, Bs, Ts, mod4, row_smp, False, l, W, rope,
                                   (cache_ckv, kr_cache), state_lru[:, l])
    return (xp.reshape(Bc, Tc, D), xs.reshape(Bs, Ts, D), jnp.stack(ckv_list, axis=1),
            jnp.stack(kr_list, axis=1), jnp.stack(st_list, axis=1))
```

```python
import functools

import jax
import jax.numpy as jnp
from jax import lax
from jax.experimental import pallas as pl
from jax.experimental.pallas import tpu as pltpu

F32 = jnp.float32
BF16 = jnp.bfloat16

EPS = 1e-6
GRID_W = 64
N_HEADS = 8
QK_NOPE = 128
QK_ROPE = 64
V_DIM = 128
QK_DIM = QK_NOPE + QK_ROPE
Q_LORA = 512
KV_LORA = 256
ROPE_BASE = 10000.0
LRU_WIDTH = 1024
LRU_C = 8.0
POOL_WIDTH = 1024
POOL_WINDOWS = (2, 4, 8, 16)
N_MOD = 6

LANE = 128
SUBLANE = 8
MXU_TILE = 256
VMEM_LIMIT = 56 * 2**20
MIXER_STEP_ROWS = 1024
FFN_ROW_TILE = 512
FFN_HIDDEN_TILE = 1024
ATTN_STEP_ROWS = 4096
ATTN_CHAIN_ROWS = 256
ATTN_LOOKAHEAD = 2
LOG2E = 1.4426950408889634
Q_PRESCALE = QK_DIM ** -0.5 * LOG2E

OFF_Q = 0
OFF_KV = Q_LORA
OFF_KR = OFF_KV + KV_LORA
OFF_KROT = OFF_KR + LANE
OFF_LRU = OFF_KROT + LANE
OFF_GATE = OFF_LRU + LRU_WIDTH
OFF_POOL = OFF_GATE + LRU_WIDTH
OFF_BR = OFF_POOL + POOL_WIDTH


def _params(sem):
    return pltpu.CompilerParams(dimension_semantics=sem, vmem_limit_bytes=VMEM_LIMIT)


def _pick(n, prefs):
    for p in prefs:
        if n % p == 0:
            return p
    return n


def _mod_kernel(c_ref, w_ref, b_ref, o_ref):
    c = c_ref[...]
    s = c * jax.nn.sigmoid(c)
    o_ref[...] = jnp.dot(s.astype(BF16), w_ref[...].astype(BF16), preferred_element_type=F32) + b_ref[...]


def _modulation(cond, w_mod, b_mod):
    L, D, N = w_mod.shape
    R = cond.shape[0]
    tn = _pick(N, (1024, 512, 256, 128))
    return pl.pallas_call(
        _mod_kernel,
        grid=(L, N // tn),
        in_specs=[
            pl.BlockSpec((R, D), lambda l, j: (0, 0)),
            pl.BlockSpec((None, D, tn), lambda l, j: (l, 0, j)),
            pl.BlockSpec((None, 1, tn), lambda l, j: (l, 0, j)),
        ],
        out_specs=pl.BlockSpec((None, R, tn), lambda l, j: (l, 0, j)),
        out_shape=jax.ShapeDtypeStruct((L, R, N), F32),
        compiler_params=_params(("arbitrary", "arbitrary")),
        name="modulation",
    )(cond, w_mod, b_mod.reshape(L, 1, N))


def _inproj_kernel(x_ref, nw_ref, sh_ref, sc_ref, w_ref, z_ref, g_ref, h_scr, *, n_lin):
    j = pl.program_id(1)

    @pl.when(j == 0)
    def _():
        h_scr[...] = _norm_mod(x_ref[...], nw_ref[...], sc_ref[...], sh_ref[...])

    @pl.when(j < n_lin)
    def _():
        z_ref[...] = jnp.dot(h_scr[...], w_ref[...], preferred_element_type=F32)

    @pl.when(j >= n_lin)
    def _():
        g_ref[...] = _sigmoid(jnp.dot(h_scr[...], w_ref[...], preferred_element_type=F32)).astype(BF16)


def _in_proj(x2d, mod4, row_of, norm_w, w_in_p, l, tm):
    M, D = x2d.shape
    N = w_in_p.shape[-1]
    tn = _pick(OFF_BR, (1024, 512, 256))
    n_lin = OFF_BR // tn
    return pl.pallas_call(
        functools.partial(_inproj_kernel, n_lin=n_lin),
        grid=(M // tm, N // tn),
        in_specs=[
            pl.BlockSpec((tm, D), lambda i, j: (i, 0)),
            pl.BlockSpec((None, 1, D), lambda i, j: (l, 0, 0)),
            pl.BlockSpec((None, None, 1, D), lambda i, j: (l, row_of(i * tm), 0, 0)),
            pl.BlockSpec((None, None, 1, D), lambda i, j: (l, row_of(i * tm), 0, 1)),
            pl.BlockSpec((None, D, tn), lambda i, j: (l, 0, j)),
        ],
        out_specs=[
            pl.BlockSpec((tm, tn), lambda i, j: (i, jnp.minimum(j, n_lin - 1))),
            pl.BlockSpec((tm, tn), lambda i, j: (i, jnp.maximum(j - n_lin, 0))),
        ],
        out_shape=[
            jax.ShapeDtypeStruct((M, OFF_BR), F32),
            jax.ShapeDtypeStruct((M, N - OFF_BR), BF16),
        ],
        scratch_shapes=[pltpu.VMEM((tm, D), BF16)],
        compiler_params=_params(("arbitrary", "arbitrary")),
        name="in_proj",
    )(x2d, norm_w, mod4, mod4, w_in_p)


def _head_scale(nope, rope_sq_sum):
    ss = jnp.sum(nope * nope, axis=-1, keepdims=True) + rope_sq_sum
    return lax.rsqrt(ss * (1.0 / QK_DIM) + EPS)


def _qprep_kernel(*refs, use_rope):
    if use_rope:
        ql_ref, an_ref, w_ref, wrot_ref, gn_ref, gr_ref, gs_ref, cos_ref, sin_ref, q_ref = refs
    else:
        ql_ref, an_ref, w_ref, gn_ref, gr_ref, q_ref = refs
    ql = ql_ref[...]
    ms = jnp.mean(ql * ql, axis=-1, keepdims=True)
    qn = (ql * lax.rsqrt(ms + EPS) * an_ref[...]).astype(BF16)
    qq = jnp.dot(qn, w_ref[...], preferred_element_type=F32)
    if use_rope:
        qrot = jnp.dot(qn, wrot_ref[...], preferred_element_type=F32)
        cos = cos_ref[...]
        sin = sin_ref[...]
        gs = gs_ref[...]
    gn = gn_ref[...]
    gr = gr_ref[...]
    hw = N_HEADS * LANE
    for h in range(N_HEADS):
        nope = qq[:, h * LANE:(h + 1) * LANE]
        rope = qq[:, hw + h * LANE:hw + (h + 1) * LANE]
        s = _head_scale(nope, jnp.sum(rope * rope, axis=-1, keepdims=True))
        if use_rope:
            r = gr * rope * cos + gs * qrot[:, h * LANE:(h + 1) * LANE] * sin
        else:
            r = gr * rope
        s = s * Q_PRESCALE
        q_ref[h, :, 0:LANE] = (nope * s * gn).astype(BF16)
        q_ref[h, :, LANE:2 * LANE] = (r * s).astype(BF16)


def _q_prep_io(z, B, T, l, W, rope, tm):
    nb = T // tm
    use_rope = rope is not None
    wspec = lambda n: pl.BlockSpec((None, Q_LORA, n), lambda i: (l, 0, 0))
    vec = pl.BlockSpec((None, 1, LANE), lambda i: (l, 0, 0))
    in_specs = [
        pl.BlockSpec((tm, Q_LORA), lambda i: (i, OFF_Q // Q_LORA)),
        pl.BlockSpec((None, 1, Q_LORA), lambda i: (l, 0, 0)),
        wspec(2 * N_HEADS * LANE),
    ]
    args = [z, W["q_a_norm"], W["w_q"]]
    if use_rope:
        tab = pl.BlockSpec((tm, LANE), lambda i: (i % nb, 0))
        in_specs += [wspec(N_HEADS * LANE), vec, vec, vec, tab, tab]
        args += [W["w_qrot"], W["qg_n"], W["qg_r"], W["qg_s"], rope[0], rope[1]]
    else:
        in_specs += [vec, vec]
        args += [W["qg_n"], W["qg_r"]]
    out_specs = [pl.BlockSpec((None, N_HEADS, tm, 2 * LANE), lambda i: (i // nb, 0, i % nb, 0))]
    out_shape = [jax.ShapeDtypeStruct((B, N_HEADS, T, 2 * LANE), BF16)]
    return in_specs, args, out_specs, out_shape


def _kprep_kernel(*refs, use_rope, normalize):
    refs = list(refs)
    kv_ref = refs.pop(0)
    kr_ref = refs.pop(0)
    krot_ref = refs.pop(0) if use_rope else None
    an_ref = refs.pop(0) if normalize else None
    w_ref = refs.pop(0)
    gn_ref = refs.pop(0)
    gr_ref = refs.pop(0)
    if use_rope:
        gs_ref, cos_ref, sin_ref = refs.pop(0), refs.pop(0), refs.pop(0)
    k_ref = refs.pop(0)
    v_ref = refs.pop(0)
    ckv_ref = refs.pop(0) if normalize else None

    kv = kv_ref[...]
    if normalize:
        ms = jnp.mean(kv * kv, axis=-1, keepdims=True)
        kv = kv * lax.rsqrt(ms + EPS) * an_ref[...]
        ckv_ref[...] = kv
    kk = jnp.dot(kv.astype(BF16), w_ref[...], preferred_element_type=F32)
    kr = kr_ref[...]
    ssr = jnp.sum(kr * kr, axis=-1, keepdims=True)
    if use_rope:
        base = gr_ref[...] * kr * cos_ref[...] + gs_ref[...] * krot_ref[...] * sin_ref[...]
    else:
        base = gr_ref[...] * kr
    gn = gn_ref[...]
    hw = N_HEADS * LANE
    for h in range(N_HEADS):
        nope = kk[:, h * LANE:(h + 1) * LANE]
        s = _head_scale(nope, ssr)
        k_ref[h, :, 0:LANE] = (nope * s * gn).astype(BF16)
        k_ref[h, :, LANE:2 * LANE] = (base * s).astype(BF16)
        v_ref[h, :, 0:LANE] = kk[:, hw + h * LANE:hw + (h + 1) * LANE].astype(BF16)
        v_ref[h, :, LANE:2 * LANE] = jnp.ones((kk.shape[0], LANE), BF16)


def _k_prep_io(kv_src, kr_src, B, T, l, W, rope, tm, *, kv_spec, kr_spec, krot_spec, normalize):
    M = B * T
    nb = T // tm
    use_rope = rope is not None
    vec = pl.BlockSpec((None, 1, LANE), lambda i: (l, 0, 0))
    in_specs = [kv_spec, kr_spec]
    args = [kv_src, kr_src]
    if use_rope:
        in_specs.append(krot_spec)
        args.append(kr_src)
    if normalize:
        in_specs.append(pl.BlockSpec((None, 1, KV_LORA), lambda i: (l, 0, 0)))
        args.append(W["kv_a_norm"])
    in_specs += [pl.BlockSpec((None, KV_LORA, 2 * N_HEADS * LANE), lambda i: (l, 0, 0)), vec, vec]
    args += [W["w_kv"], W["kg_n"], W["kg_r"]]
    if use_rope:
        tab = pl.BlockSpec((tm, LANE), lambda i: (i % nb, 0))
        in_specs += [vec, tab, tab]
        args += [W["kg_s"], rope[0], rope[1]]
    out_specs = [
        pl.BlockSpec((None, N_HEADS, tm, 2 * LANE), lambda i: (i // nb, 0, i % nb, 0)),
        pl.BlockSpec((None, N_HEADS, tm, 2 * LANE), lambda i: (i // nb, 0, i % nb, 0)),
    ]
    out_shape = [
        jax.ShapeDtypeStruct((B, N_HEADS, T, 2 * LANE), BF16),
        jax.ShapeDtypeStruct((B, N_HEADS, T, 2 * LANE), BF16),
    ]
    if normalize:
        out_specs.append(pl.BlockSpec((tm, KV_LORA), lambda i: (i, 0)))
        out_shape.append(jax.ShapeDtypeStruct((M, KV_LORA), F32))
    return in_specs, args, out_specs, out_shape


def _cache_k_prep(ckv_c, kr_c, B, P, l, W, tp):
    npb = P // tp
    in_specs, args, out_specs, out_shape = _k_prep_io(
        ckv_c, kr_c, B, P, l, W, None, tp,
        kv_spec=pl.BlockSpec((None, None, tp, KV_LORA), lambda i: (i // npb, l, i % npb, 0)),
        kr_spec=pl.BlockSpec((None, None, tp, LANE), lambda i: (i // npb, l, i % npb, 0)),
        krot_spec=None, normalize=False)
    return pl.pallas_call(
        functools.partial(_kprep_kernel, use_rope=False, normalize=False),
        grid=(B * P // tp,),
        in_specs=in_specs,
        out_specs=out_specs,
        out_shape=out_shape,
        compiler_params=_params(("arbitrary",)),
        name="k_prep",
    )(*args)


def _qkprep_kernel(*refs, use_rope, n_q_in, n_k_in):
    q_in, k_in = refs[:n_q_in], refs[n_q_in:n_q_in + n_k_in]
    q_out, k_out = refs[n_q_in + n_k_in], refs[n_q_in + n_k_in + 1:]
    _qprep_kernel(*q_in, q_out, use_rope=use_rope)
    _kprep_kernel(*k_in, *k_out, use_rope=use_rope, normalize=True)


def _qk_prep(z, B, T, l, W, rope, tm):
    q_io = _q_prep_io(z, B, T, l, W, rope, tm)
    k_io = _k_prep_io(
        z, z, B, T, l, W, rope, tm,
        kv_spec=pl.BlockSpec((tm, KV_LORA), lambda i: (i, OFF_KV // KV_LORA)),
        kr_spec=pl.BlockSpec((tm, LANE), lambda i: (i, OFF_KR // LANE)),
        krot_spec=pl.BlockSpec((tm, LANE), lambda i: (i, OFF_KROT // LANE)),
        normalize=True)
    return pl.pallas_call(
        functools.partial(_qkprep_kernel, use_rope=rope is not None, n_q_in=len(q_io[1]), n_k_in=len(k_io[1])),
        grid=(B * T // tm,),
        in_specs=q_io[0] + k_io[0],
        out_specs=q_io[2] + k_io[2],
        out_shape=q_io[3] + k_io[3],
        compiler_params=_params(("arbitrary",)),
        name="qk_prep",
    )(*q_io[1], *k_io[1])


def _attn_kernel(*refs, has_ctx, nsub):
    if has_ctx:
        q_ref, k_ref, v_ref, kc_ref, vc_ref, o_ref = refs
    else:
        q_ref, k_ref, v_ref, o_ref = refs
    nt = (((1,), (1,)), ((), ()))
    hb, tq, _ = q_ref.shape
    ts = tq // nsub

    def key_tiles(h):
        out = []
        for kref, vref in ((k_ref, v_ref), (kc_ref, vc_ref)) if has_ctx else ((k_ref, v_ref),):
            kt = min(MXU_TILE, kref.shape[1])
            out += [(kref.at[h], vref.at[h], t * kt, kt) for t in range(kref.shape[1] // kt)]
        return out

    def fold_lanes(x, op):
        out = x[:, :LANE]
        for c in range(1, x.shape[1] // LANE):
            out = op(out, x[:, c * LANE:(c + 1) * LANE])
        return out

    def scores(h, r):
        q = q_ref[h, r * ts:(r + 1) * ts, :]
        s_tiles = []
        mrun = None
        for kref, _, off, kt in key_tiles(h):
            st = lax.dot_general(q, kref[off:off + kt, :], nt, preferred_element_type=F32)
            s_tiles.append(st)
            tmax = fold_lanes(st, jnp.maximum)
            mrun = tmax if mrun is None else jnp.maximum(mrun, tmax)
        return s_tiles, jnp.max(mrun, axis=-1, keepdims=True)

    def weights_and_values(h, r, s_tiles, m):
        o = None
        for st, (_, vref, off, kt) in zip(s_tiles, key_tiles(h)):
            p = jnp.exp2(st - m)
            ot = jnp.dot(p.astype(BF16), vref[off:off + kt, :], preferred_element_type=F32)
            o = ot if o is None else o + ot
        o_ref[r * ts:(r + 1) * ts, h * LANE:(h + 1) * LANE] = (o[:, :LANE] / o[:, LANE:]).astype(BF16)

    chains = [(h, r) for h in range(hb) for r in range(nsub)]
    ahead = min(ATTN_LOOKAHEAD, len(chains))
    pending = [scores(*c) for c in chains[:ahead]]
    for n, c in enumerate(chains):
        s_tiles, m = pending.pop(0)
        if n + ahead < len(chains):
            pending.append(scores(*chains[n + ahead]))
        weights_and_values(*c, s_tiles, m)


def _attention(q, k, v, kc, vc, tq):
    B, H, T, _ = q.shape
    S = k.shape[2]
    has_ctx = kc is not None
    hb = max(1, min(H, ATTN_STEP_ROWS // tq))
    in_specs = [
        pl.BlockSpec((None, hb, tq, 2 * LANE), lambda b, h, i: (b, h, i, 0)),
        pl.BlockSpec((None, hb, S, 2 * LANE), lambda b, h, i: (b, h, 0, 0)),
        pl.BlockSpec((None, hb, S, 2 * LANE), lambda b, h, i: (b, h, 0, 0)),
    ]
    args = [q, k, v]
    if has_ctx:
        P = kc.shape[2]
        in_specs += [
            pl.BlockSpec((None, hb, P, 2 * LANE), lambda b, h, i: (b, h, 0, 0)),
            pl.BlockSpec((None, hb, P, 2 * LANE), lambda b, h, i: (b, h, 0, 0)),
        ]
        args += [kc, vc]
    nsub = max(1, tq // ATTN_CHAIN_ROWS)
    return pl.pallas_call(
        functools.partial(_attn_kernel, has_ctx=has_ctx, nsub=nsub),
        grid=(B, H // hb, T // tq),
        in_specs=in_specs,
        out_specs=pl.BlockSpec((None, tq, hb * LANE), lambda b, h, i: (b, i, h)),
        out_shape=jax.ShapeDtypeStruct((B, T, H * LANE), BF16),
        compiler_params=_params(("arbitrary", "arbitrary", "arbitrary")),
        name="attention",
    )(*args)


def _shift_rows(x, k, sub):
    n = x.shape[0]
    rolled = pltpu.roll(x, k % n, 0)
    if k > 0:
        head = jnp.where(sub >= k, rolled[:SUBLANE], 0.0)
        return jnp.concatenate([head, rolled[SUBLANE:]], axis=0)
    tail = jnp.where(sub < SUBLANE + k, rolled[n - SUBLANE:], 0.0)
    return jnp.concatenate([rolled[:n - SUBLANE], tail], axis=0)


def _scan_chunk(a, b, carry, sub, reverse):
    n = a.shape[0]
    ng = n // SUBLANE
    hs = [None] * ng
    for j in (range(ng - 1, -1, -1) if reverse else range(ng)):
        aj = a[j * SUBLANE:(j + 1) * SUBLANE]
        bj = b[j * SUBLANE:(j + 1) * SUBLANE]
        d = 1
        while d < SUBLANE:
            shift = SUBLANE - d if reverse else d
            valid = (sub < SUBLANE - d) if reverse else (sub >= d)
            a_s = pltpu.roll(aj, shift, 0)
            b_s = pltpu.roll(bj, shift, 0)
            bj = jnp.where(valid, aj * b_s, 0.0) + bj
            aj = jnp.where(valid, aj * a_s, aj)
            d *= 2
        h = aj * carry + bj
        hs[j] = h
        carry = h[0:1] if reverse else h[SUBLANE - 1:SUBLANE]
    return jnp.concatenate(hs, axis=0), carry


def _sigmoid(x):
    return 0.5 * jnp.tanh(0.5 * x) + 0.5


def _softplus(x):
    return jnp.maximum(x, 0.0) + jnp.log(1.0 + jnp.exp(-jnp.abs(x)))


def _gelu_tanh(x):
    return 0.5 * x * (1.0 + jnp.tanh(0.7978845608028654 * (x + 0.044715 * x * x * x)))


def _lru_kernel(u_ref, ug_ref, cw_ref, cb_ref, w_ref, b_ref, lam_ref, h0_ref, y_ref, st_ref,
                xc_scr, hf_scr, hb_scr, *, tc):
    C = xc_scr.shape[1]
    for c in range(w_ref.shape[0]):
        cs = slice(c * C, (c + 1) * C)
        _lru_tile(u_ref.at[:, cs], ug_ref.at[:, cs], cw_ref.at[:, cs], cb_ref.at[:, cs], w_ref.at[c],
                  b_ref.at[c], lam_ref.at[:, cs], h0_ref.at[:, cs], y_ref.at[:, cs], st_ref.at[:, cs],
                  xc_scr, hf_scr, hb_scr, tc=tc)


def _lru_tile(u_ref, ug_ref, cw_ref, cb_ref, w_ref, b_ref, lam_ref, h0_ref, y_ref, st_ref,
              xc_scr, hf_scr, hb_scr, *, tc):
    T, C = xc_scr.shape
    nch = T // tc
    sub = lax.broadcasted_iota(jnp.int32, (SUBLANE, C), 0)
    u = u_ref[...]
    cw = cw_ref[...]
    xc = (cb_ref[...] + cw[0:1] * _shift_rows(u, 2, sub) + cw[1:2] * _shift_rows(u, 1, sub)
          + cw[2:3] * u + cw[3:4] * _shift_rows(u, -1, sub))
    xc_scr[...] = xc

    decay = LRU_C * _softplus(-lam_ref[...])
    h0 = h0_ref[...]

    def gates(xcj, d):
        g = jnp.dot(xcj.astype(BF16), w_ref[:, 2 * d * C:2 * (d + 1) * C], preferred_element_type=F32)
        g = g + b_ref[:, 2 * d * C:2 * (d + 1) * C]
        r = _sigmoid(g[:, :C])
        i = _sigmoid(g[:, C:])
        nla = r * decay[d:d + 1]
        a = jnp.exp2(nla * (-LOG2E))
        v = jnp.tanh(nla) * (a * a + 1.0)
        b = jnp.where(v > 0.0, v * lax.rsqrt(v), 0.0) * (i * xcj)
        return a, b

    def scan_dir(t0, carry, d):
        a, b = gates(xc_scr[pl.ds(t0, tc), :], d)
        return _scan_chunk(a, b, carry, sub, d == 1)

    def emit(t0, hf, hb):
        y = (hf + hb) * _gelu_tanh(ug_ref[pl.ds(t0, tc), :])
        y_ref[pl.ds(t0, tc), :] = y.astype(BF16)

    def meet(jj, carry):
        tf0 = pl.multiple_of(jj * tc, tc)
        tb0 = pl.multiple_of((nch - 1 - jj) * tc, tc)
        hf, cf = scan_dir(tf0, carry[0], 0)
        hb, cb = scan_dir(tb0, carry[1], 1)
        hf_scr[pl.ds(tf0, tc), :] = hf
        hb_scr[pl.ds(tb0, tc), :] = hb
        return cf, cb

    def part(jj, carry):
        tf0 = pl.multiple_of(jj * tc, tc)
        tb0 = pl.multiple_of((nch - 1 - jj) * tc, tc)
        hf, cf = scan_dir(tf0, carry[0], 0)
        hb, cb = scan_dir(tb0, carry[1], 1)
        emit(tf0, hf, hb_scr[pl.ds(tf0, tc), :])
        emit(tb0, hf_scr[pl.ds(tb0, tc), :], hb)
        return cf, cb

    carry = (h0[0:1], h0[1:2])
    if nch % 2 == 0:
        carry = lax.fori_loop(0, nch // 2, meet, carry)
        carry = lax.fori_loop(nch // 2, nch, part, carry)
    else:
        assert nch == 1
        hf, cf = scan_dir(0, carry[0], 0)
        hb, cb = scan_dir(0, carry[1], 1)
        emit(0, hf, hb)
        carry = (cf, cb)
    st_ref[0:1, :] = carry[0]
    st_ref[1:2, :] = carry[1]


def _lru_mixer(z3, h0, l, W, tc):
    B, T, _ = z3.shape
    C = MXU_TILE
    nct = LRU_WIDTH // C
    nc = max(1, min(nct, MIXER_STEP_ROWS // T))
    Cw = nc * C
    return pl.pallas_call(
        functools.partial(_lru_kernel, tc=tc),
        grid=(B, nct // nc),
        in_specs=[
            pl.BlockSpec((None, T, Cw), lambda b, c: (b, 0, OFF_LRU // Cw + c)),
            pl.BlockSpec((None, T, Cw), lambda b, c: (b, 0, OFF_GATE // Cw + c)),
            pl.BlockSpec((None, 4, Cw), lambda b, c: (l, 0, c)),
            pl.BlockSpec((None, 1, Cw), lambda b, c: (l, 0, c)),
            pl.BlockSpec((None, nc, C, 4 * C), lambda b, c: (l, c, 0, 0)),
            pl.BlockSpec((None, nc, 1, 4 * C), lambda b, c: (l, c, 0, 0)),
            pl.BlockSpec((None, 2, Cw), lambda b, c: (l, 0, c)),
            pl.BlockSpec((None, 2, Cw), lambda b, c: (b, 0, c)),
        ],
        out_specs=[
            pl.BlockSpec((None, T, Cw), lambda b, c: (b, 0, c)),
            pl.BlockSpec((None, 2, Cw), lambda b, c: (b, 0, c)),
        ],
        out_shape=[
            jax.ShapeDtypeStruct((B, T, LRU_WIDTH), BF16),
            jax.ShapeDtypeStruct((B, 2, LRU_WIDTH), F32),
        ],
        scratch_shapes=[pltpu.VMEM((T, C), F32)] * 3,
        compiler_params=_params(("arbitrary", "arbitrary")),
        name="lru_mixer",
    )(z3, z3, W["conv_w"], W["conv_b"], W["lru_w"], W["lru_b"], W["lru_lambda"], h0)


def _pool_group(u_ref, w_ref, sc_ref, o_ref, win):
    T, C = u_ref.shape
    sub = lax.broadcasted_iota(jnp.int32, (SUBLANE, C), 0)
    u = u_ref[...]
    s = u + _shift_rows(u, 1, sub)
    w = 4
    while w <= win:
        q = w // 4
        s = _shift_rows(s, q, sub) + _shift_rows(s, -q, sub)
        w *= 2
    half = win // 2
    assert half <= SUBLANE and T >= 2 * SUBLANE
    head_cnt = (jnp.minimum(sub + half, T) - jnp.maximum(sub - half, 0)).astype(F32)
    tail_row = sub + (T - SUBLANE)
    tail_cnt = (jnp.minimum(tail_row + half, T) - jnp.maximum(tail_row - half, 0)).astype(F32)
    mean = jnp.concatenate([s[:SUBLANE] / head_cnt, s[SUBLANE:T - SUBLANE] * (1.0 / win),
                            s[T - SUBLANE:] / tail_cnt], axis=0)
    d = mean - u
    y = jnp.dot(d.astype(BF16), w_ref[...], preferred_element_type=F32) * sc_ref[...]
    o_ref[...] = y.astype(BF16)


def _pool_kernel(u_ref, w_ref, sc_ref, o_ref):
    ng_step = w_ref.shape[0]
    C = w_ref.shape[1]
    if ng_step == len(POOL_WINDOWS):
        for gi, win in enumerate(POOL_WINDOWS):
            cs = slice(gi * C, (gi + 1) * C)
            _pool_group(u_ref.at[:, cs], w_ref.at[gi], sc_ref.at[:, cs], o_ref.at[:, cs], win)
    else:
        g = pl.program_id(1)
        for gi, win in enumerate(POOL_WINDOWS):
            @pl.when(g == gi)
            def _(win=win):
                _pool_group(u_ref, w_ref.at[0], sc_ref, o_ref, win)


def _pool_mixer(z3, l, W):
    B, T, _ = z3.shape
    C = MXU_TILE
    ng = POOL_WIDTH // C
    ngs = ng if ng * T <= MIXER_STEP_ROWS else 1
    Cw = ngs * C
    return pl.pallas_call(
        _pool_kernel,
        grid=(B, ng // ngs),
        in_specs=[
            pl.BlockSpec((None, T, Cw), lambda b, g: (b, 0, OFF_POOL // Cw + g)),
            pl.BlockSpec((None, ngs, C, C), lambda b, g: (l, g, 0, 0)),
            pl.BlockSpec((None, 1, Cw), lambda b, g: (l, 0, g)),
        ],
        out_specs=pl.BlockSpec((None, T, Cw), lambda b, g: (b, 0, g)),
        out_shape=jax.ShapeDtypeStruct((B, T, POOL_WIDTH), BF16),
        compiler_params=_params(("arbitrary", "arbitrary")),
        name="pool_mixer",
    )(z3, W["pool_w"], W["pool_scale"])


def _merge_kernel(a_ref, r_ref, p_ref, g0_ref, g1_ref, g2_ref, wa_ref, wr_ref, wp_ref, o_ref):
    ya = jnp.dot(a_ref[...], wa_ref[...], preferred_element_type=F32)
    yr = jnp.dot(r_ref[...], wr_ref[...], preferred_element_type=F32)
    yp = jnp.dot(p_ref[...], wp_ref[...], preferred_element_type=F32)
    m = g0_ref[...].astype(F32) * ya + g1_ref[...].astype(F32) * yr + g2_ref[...].astype(F32) * yp
    o_ref[...] = m.astype(BF16)


def _merge(attn, lru, pool, gates, l, W, tm):
    M, K = attn.shape
    D = W["w_mla_o"].shape[-1]
    tn = _pick(D, (1024, 512, 256))
    nbr = D // tn
    act = pl.BlockSpec((tm, K), lambda i, j: (i, 0))
    wsp = pl.BlockSpec((None, K, tn), lambda i, j: (l, 0, j))
    gsp = lambda k: pl.BlockSpec((tm, tn), lambda i, j: (i, k * nbr + j))
    return pl.pallas_call(
        _merge_kernel,
        grid=(M // tm, D // tn),
        in_specs=[act, act, act, gsp(0), gsp(1), gsp(2), wsp, wsp, wsp],
        out_specs=pl.BlockSpec((tm, tn), lambda i, j: (i, j)),
        out_shape=jax.ShapeDtypeStruct((M, D), BF16),
        compiler_params=_params(("arbitrary", "arbitrary")),
        name="merge",
    )(attn, lru, pool, gates, gates, gates, W["w_mla_o"], W["w_lru_o"], W["w_pool_o"])


def _outproj_kernel(m_ref, w_ref, x_ref, g_ref, o_ref):
    y = jnp.dot(m_ref[...], w_ref[...], preferred_element_type=F32)
    o_ref[...] = x_ref[...] + g_ref[...] * y


def _out_proj(merged, x2d, mod4, row_of, l, W, tm):
    M, D = x2d.shape
    tn = _pick(D, (1024, 512, 256))
    nbr = D // tn
    return pl.pallas_call(
        _outproj_kernel,
        grid=(M // tm, D // tn),
        in_specs=[
            pl.BlockSpec((tm, D), lambda i, j: (i, 0)),
            pl.BlockSpec((None, D, tn), lambda i, j: (l, 0, j)),
            pl.BlockSpec((tm, tn), lambda i, j: (i, j)),
            pl.BlockSpec((None, None, 1, tn), lambda i, j: (l, row_of(i * tm), 0, 2 * nbr + j)),
        ],
        out_specs=pl.BlockSpec((tm, tn), lambda i, j: (i, j)),
        out_shape=jax.ShapeDtypeStruct((M, D), F32),
        compiler_params=_params(("arbitrary", "arbitrary")),
        name="out_proj",
    )(merged, W["w_out"], x2d, mod4)


def _norm_mod(x, nw, sc, sh):
    ms = jnp.mean(x * x, axis=-1, keepdims=True)
    return (x * lax.rsqrt(ms + EPS) * (nw * (1.0 + sc)) + sh).astype(BF16)


def _ffn_kernel(x_ref, nw_ref, sh_ref, sc_ref, g_ref, w1_ref, w2_ref, o_ref, h_scr):
    f = pl.program_id(1)

    @pl.when(f == 0)
    def _():
        h_scr[...] = _norm_mod(x_ref[...], nw_ref[...], sc_ref[...], sh_ref[...])
        o_ref[...] = jnp.zeros_like(o_ref)

    a = jnp.dot(h_scr[...], w1_ref[...], preferred_element_type=F32)
    a = jnp.maximum(a, 0.0)
    o_ref[...] += jnp.dot((a * a).astype(BF16), w2_ref[...], preferred_element_type=F32)

    @pl.when(f == pl.num_programs(1) - 1)
    def _():
        o_ref[...] = x_ref[...] + g_ref[...] * o_ref[...]


def _ffn(x2d, mod4, row_of, l, W, tm):
    M, D = x2d.shape
    F = W["w_ff1"].shape[-1]
    tf = _pick(F, (FFN_HIDDEN_TILE, 256))
    modv = lambda k: pl.BlockSpec((None, None, 1, D), lambda i, f: (l, row_of(i * tm), 0, k))
    return pl.pallas_call(
        _ffn_kernel,
        grid=(M // tm, F // tf),
        in_specs=[
            pl.BlockSpec((tm, D), lambda i, f: (i, 0)),
            pl.BlockSpec((None, 1, D), lambda i, f: (l, 0, 0)),
            modv(3), modv(4), modv(5),
            pl.BlockSpec((None, D, tf), lambda i, f: (l, 0, f)),
            pl.BlockSpec((None, tf, D), lambda i, f: (l, f, 0)),
        ],
        out_specs=pl.BlockSpec((tm, D), lambda i, f: (i, 0)),
        out_shape=jax.ShapeDtypeStruct((M, D), F32),
        scratch_shapes=[pltpu.VMEM((tm, D), BF16)],
        compiler_params=_params(("arbitrary", "arbitrary")),
        name="ffn",
    )(x2d, W["norm2"], mod4, mod4, mod4, W["w_ff1"], W["w_ff2"])


def _pad_lanes(x, n=LANE):
    return jnp.pad(x, [(0, 0)] * (x.ndim - 1) + [(0, n - x.shape[-1])])


def _rot_half(x):
    h = x.shape[-1] // 2
    return jnp.concatenate([-x[..., h:], x[..., :h]], axis=-1)


def _swap_half(x):
    h = x.shape[-1] // 2
    return jnp.concatenate([x[..., h:], x[..., :h]], axis=-1)


def _norm_gains(g):
    gr = g[:, QK_NOPE:]
    return (g[:, None, :QK_NOPE], _pad_lanes(gr)[:, None, :], _pad_lanes(_swap_half(gr))[:, None, :])


def _block_diag_tiles(w, per):
    *lead, nb, s, _ = w.shape
    w = w.reshape(*lead, nb // per, per, s, s)
    t = jnp.einsum("...kij,km->...kimj", w, jnp.eye(per, dtype=w.dtype))
    return t.reshape(*lead, nb // per, per * s, per * s)


def _pack_weights(w_in, q_a_norm, w_qb, kv_a_norm, w_kvb, q_norm, k_norm, w_mla_o, conv_w, conv_b,
                  lru_wa, lru_ba, lru_wx, lru_bx, lru_lambda, w_lru_o, pool_w, pool_scale, w_pool_o,
                  w_out, w_ff1, w_ff2, norm1_w, norm2_w):
    L, D, _ = w_in.shape
    o_kr = Q_LORA + KV_LORA
    w_in = w_in.astype(BF16)
    wkr = w_in[..., o_kr:o_kr + QK_ROPE]
    w_in_p = jnp.concatenate(
        [w_in[..., :o_kr], _pad_lanes(wkr), _pad_lanes(_rot_half(wkr)), w_in[..., o_kr + QK_ROPE:]], axis=-1)

    wq = w_qb.reshape(L, Q_LORA, N_HEADS, QK_DIM)
    wq_rope = wq[..., QK_NOPE:]
    w_q = jnp.concatenate([wq[..., :QK_NOPE].reshape(L, Q_LORA, -1),
                           _pad_lanes(wq_rope).reshape(L, Q_LORA, -1)], axis=-1)
    w_qrot = _pad_lanes(_rot_half(wq_rope)).reshape(L, Q_LORA, -1)

    wkv = w_kvb.reshape(L, KV_LORA, N_HEADS, QK_NOPE + V_DIM)
    w_kv = jnp.concatenate([wkv[..., :QK_NOPE].reshape(L, KV_LORA, -1),
                            wkv[..., QK_NOPE:].reshape(L, KV_LORA, -1)], axis=-1)

    qg = _norm_gains(q_norm)
    kg = _norm_gains(k_norm)

    per = MXU_TILE // lru_wa.shape[-1]
    ta = _block_diag_tiles(lru_wa, per)
    tx = _block_diag_tiles(lru_wx, per)
    lru_w = jnp.concatenate([ta[:, 0], tx[:, 0], ta[:, 1], tx[:, 1]], axis=-1)
    nct = lru_w.shape[1]
    ba = lru_ba.reshape(L, 2, nct, 1, MXU_TILE)
    bx = lru_bx.reshape(L, 2, nct, 1, MXU_TILE)
    lru_b = jnp.concatenate([ba[:, 0], bx[:, 0], ba[:, 1], bx[:, 1]], axis=-1)

    return dict(
        w_in=w_in_p.astype(BF16), norm1=norm1_w[:, None, :], norm2=norm2_w[:, None, :],
        q_a_norm=q_a_norm[:, None, :], kv_a_norm=kv_a_norm[:, None, :],
        w_q=w_q.astype(BF16), w_qrot=w_qrot.astype(BF16), w_kv=w_kv.astype(BF16),
        qg_n=qg[0], qg_r=qg[1], qg_s=qg[2], kg_n=kg[0], kg_r=kg[1], kg_s=kg[2],
        w_mla_o=w_mla_o.astype(BF16), w_lru_o=w_lru_o.astype(BF16), w_pool_o=w_pool_o.astype(BF16),
        conv_w=conv_w, conv_b=conv_b[:, None, :], lru_w=lru_w.astype(BF16), lru_b=lru_b,
        lru_lambda=lru_lambda, pool_w=pool_w.astype(BF16), pool_scale=pool_scale[:, None, :],
        w_out=w_out.astype(BF16), w_ff1=w_ff1.astype(BF16), w_ff2=w_ff2.astype(BF16),
    )


def _rope_tables(T):
    rows = T // GRID_W
    row = jnp.repeat(jnp.arange(rows), GRID_W).astype(F32)
    col = jnp.tile(jnp.arange(GRID_W), rows).astype(F32)
    n_freq = QK_ROPE // 4
    inv = ROPE_BASE ** (-(jnp.arange(n_freq, dtype=F32) / n_freq))
    ang = jnp.concatenate([row[:, None] * inv, col[:, None] * inv], axis=-1)
    cos, sin = jnp.cos(ang), jnp.sin(ang)
    return (_pad_lanes(jnp.concatenate([cos, cos], axis=-1)), _pad_lanes(jnp.concatenate([sin, sin], axis=-1)))


def _trunk_layer(x2d, B, T, mod4, row_of, shared_cond, l, W, rope, ctx, h0):
    M = B * T
    span = M if shared_cond else T
    tm = _pick(span, (1024, 512, 256, 128))
    tmp = _pick(T, (512, 256, 128))
    z, gates = _in_proj(x2d, mod4, row_of, W["norm1"], W["w_in"], l, tm)
    z3 = z.reshape(B, T, -1)

    q, k, v, ckv = _qk_prep(z, B, T, l, W, rope, tmp)
    if ctx is None:
        kc = vc = None
    else:
        ckv_c, kr_c = ctx
        P = ckv_c.shape[2]
        kc, vc = _cache_k_prep(ckv_c, kr_c, B, P, l, W, _pick(P, (512, 256, 128)))
    attn = _attention(q, k, v, kc, vc, _pick(T, (ATTN_STEP_ROWS, 1024, 512, 256, 128)))

    lru, state = _lru_mixer(z3, h0, l, W, _pick(T, (256, 128)))
    pool = _pool_mixer(z3, l, W)

    merged = _merge(attn.reshape(M, -1), lru.reshape(M, -1), pool.reshape(M, -1), gates, l, W, tm)
    x2d = _out_proj(merged, x2d, mod4, row_of, l, W, tm)
    x2d = _ffn(x2d, mod4, row_of, l, W, _pick(span, (FFN_ROW_TILE, 512, 256, 128)))
    return x2d, z, ckv, state


def kernel(x_prompt, x_sample, c, cache_ckv, cache_krope, state_lru, c_ctx, w_mod, b_mod, norm1_w, norm2_w, w_in, q_a_norm, w_qb, kv_a_norm, w_kvb, q_norm, k_norm, w_mla_o, conv_w, conv_b, lru_wa, lru_ba, lru_wx, lru_bx, lru_lambda, w_lru_o, pool_w, pool_scale, w_pool_o, w_out, w_ff1, w_ff2):
    L = w_in.shape[0]
    Bc, Tc, D = x_prompt.shape
    Bs, Ts, _ = x_sample.shape

    W = _pack_weights(w_in, q_a_norm, w_qb, kv_a_norm, w_kvb, q_norm, k_norm, w_mla_o, conv_w, conv_b,
                      lru_wa, lru_ba, lru_wx, lru_bx, lru_lambda, w_lru_o, pool_w, pool_scale, w_pool_o,
                      w_out, w_ff1, w_ff2, norm1_w, norm2_w)

    rows = Bs + 1
    rpad = -rows % 8
    cond = jnp.concatenate([c, c_ctx[None, :], jnp.zeros((rpad, D), F32)], axis=0)
    mod = _modulation(cond, w_mod, b_mod)
    mod4 = mod.reshape(L, rows + rpad, 1, N_MOD * D)

    rope = _rope_tables(Ts)
    kr_cache = _pad_lanes(cache_krope)
    zero_state = jnp.zeros((Bc, 2, LRU_WIDTH), F32)

    row_smp = lambda t: t // Ts
    row_ctx = lambda t: Bs

    xp = x_prompt.reshape(Bc * Tc, D)
    xs = x_sample.reshape(Bs * Ts, D)
    ckv_list, kr_list, st_list = [], [], []
    for l in range(L):
        xp, zc, ckv, st = _trunk_layer(xp, Bc, Tc, mod4, row_ctx, True, l, W, None, None, zero_state)
        ckv_list.append(ckv.reshape(Bc, Tc, KV_LORA))
        kr_list.append(zc[:, OFF_KR:OFF_KR + QK_ROPE].reshape(Bc, Tc, QK_ROPE))
        st_list.append(st)
        xs, _, _, _ = _trunk_layer(xs, Bs, Ts, mod4, row_smp, False, l, W, rope,
                                   (cache_ckv, kr_cache), state_lru[:, l])
    return (xp.reshape(Bc, Tc, D), xs.reshape(Bs, Ts, D), jnp.stack(ckv_list, axis=1),
            jnp.stack(kr_list, axis=1), jnp.stack(st_list, axis=1))
```
